```python
import math
import jax
import jax.numpy as jnp
from jax import lax
import numpy as np

D_MODEL = 1024
BATCH = 32
SEQ = 2048
DEPTH = 1

GRID_W = 64
CTX_LEN = 256
EPS = 1e-6
HY_WIDTH = D_MODEL // 2
HY_ORDER = 2
SHORT_CONV = 3
FILT_BANDS = 8
FILT_EMB = 1 + 2 * FILT_BANDS
FILT_HIDDEN = 64
DECAY_TARGET = 1e-2
FAST_DECAY_PCT = 0.3
SLOW_DECAY_PCT = 1.5
NA_HEADS = 8
NA_HEAD_DIM = 64
NA_WIDTH = NA_HEADS * NA_HEAD_DIM
NA_WIN_H = 8
NA_WIN_W = 16
NA_COL_BLOCK = 16
NA_COL_BAND = 32
N_EXPERTS = 16
EC_CAPACITY = 2
D_FF_EXPERT = 2816
COL_Q = 3 * HY_WIDTH
COL_K = COL_Q + NA_WIDTH
COL_V = COL_K + NA_WIDTH
COL_G_HY = COL_V + NA_WIDTH
COL_G_NA = COL_G_HY + D_MODEL
P_IN = COL_G_NA + D_MODEL

kernel_name = 'hybrid_hyena_natten_ec_dit_block'


def _rmsnorm(x, g):
    x32 = x.astype(jnp.float32)
    y = x32 * lax.rsqrt(jnp.mean(x32 * x32, axis=-1, keepdims=True) + EPS)
    return y.astype(x.dtype) * g


def _modulate(h, shift, scale):
    return h * (1.0 + scale) + shift


def _heads(t):
    return t.reshape(t.shape[0], t.shape[1], NA_HEADS, NA_HEAD_DIM)


def _short_conv(u, w, b):
    L = u.shape[1]
    pad = SHORT_CONV // 2
    up = jnp.pad(u, ((0, 0), (pad, pad), (0, 0)))
    y = b
    for j in range(SHORT_CONV):
        y = y + up[:, j:j + L] * w[j]
    return y


def _hyena_filters(L, w1, b1, w2, b2, w3, freq):
    pos = jnp.arange(L, dtype=jnp.float32)
    t = pos / max(L - 1, 1)
    omega = 2.0 * math.pi * pos / L
    bands = jnp.linspace(1e-4, FILT_BANDS - 1, FILT_BANDS, dtype=jnp.float32)
    ang = omega[:, None] * bands[None, :]
    feats = jnp.concatenate([t[:, None], jnp.cos(ang), -jnp.sin(ang)], axis=-1)
    h = jnp.sin(freq * (feats @ w1 + b1))
    h = jnp.sin(freq * (h @ w2 + b2))
    h = (h @ w3).astype(jnp.float32).reshape(L, 2, HY_ORDER, HY_WIDTH)
    max_decay = math.log(DECAY_TARGET) / FAST_DECAY_PCT
    min_decay = math.log(DECAY_TARGET) / SLOW_DECAY_PCT
    deltas = jnp.linspace(min_decay, max_decay, HY_WIDTH, dtype=jnp.float32)
    decay = jnp.exp(-t[:, None] * jnp.abs(deltas)[None, :])
    return h * decay[:, None, None, :]


def _bidir_long_conv(u, h_past, h_future, skip):
    L = u.shape[1]
    k = jnp.concatenate([h_past, jnp.zeros_like(h_past[:1]), h_future[:0:-1]], axis=0)
    u32 = u.astype(jnp.float32)
    U = jnp.fft.rfft(u32, n=2 * L, axis=1)
    K = jnp.fft.rfft(k, axis=0)
    y = jnp.fft.irfft(U * K[None], n=2 * L, axis=1)[:, :L]
    return (y + u32 * skip.astype(jnp.float32)).astype(u.dtype)


def _hyena_mixer(z, short_w, short_b, skip, w1, b1, w2, b2, w3, freq):
    u = _short_conv(z, short_w, short_b)
    v, x1, x2 = jnp.split(u, 3, axis=-1)
    filt = _hyena_filters(z.shape[1], w1, b1, w2, b2, w3, freq)
    y = x1 * _bidir_long_conv(v, filt[:, 0, 0], filt[:, 1, 0], skip[0])
    y = x2 * _bidir_long_conv(y, filt[:, 0, 1], filt[:, 1, 1], skip[1])
    return y


def _na_latent(q, k, v, kc, vc, rpb):
    B, L = q.shape[0], q.shape[1]
    rows = L // GRID_W
    kh = min(NA_WIN_H, rows)
    n_cb = GRID_W // NA_COL_BLOCK
    n_loc = kh * NA_COL_BAND
    scale = NA_HEAD_DIM ** -0.5
    qg = q.reshape(B, rows, GRID_W, NA_HEADS, NA_HEAD_DIM)
    kg = k.reshape(B, rows, GRID_W, NA_HEADS, NA_HEAD_DIM)
    vg = v.reshape(B, rows, GRID_W, NA_HEADS, NA_HEAD_DIM)
    q_cols = np.arange(GRID_W).reshape(n_cb, NA_COL_BLOCK)
    band_start = np.clip(q_cols[:, 0] - NA_WIN_W // 2, 0, GRID_W - NA_COL_BAND)
    band_cols = band_start[:, None] + np.arange(NA_COL_BAND)[None, :]
    win_start = np.clip(q_cols - NA_WIN_W // 2, 0, GRID_W - NA_WIN_W)
    kcol = band_cols[:, None, :]
    in_win = (kcol >= win_start[:, :, None]) & (kcol < win_start[:, :, None] + NA_WIN_W)
    ci = np.clip(kcol - q_cols[:, :, None] + NA_WIN_W - 1, 0, 2 * NA_WIN_W - 2)

    def row_block(r):
        rs = jnp.clip(r - NA_WIN_H // 2, 0, rows - kh)
        q_r = lax.dynamic_index_in_dim(qg, r, axis=1, keepdims=False)
        q_r = q_r.reshape(B, n_cb, NA_COL_BLOCK, NA_HEADS, NA_HEAD_DIM)
        k_band = lax.dynamic_slice_in_dim(kg, rs, kh, axis=1)[:, :, band_cols]
        v_band = lax.dynamic_slice_in_dim(vg, rs, kh, axis=1)[:, :, band_cols]
        s_loc = jnp.einsum('bnqhd,bknchd->bhnqkc', q_r, k_band).astype(jnp.float32) * scale
        ri = rs + jnp.arange(kh) - r + NA_WIN_H - 1
        bias = rpb[:, ri[None, None, :, None], ci[:, :, None, :]].astype(jnp.float32)
        s_loc = jnp.where(in_win[:, :, None, :], s_loc + bias, -jnp.inf)
        s_ctx = jnp.einsum('bnqhd,bmhd->bhnqm', q_r, kc).astype(jnp.float32) * scale
        s = jnp.concatenate([s_loc.reshape(B, NA_HEADS, n_cb, NA_COL_BLOCK, n_loc), s_ctx], axis=-1)
        p = jax.nn.softmax(s, axis=-1).astype(v.dtype)
        p_loc = p[..., :n_loc].reshape(B, NA_HEADS, n_cb, NA_COL_BLOCK, kh, NA_COL_BAND)
        o = (jnp.einsum('bhnqkc,bknchd->bnqhd', p_loc, v_band)
             + jnp.einsum('bhnqm,bmhd->bnqhd', p[..., n_loc:], vc))
        return o.reshape(B, GRID_W, NA_WIDTH)

    out = lax.map(row_block, jnp.arange(rows))
    return out.transpose(1, 0, 2, 3).reshape(B, L, NA_WIDTH)


def _ctx_attention(q, k, v):
    B, M = q.shape[0], q.shape[1]
    s = jnp.einsum('bqhd,bkhd->bhqk', q, k).astype(jnp.float32) * (NA_HEAD_DIM ** -0.5)
    p = jax.nn.softmax(s, axis=-1).astype(v.dtype)
    return jnp.einsum('bhqk,bkhd->bqhd', p, v).reshape(B, M, NA_WIDTH)


def _merge(z, hy_out, na_out, w_branch_hy, w_branch_na, w_out):
    a = hy_out @ w_branch_hy
    b = na_out @ w_branch_na
    g_hy = jax.nn.sigmoid(z[..., COL_G_HY:COL_G_NA])
    g_na = jax.nn.sigmoid(z[..., COL_G_NA:P_IN])
    return (g_hy * a + g_na * b) @ w_out


def _ec_ffn(h, w_router, w_gate, w_up, w_down):
    B, L, _ = h.shape
    cap = EC_CAPACITY * L // N_EXPERTS
    aff = jax.nn.softmax((h @ w_router).astype(jnp.float32), axis=-1)
    g, idx = lax.top_k(jnp.swapaxes(aff, 1, 2), cap)
    g_e, idx_e = jnp.swapaxes(g, 0, 1), jnp.swapaxes(idx, 0, 1)
    b_ix = jnp.arange(B)[None, :, None]
    x_e = h[b_ix, idx_e]

    def expert(args):
        xe, wg, wu, wd = args
        return (jax.nn.silu(xe @ wg) * (xe @ wu)) @ wd

    y_e = lax.map(expert, (x_e, w_gate, w_up, w_down))
    return jnp.zeros_like(h).at[b_ix, idx_e].add(g_e[..., None].astype(h.dtype) * y_e)


def setup_inputs(seed: int = 0) -> dict:
    key = jax.random.key(seed)
    ks = jax.random.split(key, 28)

    def nrm(k, shape, s):
        return jax.random.normal(k, shape, jnp.float32) * s

    return {
        'x': nrm(ks[0], (BATCH, SEQ, D_MODEL), 1.0),
        'c': nrm(ks[1], (BATCH, D_MODEL), 1.0),
        'ctx': nrm(ks[2], (BATCH, CTX_LEN, D_MODEL), 1.0),
        'c_ctx': nrm(ks[3], (D_MODEL,), 1.0),
        'w_mod': nrm(ks[4], (DEPTH, D_MODEL, 6 * D_MODEL), 0.2 * D_MODEL ** -0.5),
        'b_mod': nrm(ks[5], (DEPTH, 6 * D_MODEL), 0.01),
        'norm1_g': 1.0 + nrm(ks[6], (DEPTH, D_MODEL), 0.05),
        'norm2_g': 1.0 + nrm(ks[7], (DEPTH, D_MODEL), 0.05),
        'w_in': nrm(ks[8], (DEPTH, D_MODEL, P_IN), D_MODEL ** -0.5),
        'b_in': nrm(ks[9], (DEPTH, P_IN), 0.01),
        'hy_short_w': nrm(ks[10], (DEPTH, SHORT_CONV, 3 * HY_WIDTH), 0.5),
        'hy_short_b': nrm(ks[11], (DEPTH, 3 * HY_WIDTH), 0.01),
        'hy_skip': nrm(ks[12], (DEPTH, HY_ORDER, HY_WIDTH), 0.5),
        'filt_w1': nrm(ks[13], (DEPTH, FILT_EMB, FILT_HIDDEN), FILT_EMB ** -0.5),
        'filt_b1': nrm(ks[14], (DEPTH, FILT_HIDDEN), 0.1),
        'filt_w2': nrm(ks[15], (DEPTH, FILT_HIDDEN, FILT_HIDDEN), FILT_HIDDEN ** -0.5),
        'filt_b2': nrm(ks[16], (DEPTH, FILT_HIDDEN), 0.1),
        'filt_w3': nrm(ks[17], (DEPTH, FILT_HIDDEN, 2 * HY_ORDER * HY_WIDTH), 0.005),
        'filt_freq': 1.0 + nrm(ks[18], (DEPTH, FILT_HIDDEN), 0.1),
        'na_rpb': nrm(ks[19], (DEPTH, NA_HEADS, 2 * NA_WIN_H - 1, 2 * NA_WIN_W - 1), 0.1),
        'w_branch_hy': nrm(ks[20], (DEPTH, HY_WIDTH, D_MODEL), HY_WIDTH ** -0.5),
        'w_branch_na': nrm(ks[21], (DEPTH, NA_WIDTH, D_MODEL), NA_WIDTH ** -0.5),
        'w_out': nrm(ks[22], (DEPTH, D_MODEL, D_MODEL), D_MODEL ** -0.5),
        'w_router': nrm(ks[23], (DEPTH, D_MODEL, N_EXPERTS), D_MODEL ** -0.5),
        'w_gate': nrm(ks[24], (DEPTH, N_EXPERTS, D_MODEL, D_FF_EXPERT), D_MODEL ** -0.5),
        'w_up': nrm(ks[25], (DEPTH, N_EXPERTS, D_MODEL, D_FF_EXPERT), D_MODEL ** -0.5),
        'w_down': nrm(ks[26], (DEPTH, N_EXPERTS, D_FF_EXPERT, D_MODEL), D_FF_EXPERT ** -0.5),
        'final_g': 1.0 + nrm(ks[27], (D_MODEL,), 0.05),
    }


def reference(x, c, ctx, c_ctx, w_mod, b_mod, norm1_g, norm2_g, w_in, b_in, hy_short_w, hy_short_b,
              hy_skip, filt_w1, filt_b1, filt_w2, filt_b2, filt_w3, filt_freq, na_rpb, w_branch_hy,
              w_branch_na, w_out, w_router, w_gate, w_up, w_down, final_g):
    for i in range(DEPTH):
        last = i == DEPTH - 1
        hy_params = (hy_short_w[i], hy_short_b[i], hy_skip[i], filt_w1[i], filt_b1[i],
                     filt_w2[i], filt_b2[i], filt_w3[i], filt_freq[i])
        mod_x = jax.nn.silu(c) @ w_mod[i] + b_mod[i]
        mod_c = jax.nn.silu(c_ctx) @ w_mod[i] + b_mod[i]
        sh1, sc1, g1, sh2, sc2, g2 = jnp.split(mod_x[:, None, :], 6, axis=-1)
        csh1, csc1, cg1, csh2, csc2, cg2 = jnp.split(mod_c, 6)

        hx = _modulate(_rmsnorm(x, norm1_g[i]), sh1, sc1)
        hc = _modulate(_rmsnorm(ctx, norm1_g[i]), csh1, csc1)
        zx = hx @ w_in[i] + b_in[i]
        if last:
            kvc = hc @ w_in[i][:, COL_K:COL_G_HY] + b_in[i][COL_K:COL_G_HY]
            kc, vc = (_heads(t) for t in jnp.split(kvc, 2, axis=-1))
        else:
            zc = hc @ w_in[i] + b_in[i]
            kc, vc = _heads(zc[..., COL_K:COL_V]), _heads(zc[..., COL_V:COL_G_HY])

        hy_x = _hyena_mixer(zx[..., :COL_Q], *hy_params)
        na_x = _na_latent(_heads(zx[..., COL_Q:COL_K]), _heads(zx[..., COL_K:COL_V]),
                          _heads(zx[..., COL_V:COL_G_HY]), kc, vc, na_rpb[i])
        mix_x = _merge(zx, hy_x, na_x, w_branch_hy[i], w_branch_na[i], w_out[i])

        if not last:
            hy_c = _hyena_mixer(zc[..., :COL_Q], *hy_params)
            na_c = _ctx_attention(_heads(zc[..., COL_Q:COL_K]), kc, vc)
            mix_c = _merge(zc, hy_c, na_c, w_branch_hy[i], w_branch_na[i], w_out[i])
            ctx = ctx + cg1 * mix_c
            hc2 = _modulate(_rmsnorm(ctx, norm2_g[i]), csh2, csc2)
            ctx = ctx + cg2 * _ec_ffn(hc2, w_router[i], w_gate[i], w_up[i], w_down[i])

        x = x + g1 * mix_x
        hx2 = _modulate(_rmsnorm(x, norm2_g[i]), sh2, sc2)
        x = x + g2 * _ec_ffn(hx2, w_router[i], w_gate[i], w_up[i], w_down[i])
    return _rmsnorm(x, final_g)
```

```python
import functools
import math

import jax
import jax.numpy as jnp
import numpy as np
from jax import lax
from jax.experimental import pallas as pl
from jax.experimental.pallas import tpu as pltpu

F32 = jnp.float32
BF16 = jnp.bfloat16

EPS = 1e-6
GRID_W = 64
HY_ORDER = 2
SHORT_CONV = 3
FILT_BANDS = 8
DECAY_TARGET = 1e-2
FAST_DECAY_PCT = 0.3
SLOW_DECAY_PCT = 1.5
NA_HEADS = 8
NA_HEAD_DIM = 64
NA_WIN_H = 8
NA_WIN_W = 16
NA_ROWS_PER_STEP = 4
NA_SLAB_ROWS = NA_ROWS_PER_STEP + NA_WIN_H
N_EXPERTS = 16
EC_CAPACITY = 2

MASK_VALUE = -1e30
LANES = 128
VMEM_LIMIT = 56 * 1024 * 1024


def _cparams(*sem):
    return pltpu.CompilerParams(dimension_semantics=sem, vmem_limit_bytes=VMEM_LIMIT)


def _const_spec(shape):
    nd = len(shape)
    return pl.BlockSpec(shape, lambda *_: (0,) * nd, pipeline_mode=pl.Buffered(1))


def _dot(a, b):
    return jnp.dot(a, b, preferred_element_type=F32)


def _dot_nt(a, b):
    return lax.dot_general(a, b, (((1,), (1,)), ((), ())), preferred_element_type=F32)


def _mod_kernel(c_ref, w_ref, b_ref, o_ref):
    c = c_ref[...]
    s = c * jax.nn.sigmoid(c)
    o_ref[...] = _dot(s, w_ref[...]) + b_ref[...]


def _modulation(cc, w_mod, b_mod):
    rows, d = cc.shape
    n = w_mod.shape[1]
    tn = 1536
    return pl.pallas_call(
        _mod_kernel,
        grid=(n // tn,),
        in_specs=[
            pl.BlockSpec((rows, d), lambda j: (0, 0)),
            pl.BlockSpec((d, tn), lambda j: (0, j)),
            pl.BlockSpec((1, tn), lambda j: (0, j)),
        ],
        out_specs=pl.BlockSpec((rows, tn), lambda j: (0, j)),
        out_shape=jax.ShapeDtypeStruct((rows, n), F32),
        compiler_params=_cparams("arbitrary"),
        name="modulation",
    )(cc, w_mod, b_mod.reshape(1, n))


def _rms_mod(x, g, shift, scale):
    ms = jnp.mean(x * x, axis=-1, keepdims=True)
    return (x * lax.rsqrt(ms + EPS) * g) * (1.0 + scale) + shift


def _in_proj_kernel(x_ref, mod_ref, g_ref, w_ref, b_ref, o_ref, *, n_chunk):
    h = _rms_mod(x_ref[0], g_ref[...], mod_ref[0, 0:1, :], mod_ref[0, 1:2, :]).astype(BF16)
    n = w_ref.shape[1]
    for j in range(0, n, n_chunk):
        z = _dot(h, w_ref[:, j:j + n_chunk]) + b_ref[:, j:j + n_chunk]
        o_ref[0, :, j:j + n_chunk] = z.astype(o_ref.dtype)


def _in_proj(x, mod3, mod_row, g, w, b, tm):
    bsz, t, d = x.shape
    n = w.shape[1]
    return pl.pallas_call(
        functools.partial(_in_proj_kernel, n_chunk=min(n, 1024)),
        grid=(bsz, t // tm),
        in_specs=[
            pl.BlockSpec((1, tm, d), lambda i, j: (i, j, 0)),
            pl.BlockSpec((1, 6, d), lambda i, j: (mod_row(i), 0, 0)),
            _const_spec((1, d)),
            _const_spec((d, n)),
            _const_spec((1, n)),
        ],
        out_specs=pl.BlockSpec((1, tm, n), lambda i, j: (i, j, 0)),
        out_shape=jax.ShapeDtypeStruct((bsz, t, n), BF16),
        compiler_params=_cparams("parallel", "parallel"),
        name="in_proj",
    )(x, mod3, g.reshape(1, d), w, b.reshape(1, n))


def _dft_matrix(seq):
    n = 2 * seq
    i = jnp.arange(seq, dtype=jnp.int32)
    ft = (i[:, None] * i[None, :]) % n
    ang = ft.astype(F32) * (2.0 * math.pi / n)
    return jnp.concatenate([jnp.cos(ang), -jnp.sin(ang)], axis=1).astype(BF16)


def _alt_sign(shape):
    row = lax.broadcasted_iota(jnp.int32, shape, 0)
    return (1 - 2 * (row & 1)).astype(F32)


def _filter_kernel(feats_ref, w1_ref, b1_ref, w2_ref, b2_ref, freq_ref, w3p_ref, w3f_ref, decay_ref,
                   cs_ref, spec_ref, nyq_ref):
    seq = feats_ref.shape[0]
    n = 2 * seq
    freq = freq_ref[...]
    h = jnp.sin(freq * (_dot(feats_ref[...], w1_ref[...]) + b1_ref[...]))
    h = jnp.sin(freq * (_dot(h, w2_ref[...]) + b2_ref[...]))
    decay = decay_ref[...]
    h_past = _dot(h, w3p_ref[...]) * decay
    h_fut = _dot(h, w3f_ref[...]) * decay
    row = lax.broadcasted_iota(jnp.int32, h_fut.shape, 0)
    h_fut = jnp.where(row == 0, 0.0, h_fut)
    even = h_past + h_fut
    odd = h_past - h_fut
    wgt = jnp.where(row == 0, 1.0 / n, 2.0 / n)
    spec_ref[0, 0] = _dot(cs_ref[:, 0:seq], even.astype(BF16)) * wgt
    spec_ref[0, 1] = _dot(cs_ref[:, seq:n], odd.astype(BF16)) * wgt
    nyq_ref[0] = jnp.sum(_alt_sign(even.shape) * even, axis=0, keepdims=True) * (1.0 / n)


def _filter_spectra(seq, width, w1, b1, w2, b2, w3, freq, cs, cb):
    pos = jnp.arange(seq, dtype=F32)
    t = pos / max(seq - 1, 1)
    omega = 2.0 * math.pi * pos / seq
    bands = jnp.linspace(1e-4, FILT_BANDS - 1, FILT_BANDS, dtype=F32)
    ang = omega[:, None] * bands[None, :]
    feats = jnp.concatenate([t[:, None], jnp.cos(ang), -jnp.sin(ang)], axis=-1)
    emb, hid = w1.shape
    emb_pad = -(-emb // 8) * 8
    feats = jnp.pad(feats, ((0, 0), (0, emb_pad - emb)))
    w1 = jnp.pad(w1, ((0, emb_pad - emb), (0, 0)))
    max_decay = math.log(DECAY_TARGET) / FAST_DECAY_PCT
    min_decay = math.log(DECAY_TARGET) / SLOW_DECAY_PCT
    deltas = jnp.linspace(min_decay, max_decay, width, dtype=F32)
    decay = jnp.exp(-t[:, None] * jnp.abs(deltas)[None, :])
    ncb = width // cb
    return pl.pallas_call(
        _filter_kernel,
        grid=(HY_ORDER, ncb),
        in_specs=[
            _const_spec((seq, emb_pad)),
            _const_spec((emb_pad, hid)),
            _const_spec((1, hid)),
            _const_spec((hid, hid)),
            _const_spec((1, hid)),
            _const_spec((1, hid)),
            pl.BlockSpec((hid, cb), lambda o, c: (0, o * ncb + c)),
            pl.BlockSpec((hid, cb), lambda o, c: (0, (HY_ORDER + o) * ncb + c)),
            pl.BlockSpec((seq, cb), lambda o, c: (0, c)),
            _const_spec((seq, 2 * seq)),
        ],
        out_specs=[
            pl.BlockSpec((1, 2, seq, cb), lambda o, c: (o, 0, 0, c)),
            pl.BlockSpec((1, 1, cb), lambda o, c: (o, 0, c)),
        ],
        out_shape=[
            jax.ShapeDtypeStruct((HY_ORDER, 2, seq, width), F32),
            jax.ShapeDtypeStruct((HY_ORDER, 1, width), F32),
        ],
        compiler_params=_cparams("arbitrary", "arbitrary"),
        name="hyena_filters",
    )(feats, w1, b1.reshape(1, hid), w2, b2.reshape(1, hid), freq.reshape(1, hid), w3, w3, decay, cs)


def _hyena_kernel(zv_ref, z1_ref, z2_ref, sw_ref, sb_ref, skip_ref, spec_ref, nyq_ref, cs_ref, o_ref,
                  u_ref, g_ref, ub_ref, yri_ref, *, mc):
    seq, cb = zv_ref.shape[1], zv_ref.shape[2]
    row = lax.broadcasted_iota(jnp.int32, (seq, cb), 0)
    first, last = row == 0, row == seq - 1

    def short_conv(z_ref, g):
        z = z_ref[0].astype(F32)
        prev = jnp.where(first, 0.0, pltpu.roll(z, 1, 0))
        nxt = jnp.where(last, 0.0, pltpu.roll(z, seq - 1, 0))
        return sb_ref[g] + prev * sw_ref[g, 0:1, :] + z * sw_ref[g, 1:2, :] + nxt * sw_ref[g, 2:3, :]

    def long_conv(o, gate_ref, g):
        u = u_ref[...]
        ub_ref[...] = u.astype(BF16)
        nyq = jnp.sum((1 - 2 * (row & 1)).astype(F32) * u, axis=0, keepdims=True) * nyq_ref[o]
        sign = _alt_sign((mc, cb))
        for m in range(0, seq, mc):
            rows = slice(m, m + mc)
            re = _dot(cs_ref[rows, 0:seq], ub_ref[...])
            im = _dot(cs_ref[rows, seq:2 * seq], ub_ref[...])
            ka, kb = spec_ref[o, 0, rows, :], spec_ref[o, 1, rows, :]
            yri_ref[m:m + mc, :] = (re * ka - im * kb).astype(BF16)
            yri_ref[seq + m:seq + m + mc, :] = (re * kb + im * ka).astype(BF16)
        g_ref[...] = short_conv(gate_ref, g)
        for m in range(0, seq, mc):
            rows = slice(m, m + mc)
            y = _dot(cs_ref[rows, :], yri_ref[...])
            y = y + sign * nyq + u_ref[rows, :] * skip_ref[o]
            u_ref[rows, :] = g_ref[rows, :] * y

    u_ref[...] = short_conv(zv_ref, 0)
    long_conv(0, z1_ref, 1)
    long_conv(1, z2_ref, 2)
    o_ref[0] = u_ref[...].astype(o_ref.dtype)


def _hyena(zx, short_w, short_b, skip, spec, nyq, cs, width, cb, mc):
    bsz, seq, _ = zx.shape
    ncb = width // cb
    sw = short_w.reshape(SHORT_CONV, 3, width).transpose(1, 0, 2)
    sb = short_b.reshape(3, 1, width)
    skip3 = skip.reshape(HY_ORDER, 1, width)

    def zspec(g):
        return pl.BlockSpec((1, seq, cb), lambda c, i: (i, 0, g * ncb + c))

    return pl.pallas_call(
        functools.partial(_hyena_kernel, mc=mc),
        grid=(ncb, bsz),
        in_specs=[
            zspec(0), zspec(1), zspec(2),
            pl.BlockSpec((3, SHORT_CONV, cb), lambda c, i: (0, 0, c)),
            pl.BlockSpec((3, 1, cb), lambda c, i: (0, 0, c)),
            pl.BlockSpec((HY_ORDER, 1, cb), lambda c, i: (0, 0, c)),
            pl.BlockSpec((HY_ORDER, 2, seq, cb), lambda c, i: (0, 0, 0, c), pipeline_mode=pl.Buffered(1)),
            pl.BlockSpec((HY_ORDER, 1, cb), lambda c, i: (0, 0, c)),
            _const_spec((seq, 2 * seq)),
        ],
        out_specs=pl.BlockSpec((1, seq, cb), lambda c, i: (i, 0, c)),
        out_shape=jax.ShapeDtypeStruct((bsz, seq, width), BF16),
        scratch_shapes=[
            pltpu.VMEM((seq, cb), F32),
            pltpu.VMEM((seq, cb), F32),
            pltpu.VMEM((seq, cb), BF16),
            pltpu.VMEM((2 * seq, cb), BF16),
        ],
        compiler_params=_cparams("parallel", "parallel"),
        name="hyena",
    )(zx, zx, zx, sw, sb, skip3, spec, nyq, cs)


def _na_col_bias_kernel(rpb_ref, sel_ref, mask_ref, o_ref):
    acc = jnp.zeros(o_ref.shape, F32) + mask_ref[...]
    for ci in range(sel_ref.shape[0]):
        acc = acc + rpb_ref[:, ci:ci + 1] * sel_ref[ci:ci + 1, :]
    o_ref[...] = acc


def _na_bias_tables(rpb, rows):
    rq, sl, half = NA_ROWS_PER_STEP, NA_SLAB_ROWS, NA_WIN_H // 2
    n_ri, n_ci = 2 * NA_WIN_H - 1, 2 * NA_WIN_W - 1
    qc = np.arange(GRID_W)
    kc = np.arange(GRID_W)
    ws = np.clip(qc - NA_WIN_W // 2, 0, GRID_W - NA_WIN_W)
    in_win = (kc[None, :] >= ws[:, None]) & (kc[None, :] < ws[:, None] + NA_WIN_W)
    ci = np.clip(kc[None, :] - qc[:, None] + NA_WIN_W - 1, 0, n_ci - 1)
    sel = ((ci[None] == np.arange(n_ci)[:, None, None]) & in_win[None]).astype(np.float32)
    sel = sel.reshape(n_ci, GRID_W * GRID_W)
    mask = np.where(in_win, 0.0, MASK_VALUE).astype(np.float32).reshape(1, GRID_W * GRID_W)
    col = pl.pallas_call(
        _na_col_bias_kernel,
        out_shape=jax.ShapeDtypeStruct((NA_HEADS * n_ri, GRID_W * GRID_W), F32),
        name="na_col_bias",
    )(rpb.reshape(NA_HEADS * n_ri, n_ci), jnp.asarray(sel), jnp.asarray(mask))
    col = col.reshape(NA_HEADS, n_ri, GRID_W, GRID_W)

    n_steps = rows // rq
    tables = []
    for step in (0, 1, n_steps - 1):
        start = int(np.clip(step * rq - half, 0, rows - sl))
        per_row = []
        for i in range(rq):
            r = step * rq + i
            rs = int(np.clip(r - half, 0, rows - NA_WIN_H))
            ri0 = rs - r + NA_WIN_H - 1
            blk = col[:, ri0:ri0 + NA_WIN_H].transpose(0, 2, 1, 3)
            lo = rs - start
            blk = jnp.pad(blk, ((0, 0), (0, 0), (lo, sl - NA_WIN_H - lo), (0, 0)), constant_values=MASK_VALUE)
            per_row.append(blk.reshape(NA_HEADS, GRID_W, sl * GRID_W))
        tables.append(jnp.concatenate(per_row, axis=1))
    return jnp.stack(tables)


def _na_kernel(q_ref, k_ref, v_ref, kvc_ref, bias_ref, o_ref, *, rows):
    rq, sl = NA_ROWS_PER_STEP, NA_SLAB_ROWS
    width = NA_HEADS * NA_HEAD_DIM
    step = pl.program_id(1)
    start = jnp.clip(step * rq - NA_WIN_H // 2, 0, rows - sl)
    start = pl.multiple_of(start * GRID_W, GRID_W)
    lane = lax.broadcasted_iota(jnp.int32, (rq * GRID_W, LANES), 1)
    for hp in range(width // LANES):
        cols = slice(hp * LANES, (hp + 1) * LANES)
        q2 = q_ref[0, :, cols]
        k2 = k_ref[0, pl.ds(start, sl * GRID_W), cols]
        v2 = v_ref[0, pl.ds(start, sl * GRID_W), cols]
        kc2 = kvc_ref[0, :, cols]
        vc2 = kvc_ref[0, :, width + hp * LANES:width + (hp + 1) * LANES]
        outs = []
        for s in range(LANES // NA_HEAD_DIM):
            own = (lane >= s * NA_HEAD_DIM) & (lane < (s + 1) * NA_HEAD_DIM)
            qm = jnp.where(own, q2, jnp.zeros_like(q2))
            s_loc = _dot_nt(qm, k2) + bias_ref[0, hp * (LANES // NA_HEAD_DIM) + s]
            s_ctx = _dot_nt(qm, kc2)
            m = jnp.maximum(jnp.max(s_loc, axis=-1, keepdims=True), jnp.max(s_ctx, axis=-1, keepdims=True))
            p_loc = jnp.exp(s_loc - m)
            p_ctx = jnp.exp(s_ctx - m)
            den = jnp.sum(p_loc, axis=-1, keepdims=True) + jnp.sum(p_ctx, axis=-1, keepdims=True)
            o = _dot(p_loc.astype(BF16), v2) + _dot(p_ctx.astype(BF16), vc2)
            outs.append(jnp.where(own, o / den, 0.0))
        o_ref[0, :, cols] = sum(outs).astype(o_ref.dtype)


def _na(zx, kvc, bias, col_q):
    bsz, seq, _ = zx.shape
    rows = seq // GRID_W
    rq = NA_ROWS_PER_STEP
    assert rows % rq == 0 and rows >= NA_SLAB_ROWS and rows // rq >= 3
    n_steps = rows // rq
    width = NA_HEADS * NA_HEAD_DIM
    qb = col_q // width

    def cfg(s):
        return jnp.minimum(s, 1) + jnp.maximum(s - (n_steps - 2), 0)

    return pl.pallas_call(
        functools.partial(_na_kernel, rows=rows),
        grid=(bsz, n_steps),
        in_specs=[
            pl.BlockSpec((1, rq * GRID_W, width), lambda i, s: (i, s, qb)),
            pl.BlockSpec((1, seq, width), lambda i, s: (i, 0, qb + 1)),
            pl.BlockSpec((1, seq, width), lambda i, s: (i, 0, qb + 2)),
            pl.BlockSpec((1,) + kvc.shape[1:], lambda i, s: (i, 0, 0)),
            pl.BlockSpec((1,) + bias.shape[1:], lambda i, s: (cfg(s), 0, 0, 0)),
        ],
        out_specs=pl.BlockSpec((1, rq * GRID_W, width), lambda i, s: (i, s, 0)),
        out_shape=jax.ShapeDtypeStruct((bsz, seq, width), BF16),
        compiler_params=_cparams("parallel", "arbitrary"),
        name="na",
    )(zx, zx, zx, kvc, bias)


def _merge_kernel(hy_ref, na_ref, ghy_ref, gna_ref, x_ref, mod_ref, g2_ref, wbh_ref, wbn_ref, wo_ref, wr_ref,
                  x1_ref, h2_ref, lg_ref):
    a = _dot(hy_ref[0], wbh_ref[...])
    b = _dot(na_ref[0], wbn_ref[...])
    m = jax.nn.sigmoid(ghy_ref[0].astype(F32)) * a + jax.nn.sigmoid(gna_ref[0].astype(F32)) * b
    mix = _dot(m.astype(BF16), wo_ref[...])
    x1 = x_ref[0] + mod_ref[0, 2:3, :] * mix
    x1_ref[0] = x1
    h2 = _rms_mod(x1, g2_ref[...], mod_ref[0, 3:4, :], mod_ref[0, 4:5, :])
    h2_ref[0] = h2.astype(h2_ref.dtype)
    lg_ref[0] = _dot_nt(wr_ref[...], h2)


def _merge(hy, na, zx, x, mod3, g2, wbh, wbn, wo, wr_t, col_g, tm):
    bsz, seq, d = x.shape
    gb = col_g // d
    ne = wr_t.shape[0]
    return pl.pallas_call(
        _merge_kernel,
        grid=(bsz, seq // tm),
        in_specs=[
            pl.BlockSpec((1, tm, hy.shape[2]), lambda i, j: (i, j, 0)),
            pl.BlockSpec((1, tm, na.shape[2]), lambda i, j: (i, j, 0)),
            pl.BlockSpec((1, tm, d), lambda i, j: (i, j, gb)),
            pl.BlockSpec((1, tm, d), lambda i, j: (i, j, gb + 1)),
            pl.BlockSpec((1, tm, d), lambda i, j: (i, j, 0)),
            pl.BlockSpec((1, 6, d), lambda i, j: (i, 0, 0)),
            _const_spec((1, d)),
            _const_spec(wbh.shape),
            _const_spec(wbn.shape),
            _const_spec(wo.shape),
            _const_spec(wr_t.shape),
        ],
        out_specs=[
            pl.BlockSpec((1, tm, d), lambda i, j: (i, j, 0)),
            pl.BlockSpec((1, tm, d), lambda i, j: (i, j, 0)),
            pl.BlockSpec((1, ne, tm), lambda i, j: (i, 0, j)),
        ],
        out_shape=[
            jax.ShapeDtypeStruct((bsz, seq, d), F32),
            jax.ShapeDtypeStruct((bsz, seq, d), BF16),
            jax.ShapeDtypeStruct((bsz, ne, seq), F32),
        ],
        compiler_params=_cparams("parallel", "parallel"),
        name="merge",
    )(hy, na, zx, zx, x, mod3, g2.reshape(1, d), wbh, wbn, wo, wr_t)


def _route_kernel(lg_ref, tri_ref, rank_ref, rank_t_ref, gate_t_ref, *, cap):
    lg = lg_ref[0]
    ne, t = lg.shape
    e = jnp.exp(lg - jnp.max(lg, axis=0, keepdims=True))
    aff = e / jnp.sum(e, axis=0, keepdims=True)

    def bit_step(i, bits):
        cand = bits | (jnp.int32(1) << (30 - i))
        keep = jnp.sum((aff >= pltpu.bitcast(cand, F32)).astype(jnp.int32), axis=1, keepdims=True) >= cap
        return jnp.where(keep, cand, bits)

    thr = pltpu.bitcast(lax.fori_loop(0, 31, bit_step, jnp.zeros((ne, 1), jnp.int32)), F32)
    above = aff > thr
    tie = aff == thr
    need = cap - jnp.sum(above.astype(jnp.int32), axis=1, keepdims=True)
    tri = tri_ref[...]
    tie_before = _dot(tie.astype(BF16), tri)
    sel = above | (tie & (tie_before < need.astype(F32)))
    sel_before = _dot(sel.astype(BF16), tri)
    rank = jnp.where(sel, sel_before, -1.0)
    gate = jnp.where(sel, aff, 0.0)
    rank_ref[0] = rank.astype(jnp.int32)
    pad = jnp.full((LANES - ne, t), -1.0, F32)
    rank_t_ref[0] = jnp.concatenate([rank, pad], axis=0).T.astype(jnp.int32)
    gate_t_ref[0] = jnp.concatenate([gate, jnp.zeros((LANES - ne, t), F32)], axis=0).T


def _route(logits_t, cap):
    bsz, ne, t = logits_t.shape
    i = jnp.arange(t, dtype=jnp.int32)
    tri = (i[:, None] < i[None, :]).astype(BF16)
    return pl.pallas_call(
        functools.partial(_route_kernel, cap=cap),
        grid=(bsz,),
        in_specs=[
            pl.BlockSpec((1, ne, t), lambda b: (b, 0, 0)),
            _const_spec((t, t)),
        ],
        out_specs=[
            pl.BlockSpec((1, ne, t), lambda b: (b, 0, 0)),
            pl.BlockSpec((1, t, LANES), lambda b: (b, 0, 0)),
            pl.BlockSpec((1, t, LANES), lambda b: (b, 0, 0)),
        ],
        out_shape=[
            jax.ShapeDtypeStruct((bsz, ne, t), jnp.int32),
            jax.ShapeDtypeStruct((bsz, t, LANES), jnp.int32),
            jax.ShapeDtypeStruct((bsz, t, LANES), F32),
        ],
        compiler_params=_cparams("parallel"),
        name="route",
    )(logits_t, tri)


def _expert_kernel(rank_ref, h_ref, wg_ref, wu_ref, wd_ref, y_ref, *, cap):
    e = pl.program_id(0)
    t = h_ref.shape[1]
    rank = rank_ref[0, pl.ds(e, 1), :]
    slot = lax.broadcasted_iota(jnp.int32, (cap, t), 0)
    onehot = jnp.where(rank == slot, 1.0, 0.0).astype(BF16)
    xe = _dot(onehot, h_ref[0]).astype(BF16)
    g = _dot(xe, wg_ref[0])
    u = _dot(xe, wu_ref[0])
    act = (g * jax.nn.sigmoid(g) * u).astype(BF16)
    y_ref[0, 0] = _dot(act, wd_ref[0]).astype(y_ref.dtype)


def _experts(rank, h2, wg, wu, wd, cap):
    bsz, t, d = h2.shape
    ne, _, f = wg.shape
    return pl.pallas_call(
        functools.partial(_expert_kernel, cap=cap),
        grid=(ne, bsz),
        in_specs=[
            pl.BlockSpec((1, ne, t), lambda e, b: (b, 0, 0)),
            pl.BlockSpec((1, t, d), lambda e, b: (b, 0, 0)),
            pl.BlockSpec((1, d, f), lambda e, b: (e, 0, 0), pipeline_mode=pl.Buffered(1)),
            pl.BlockSpec((1, d, f), lambda e, b: (e, 0, 0), pipeline_mode=pl.Buffered(1)),
            pl.BlockSpec((1, f, d), lambda e, b: (e, 0, 0), pipeline_mode=pl.Buffered(1)),
        ],
        out_specs=pl.BlockSpec((1, 1, cap, d), lambda e, b: (b, e, 0, 0)),
        out_shape=jax.ShapeDtypeStruct((bsz, ne, cap, d), BF16),
        compiler_params=_cparams("arbitrary", "arbitrary"),
        name="experts",
    )(rank, h2, wg, wu, wd)


def _combine_kernel(rank_t_ref, gate_t_ref, y_ref, x1_ref, mod_ref, gf_ref, o_ref, *, cap):
    ne = y_ref.shape[1]
    tm = x1_ref.shape[1]
    slot = lax.broadcasted_iota(jnp.int32, (tm, cap), 1)
    acc = jnp.zeros(x1_ref.shape[1:], F32)
    for e in range(ne):
        r = rank_t_ref[0, :, e:e + 1]
        g = gate_t_ref[0, :, e:e + 1]
        scat = jnp.where(r == slot, g, 0.0).astype(BF16)
        acc = acc + _dot(scat, y_ref[0, e])
    x2 = x1_ref[0] + mod_ref[0, 5:6, :] * acc
    ms = jnp.mean(x2 * x2, axis=-1, keepdims=True)
    o_ref[0] = x2 * lax.rsqrt(ms + EPS) * gf_ref[...]


def _combine(rank_t, gate_t, y, x1, mod3, final_g, cap, tm):
    bsz, t, d = x1.shape
    ne = y.shape[1]
    return pl.pallas_call(
        functools.partial(_combine_kernel, cap=cap),
        grid=(bsz, t // tm),
        in_specs=[
            pl.BlockSpec((1, tm, LANES), lambda i, j: (i, j, 0)),
            pl.BlockSpec((1, tm, LANES), lambda i, j: (i, j, 0)),
            pl.BlockSpec((1, ne, cap, d), lambda i, j: (i, 0, 0, 0)),
            pl.BlockSpec((1, tm, d), lambda i, j: (i, j, 0)),
            pl.BlockSpec((1, 6, d), lambda i, j: (i, 0, 0)),
            _const_spec((1, d)),
        ],
        out_specs=pl.BlockSpec((1, tm, d), lambda i, j: (i, j, 0)),
        out_shape=jax.ShapeDtypeStruct((bsz, t, d), F32),
        compiler_params=_cparams("parallel", "arbitrary"),
        name="combine",
    )(rank_t, gate_t, y, x1, mod3, final_g.reshape(1, d))


def kernel(x, c, ctx, c_ctx, w_mod, b_mod, norm1_g, norm2_g, w_in, b_in, hy_short_w, hy_short_b, hy_skip, filt_w1, filt_b1, filt_w2, filt_b2, filt_w3, filt_freq, na_rpb, w_branch_hy, w_branch_na, w_out, w_router, w_gate, w_up, w_down, final_g):
    depth = w_mod.shape[0]
    bsz, seq, d = x.shape
    hy_width = w_branch_hy.shape[1]
    na_width = w_branch_na.shape[1]
    col_q = 3 * hy_width
    col_k = col_q + na_width
    col_g = col_q + 3 * na_width
    cap = EC_CAPACITY * seq // N_EXPERTS
    rows = seq // GRID_W
    mod_rows = -(-(bsz + 1) // 8) * 8

    cs = _dft_matrix(seq)
    for i in range(depth):
        assert i == depth - 1, "only the final layer's data flow (context feeds keys/values only) is implemented"
        cc = jnp.zeros((mod_rows, d), F32).at[:bsz].set(c).at[bsz].set(c_ctx)
        mod3 = _modulation(cc, w_mod[i], b_mod[i]).reshape(mod_rows, 6, d)

        qscale = jnp.ones((w_in.shape[2],), F32).at[col_q:col_k].set(NA_HEAD_DIM ** -0.5)
        w_in_s = (w_in[i] * qscale).astype(BF16)
        b_in_s = b_in[i] * qscale
        zx = _in_proj(x, mod3, lambda b: b, norm1_g[i], w_in_s, b_in_s, tm=512)
        kvc = _in_proj(ctx, mod3, lambda b: bsz, norm1_g[i], w_in_s[:, col_k:col_g], b_in_s[col_k:col_g],
                       tm=ctx.shape[1])

        spec, nyq = _filter_spectra(seq, hy_width, filt_w1[i], filt_b1[i], filt_w2[i], filt_b2[i], filt_w3[i],
                                    filt_freq[i], cs, cb=256)
        hy = _hyena(zx, hy_short_w[i], hy_short_b[i], hy_skip[i], spec, nyq, cs, hy_width, cb=256, mc=512)
        na = _na(zx, kvc, _na_bias_tables(na_rpb[i], rows), col_q)

        x1, h2, logits_t = _merge(hy, na, zx, x, mod3, norm2_g[i], w_branch_hy[i].astype(BF16),
                                  w_branch_na[i].astype(BF16), w_out[i].astype(BF16), w_router[i].T, col_g, tm=512)
        rank, rank_t, gate_t = _route(logits_t, cap)
        y = _experts(rank, h2, w_gate[i].astype(BF16), w_up[i].astype(BF16), w_down[i].astype(BF16), cap)
        x = _combine(rank_t, gate_t, y, x1, mod3, final_g, cap, tm=512)
    return x
```

```python
import functools
import math

import jax
import jax.numpy as jnp
import numpy as np
from jax import lax
from jax.experimental import pallas as pl
from jax.experimental.pallas import tpu as pltpu

F32 = jnp.float32
BF16 = jnp.bfloat16

EPS = 1e-6
GRID_W = 64
HY_ORDER = 2
SHORT_CONV = 3
FILT_BANDS = 8
DECAY_TARGET = 1e-2
FAST_DECAY_PCT = 0.3
SLOW_DECAY_PCT = 1.5
HY_BLOCK = 512
NA_HEADS = 8
NA_HEAD_DIM = 64
NA_WIN_H = 8
NA_WIN_W = 16
NA_ROWS_PER_STEP = 4
NA_SLAB_ROWS = NA_ROWS_PER_STEP + NA_WIN_H
N_EXPERTS = 16
EC_CAPACITY = 2

MASK_VALUE = -1e30
LANES = 128
BF16_SUBLANES = 16
VMEM_LIMIT = 56 * 1024 * 1024


def _cparams(*sem):
    return pltpu.CompilerParams(dimension_semantics=sem, vmem_limit_bytes=VMEM_LIMIT)


def _const_spec(shape):
    nd = len(shape)
    return pl.BlockSpec(shape, lambda *_: (0,) * nd, pipeline_mode=pl.Buffered(1))


def _dot(a, b):
    return jnp.dot(a, b, preferred_element_type=F32)


def _dot_nt(a, b):
    return lax.dot_general(a, b, (((1,), (1,)), ((), ())), preferred_element_type=F32)


def _mod_kernel(c_ref, w_ref, b_ref, o_ref):
    c = c_ref[...]
    s = c * jax.nn.sigmoid(c)
    o_ref[...] = _dot(s, w_ref[...]) + b_ref[...]


def _modulation(cc, w_mod, b_mod):
    rows, d = cc.shape
    n = w_mod.shape[1]
    tn = 1536
    return pl.pallas_call(
        _mod_kernel,
        grid=(n // tn,),
        in_specs=[
            pl.BlockSpec((rows, d), lambda j: (0, 0)),
            pl.BlockSpec((d, tn), lambda j: (0, j)),
            pl.BlockSpec((1, tn), lambda j: (0, j)),
        ],
        out_specs=pl.BlockSpec((rows, tn), lambda j: (0, j)),
        out_shape=jax.ShapeDtypeStruct((rows, n), F32),
        compiler_params=_cparams("arbitrary"),
        name="modulation",
    )(cc, w_mod, b_mod.reshape(1, n))


def _rms_mod(x, g, shift, scale):
    ms = jnp.mean(x * x, axis=-1, keepdims=True)
    return (x * lax.rsqrt(ms + EPS) * g) * (1.0 + scale) + shift


def _in_proj_kernel(x_ref, mod_ref, g_ref, w_ref, b_ref, o_ref, *, n_chunk):
    h = _rms_mod(x_ref[0], g_ref[...], mod_ref[0, 0:1, :], mod_ref[0, 1:2, :]).astype(BF16)
    n = w_ref.shape[1]
    for j in range(0, n, n_chunk):
        z = _dot(h, w_ref[:, j:j + n_chunk]) + b_ref[:, j:j + n_chunk]
        o_ref[0, :, j:j + n_chunk] = z.astype(o_ref.dtype)


def _in_proj(x, mod3, mod_row, g, w, b, tm):
    bsz, t, d = x.shape
    n = w.shape[1]
    return pl.pallas_call(
        functools.partial(_in_proj_kernel, n_chunk=min(n, 1024)),
        grid=(bsz, t // tm),
        in_specs=[
            pl.BlockSpec((1, tm, d), lambda i, j: (i, j, 0)),
            pl.BlockSpec((1, 6, d), lambda i, j: (mod_row(i), 0, 0)),
            _const_spec((1, d)),
            _const_spec((d, n)),
            _const_spec((1, n)),
        ],
        out_specs=pl.BlockSpec((1, tm, n), lambda i, j: (i, j, 0)),
        out_shape=jax.ShapeDtypeStruct((bsz, t, n), BF16),
        compiler_params=_cparams("parallel", "parallel"),
        name="in_proj",
    )(x, mod3, g.reshape(1, d), w, b.reshape(1, n))


def _dft_matrix(blk):
    n = 2 * blk
    i = jnp.arange(blk, dtype=jnp.int32)
    ft = (i[:, None] * i[None, :]) % n
    ang = ft.astype(F32) * (2.0 * math.pi / n)
    return jnp.concatenate([jnp.cos(ang), -jnp.sin(ang)], axis=1).astype(BF16)


def _alt_sign(shape):
    row = lax.broadcasted_iota(jnp.int32, shape, 0)
    return (1 - 2 * (row & 1)).astype(F32)


def _filter_kernel(feats_ref, w1_ref, b1_ref, w2_ref, b2_ref, freq_ref, w3p_ref, w3f_ref, decay_ref,
                   cs_ref, spec_ref, nyq_ref, k2_ref, ar_ref, ai_ref, *, blk):
    seq2, cb = k2_ref.shape
    seq = seq2 // 2
    nblk = seq2 // blk
    n = 2 * blk
    freq = freq_ref[...]
    h = jnp.sin(freq * (_dot(feats_ref[...], w1_ref[...]) + b1_ref[...]))
    h = jnp.sin(freq * (_dot(h, w2_ref[...]) + b2_ref[...]))
    row = lax.broadcasted_iota(jnp.int32, (seq2, cb), 0)
    k2 = jnp.where(row < seq, _dot(h, w3f_ref[...]), _dot(h, w3p_ref[...])) * decay_ref[...]
    k2_ref[...] = jnp.where(row == 0, 0.0, k2)
    sign = _alt_sign((blk, cb))
    for d in range(nblk):
        a = k2_ref[d * blk:(d + 1) * blk, :]
        ab = a.astype(BF16)
        ar_ref[d] = _dot(cs_ref[:, 0:blk], ab)
        ai_ref[d] = _dot(cs_ref[:, blk:n], ab)
    frow = lax.broadcasted_iota(jnp.int32, (blk, cb), 0)
    wgt = jnp.where(frow == 0, 1.0 / n, 2.0 / n)
    for d in range(1, nblk):
        a0 = k2_ref[(d - 1) * blk:(d - 1) * blk + 1, :]
        spec_ref[0, d - 1, 0] = (ar_ref[d] + sign * (ar_ref[d - 1] - a0)) * wgt
        spec_ref[0, d - 1, 1] = (ai_ref[d] + sign * ai_ref[d - 1]) * wgt
        cur = jnp.sum(sign * k2_ref[d * blk:(d + 1) * blk, :], axis=0, keepdims=True)
        prev = jnp.sum(sign * k2_ref[(d - 1) * blk:d * blk, :], axis=0, keepdims=True)
        nyq_ref[0, d - 1] = (cur + prev - a0) * (1.0 / n)


def _filter_spectra(seq, width, w1, b1, w2, b2, w3, freq, cs, cb, blk):
    pos = jnp.abs(jnp.arange(2 * seq, dtype=F32) - seq)
    t = pos / max(seq - 1, 1)
    omega = 2.0 * math.pi * pos / seq
    bands = jnp.linspace(1e-4, FILT_BANDS - 1, FILT_BANDS, dtype=F32)
    ang = omega[:, None] * bands[None, :]
    feats = jnp.concatenate([t[:, None], jnp.cos(ang), -jnp.sin(ang)], axis=-1)
    emb, hid = w1.shape
    emb_pad = -(-emb // 8) * 8
    feats = jnp.pad(feats, ((0, 0), (0, emb_pad - emb)))
    w1 = jnp.pad(w1, ((0, emb_pad - emb), (0, 0)))
    max_decay = math.log(DECAY_TARGET) / FAST_DECAY_PCT
    min_decay = math.log(DECAY_TARGET) / SLOW_DECAY_PCT
    deltas = jnp.linspace(min_decay, max_decay, width, dtype=F32)
    decay = jnp.exp(-t[:, None] * jnp.abs(deltas)[None, :])
    ncb = width // cb
    nblk = 2 * seq // blk
    return pl.pallas_call(
        functools.partial(_filter_kernel, blk=blk),
        grid=(HY_ORDER, ncb),
        in_specs=[
            _const_spec((2 * seq, emb_pad)),
            _const_spec((emb_pad, hid)),
            _const_spec((1, hid)),
            _const_spec((hid, hid)),
            _const_spec((1, hid)),
            _const_spec((1, hid)),
            pl.BlockSpec((hid, cb), lambda o, c: (0, o * ncb + c)),
            pl.BlockSpec((hid, cb), lambda o, c: (0, (HY_ORDER + o) * ncb + c)),
            pl.BlockSpec((2 * seq, cb), lambda o, c: (0, c)),
            _const_spec((blk, 2 * blk)),
        ],
        out_specs=[
            pl.BlockSpec((1, nblk - 1, 2, blk, cb), lambda o, c: (o, 0, 0, 0, c)),
            pl.BlockSpec((1, nblk - 1, 1, cb), lambda o, c: (o, 0, 0, c)),
        ],
        out_shape=[
            jax.ShapeDtypeStruct((HY_ORDER, nblk - 1, 2, blk, width), F32),
            jax.ShapeDtypeStruct((HY_ORDER, nblk - 1, 1, width), F32),
        ],
        scratch_shapes=[
            pltpu.VMEM((2 * seq, cb), F32),
            pltpu.VMEM((nblk, blk, cb), F32),
            pltpu.VMEM((nblk, blk, cb), F32),
        ],
        compiler_params=_cparams("arbitrary", "arbitrary"),
        name="hyena_filters",
    )(feats, w1, b1.reshape(1, hid), w2, b2.reshape(1, hid), freq.reshape(1, hid), w3, w3, decay, cs)


def _hyena_kernel(zv_ref, z1_ref, z2_ref, sw_ref, sb_ref, skip_ref, spec_ref, nyq_ref, cs_ref, o_ref,
                  u_ref, g_ref, ub_ref, re_ref, im_ref, yri_ref, *, blk):
    seq, cb = zv_ref.shape[1], zv_ref.shape[2]
    nb = seq // blk
    row = lax.broadcasted_iota(jnp.int32, (seq, cb), 0)
    first, last = row == 0, row == seq - 1
    sign = _alt_sign((blk, cb))

    def short_conv(z_ref, g):
        z = z_ref[0].astype(F32)
        prev = jnp.where(first, 0.0, pltpu.roll(z, 1, 0))
        nxt = jnp.where(last, 0.0, pltpu.roll(z, seq - 1, 0))
        return sb_ref[g] + prev * sw_ref[g, 0:1, :] + z * sw_ref[g, 1:2, :] + nxt * sw_ref[g, 2:3, :]

    def long_conv(o, gate_ref, g):
        ub_ref[...] = u_ref[...].astype(BF16)
        nyq_in = []
        for j in range(nb):
            rows = slice(j * blk, (j + 1) * blk)
            re_ref[j] = _dot(cs_ref[:, 0:blk], ub_ref[rows, :])
            im_ref[j] = _dot(cs_ref[:, blk:2 * blk], ub_ref[rows, :])
            nyq_in.append(jnp.sum(sign * u_ref[rows, :], axis=0, keepdims=True))
        g_ref[...] = short_conv(gate_ref, g)
        for i in range(nb):
            yr = yi = nyq = None
            for j in range(nb):
                d = i - j + nb - 1
                gr, gi = spec_ref[o, d, 0], spec_ref[o, d, 1]
                re, im = re_ref[j], im_ref[j]
                tr = re * gr - im * gi
                ti = re * gi + im * gr
                tn = nyq_in[j] * nyq_ref[o, d]
                yr, yi, nyq = (tr, ti, tn) if yr is None else (yr + tr, yi + ti, nyq + tn)
            yri_ref[0:blk, :] = yr.astype(BF16)
            yri_ref[blk:2 * blk, :] = yi.astype(BF16)
            rows = slice(i * blk, (i + 1) * blk)
            y = _dot(cs_ref[...], yri_ref[...]) + sign * nyq + u_ref[rows, :] * skip_ref[o]
            g_ref[rows, :] = g_ref[rows, :] * y
        u_ref[...] = g_ref[...]

    u_ref[...] = short_conv(zv_ref, 0)
    long_conv(0, z1_ref, 1)
    long_conv(1, z2_ref, 2)
    o_ref[0] = u_ref[...].astype(o_ref.dtype)


def _hyena(zx, short_w, short_b, skip, spec, nyq, cs, width, cb, blk):
    bsz, seq, _ = zx.shape
    ncb = width // cb
    nb = seq // blk
    sw = short_w.reshape(SHORT_CONV, 3, width).transpose(1, 0, 2)
    sb = short_b.reshape(3, 1, width)
    skip3 = skip.reshape(HY_ORDER, 1, width)

    def zspec(g):
        return pl.BlockSpec((1, seq, cb), lambda c, i: (i, 0, g * ncb + c))

    return pl.pallas_call(
        functools.partial(_hyena_kernel, blk=blk),
        grid=(ncb, bsz),
        in_specs=[
            zspec(0), zspec(1), zspec(2),
            pl.BlockSpec((3, SHORT_CONV, cb), lambda c, i: (0, 0, c)),
            pl.BlockSpec((3, 1, cb), lambda c, i: (0, 0, c)),
            pl.BlockSpec((HY_ORDER, 1, cb), lambda c, i: (0, 0, c)),
            pl.BlockSpec((HY_ORDER, 2 * nb - 1, 2, blk, cb), lambda c, i: (0, 0, 0, 0, c),
                         pipeline_mode=pl.Buffered(1)),
            pl.BlockSpec((HY_ORDER, 2 * nb - 1, 1, cb), lambda c, i: (0, 0, 0, c)),
            _const_spec((blk, 2 * blk)),
        ],
        out_specs=pl.BlockSpec((1, seq, cb), lambda c, i: (i, 0, c)),
        out_shape=jax.ShapeDtypeStruct((bsz, seq, width), BF16),
        scratch_shapes=[
            pltpu.VMEM((seq, cb), F32),
            pltpu.VMEM((seq, cb), F32),
            pltpu.VMEM((seq, cb), BF16),
            pltpu.VMEM((nb, blk, cb), F32),
            pltpu.VMEM((nb, blk, cb), F32),
            pltpu.VMEM((2 * blk, cb), BF16),
        ],
        compiler_params=_cparams("parallel", "parallel"),
        name="hyena",
    )(zx, zx, zx, sw, sb, skip3, spec, nyq, cs)


def _na_col_bias_kernel(rpb_ref, sel_ref, mask_ref, o_ref):
    acc = jnp.zeros(o_ref.shape, F32) + mask_ref[...]
    for ci in range(sel_ref.shape[0]):
        acc = acc + rpb_ref[:, ci:ci + 1] * sel_ref[ci:ci + 1, :]
    o_ref[...] = acc


def _na_bias_tables(rpb, rows):
    rq, sl, half = NA_ROWS_PER_STEP, NA_SLAB_ROWS, NA_WIN_H // 2
    n_ri, n_ci = 2 * NA_WIN_H - 1, 2 * NA_WIN_W - 1
    qc = np.arange(GRID_W)
    kc = np.arange(GRID_W)
    ws = np.clip(qc - NA_WIN_W // 2, 0, GRID_W - NA_WIN_W)
    in_win = (kc[None, :] >= ws[:, None]) & (kc[None, :] < ws[:, None] + NA_WIN_W)
    ci = np.clip(kc[None, :] - qc[:, None] + NA_WIN_W - 1, 0, n_ci - 1)
    sel = ((ci[None] == np.arange(n_ci)[:, None, None]) & in_win[None]).astype(np.float32)
    sel = sel.reshape(n_ci, GRID_W * GRID_W)
    mask = np.where(in_win, 0.0, MASK_VALUE).astype(np.float32).reshape(1, GRID_W * GRID_W)
    col = pl.pallas_call(
        _na_col_bias_kernel,
        out_shape=jax.ShapeDtypeStruct((NA_HEADS * n_ri, GRID_W * GRID_W), F32),
        name="na_col_bias",
    )(rpb.reshape(NA_HEADS * n_ri, n_ci), jnp.asarray(sel), jnp.asarray(mask))
    col = col.reshape(NA_HEADS, n_ri, GRID_W, GRID_W)

    n_steps = rows // rq
    tables = []
    for step in (0, 1, n_steps - 1):
        start = int(np.clip(step * rq - half, 0, rows - sl))
        per_row = []
        for i in range(rq):
            r = step * rq + i
            rs = int(np.clip(r - half, 0, rows - NA_WIN_H))
            ri0 = rs - r + NA_WIN_H - 1
            blk = col[:, ri0:ri0 + NA_WIN_H].transpose(0, 2, 1, 3)
            lo = rs - start
            blk = jnp.pad(blk, ((0, 0), (0, 0), (lo, sl - NA_WIN_H - lo), (0, 0)), constant_values=MASK_VALUE)
            per_row.append(blk.reshape(NA_HEADS, GRID_W, sl * GRID_W))
        tables.append(jnp.concatenate(per_row, axis=1))
    return jnp.stack(tables)


def _na_kernel(q_ref, k_ref, v_ref, kvc_ref, bias_ref, *rest, rows):
    n_cast = (len(rest) - 1) // 2
    o_ref = rest[n_cast]
    for src, dst in zip(rest[:n_cast], rest[n_cast + 1:]):
        dst[...] = src[...].astype(dst.dtype)
    rq, sl = NA_ROWS_PER_STEP, NA_SLAB_ROWS
    width = NA_HEADS * NA_HEAD_DIM
    step = pl.program_id(1)
    start = jnp.clip(step * rq - NA_WIN_H // 2, 0, rows - sl)
    start = pl.multiple_of(start * GRID_W, GRID_W)
    lane = lax.broadcasted_iota(jnp.int32, (rq * GRID_W, LANES), 1)
    for hp in range(width // LANES):
        cols = slice(hp * LANES, (hp + 1) * LANES)
        q2 = q_ref[0, :, cols]
        k2 = k_ref[0, pl.ds(start, sl * GRID_W), cols]
        v2 = v_ref[0, pl.ds(start, sl * GRID_W), cols]
        kc2 = kvc_ref[0, :, cols]
        vc2 = kvc_ref[0, :, width + hp * LANES:width + (hp + 1) * LANES]
        outs = []
        for s in range(LANES // NA_HEAD_DIM):
            own = (lane >= s * NA_HEAD_DIM) & (lane < (s + 1) * NA_HEAD_DIM)
            qm = jnp.where(own, q2, jnp.zeros_like(q2))
            s_loc = _dot_nt(qm, k2) + bias_ref[0, hp * (LANES // NA_HEAD_DIM) + s]
            s_ctx = _dot_nt(qm, kc2)
            m = jnp.maximum(jnp.max(s_loc, axis=-1, keepdims=True), jnp.max(s_ctx, axis=-1, keepdims=True))
            p_loc = jnp.exp(s_loc - m)
            p_ctx = jnp.exp(s_ctx - m)
            den = jnp.sum(p_loc, axis=-1, keepdims=True) + jnp.sum(p_ctx, axis=-1, keepdims=True)
            o = _dot(p_loc.astype(BF16), v2) + _dot(p_ctx.astype(BF16), vc2)
            outs.append(jnp.where(own, o / den, 0.0))
        o_ref[0, :, cols] = sum(outs).astype(o_ref.dtype)


def _na(zx, kvc, bias, col_q, cast_2d):
    bsz, seq, _ = zx.shape
    rows = seq // GRID_W
    rq = NA_ROWS_PER_STEP
    assert rows % rq == 0 and rows >= NA_SLAB_ROWS and rows // rq >= 3
    n_steps = rows // rq
    width = NA_HEADS * NA_HEAD_DIM
    qb = col_q // width
    total = bsz * n_steps
    for w in cast_2d:
        assert w.shape[0] % (total * BF16_SUBLANES) == 0, (w.shape, total)

    def cfg(s):
        return jnp.minimum(s, 1) + jnp.maximum(s - (n_steps - 2), 0)

    def cast_spec(w):
        return pl.BlockSpec((w.shape[0] // total, w.shape[1]), lambda i, s: (i * n_steps + s, 0))

    outs = pl.pallas_call(
        functools.partial(_na_kernel, rows=rows),
        grid=(bsz, n_steps),
        in_specs=[
            pl.BlockSpec((1, rq * GRID_W, width), lambda i, s: (i, s, qb)),
            pl.BlockSpec((1, seq, width), lambda i, s: (i, 0, qb + 1)),
            pl.BlockSpec((1, seq, width), lambda i, s: (i, 0, qb + 2)),
            pl.BlockSpec((1,) + kvc.shape[1:], lambda i, s: (i, 0, 0)),
            pl.BlockSpec((1,) + bias.shape[1:], lambda i, s: (cfg(s), 0, 0, 0)),
        ] + [cast_spec(w) for w in cast_2d],
        out_specs=[pl.BlockSpec((1, rq * GRID_W, width), lambda i, s: (i, s, 0))] + [cast_spec(w) for w in cast_2d],
        out_shape=[jax.ShapeDtypeStruct((bsz, seq, width), BF16)]
        + [jax.ShapeDtypeStruct(w.shape, BF16) for w in cast_2d],
        compiler_params=_cparams("parallel", "arbitrary"),
        name="na",
    )(zx, zx, zx, kvc, bias, *cast_2d)
    return outs[0], outs[1:]


def _cast_in_na(weights, total_steps):
    max_block_bytes = 1 << 20
    for w in weights:
        rows = w.shape[0] * w.shape[1]
        if rows % (total_steps * BF16_SUBLANES) or rows // total_steps * w.shape[2] * 4 > max_block_bytes:
            return False
    return True


def _merge_kernel(hy_ref, na_ref, ghy_ref, gna_ref, x_ref, mod_ref, g2_ref, wbh_ref, wbn_ref, wo_ref, wr_ref,
                  x1_ref, h2_ref, lg_ref):
    a = _dot(hy_ref[0], wbh_ref[...])
    b = _dot(na_ref[0], wbn_ref[...])
    m = jax.nn.sigmoid(ghy_ref[0].astype(F32)) * a + jax.nn.sigmoid(gna_ref[0].astype(F32)) * b
    mix = _dot(m.astype(BF16), wo_ref[...])
    x1 = x_ref[0] + mod_ref[0, 2:3, :] * mix
    x1_ref[0] = x1
    h2 = _rms_mod(x1, g2_ref[...], mod_ref[0, 3:4, :], mod_ref[0, 4:5, :])
    h2_ref[0] = h2.astype(h2_ref.dtype)
    lg_ref[0] = _dot_nt(wr_ref[...], h2)


def _merge(hy, na, zx, x, mod3, g2, wbh, wbn, wo, wr_t, col_g, tm):
    bsz, seq, d = x.shape
    gb = col_g // d
    ne = wr_t.shape[0]
    return pl.pallas_call(
        _merge_kernel,
        grid=(bsz, seq // tm),
        in_specs=[
            pl.BlockSpec((1, tm, hy.shape[2]), lambda i, j: (i, j, 0)),
            pl.BlockSpec((1, tm, na.shape[2]), lambda i, j: (i, j, 0)),
            pl.BlockSpec((1, tm, d), lambda i, j: (i, j, gb)),
            pl.BlockSpec((1, tm, d), lambda i, j: (i, j, gb + 1)),
            pl.BlockSpec((1, tm, d), lambda i, j: (i, j, 0)),
            pl.BlockSpec((1, 6, d), lambda i, j: (i, 0, 0)),
            _const_spec((1, d)),
            _const_spec(wbh.shape),
            _const_spec(wbn.shape),
            _const_spec(wo.shape),
            _const_spec(wr_t.shape),
        ],
        out_specs=[
            pl.BlockSpec((1, tm, d), lambda i, j: (i, j, 0)),
            pl.BlockSpec((1, tm, d), lambda i, j: (i, j, 0)),
            pl.BlockSpec((1, ne, tm), lambda i, j: (i, 0, j)),
        ],
        out_shape=[
            jax.ShapeDtypeStruct((bsz, seq, d), F32),
            jax.ShapeDtypeStruct((bsz, seq, d), BF16),
            jax.ShapeDtypeStruct((bsz, ne, seq), F32),
        ],
        compiler_params=_cparams("parallel", "parallel"),
        name="merge",
    )(hy, na, zx, zx, x, mod3, g2.reshape(1, d), wbh, wbn, wo, wr_t)


def _route_kernel(lg_ref, tri_ref, rank_ref, rank_t_ref, gate_t_ref, *, cap):
    grp, ne, t = lg_ref.shape
    lg = lg_ref[...]
    e = jnp.exp(lg - jnp.max(lg, axis=1, keepdims=True))
    aff = (e / jnp.sum(e, axis=1, keepdims=True)).reshape(grp * ne, t)

    def bit_step(i, bits):
        cand = bits | (jnp.int32(1) << (30 - i))
        keep = jnp.sum((aff >= pltpu.bitcast(cand, F32)).astype(jnp.int32), axis=1, keepdims=True) >= cap
        return jnp.where(keep, cand, bits)

    thr = pltpu.bitcast(lax.fori_loop(0, 31, bit_step, jnp.zeros((grp * ne, 1), jnp.int32)), F32)
    above = aff > thr
    tie = aff == thr
    need = cap - jnp.sum(above.astype(jnp.int32), axis=1, keepdims=True)
    tri = tri_ref[...]
    tie_before = _dot(tie.astype(BF16), tri)
    sel = above | (tie & (tie_before < need.astype(F32)))
    sel_before = _dot(sel.astype(BF16), tri)
    rank = jnp.where(sel, sel_before, -1.0)
    gate = jnp.where(sel, aff, 0.0)
    rank_ref[...] = rank.astype(jnp.int32).reshape(grp, ne, t)
    pad_r = jnp.full((LANES - ne, t), -1.0, F32)
    pad_g = jnp.zeros((LANES - ne, t), F32)
    for g in range(grp):
        rows = slice(g * ne, (g + 1) * ne)
        rank_t_ref[g] = jnp.concatenate([rank[rows], pad_r], axis=0).T.astype(jnp.int32)
        gate_t_ref[g] = jnp.concatenate([gate[rows], pad_g], axis=0).T


def _route(logits_t, cap, grp):
    bsz, ne, t = logits_t.shape
    i = jnp.arange(t, dtype=jnp.int32)
    tri = (i[:, None] < i[None, :]).astype(BF16)
    return pl.pallas_call(
        functools.partial(_route_kernel, cap=cap),
        grid=(bsz // grp,),
        in_specs=[
            pl.BlockSpec((grp, ne, t), lambda b: (b, 0, 0)),
            _const_spec((t, t)),
        ],
        out_specs=[
            pl.BlockSpec((grp, ne, t), lambda b: (b, 0, 0)),
            pl.BlockSpec((grp, t, LANES), lambda b: (b, 0, 0)),
            pl.BlockSpec((grp, t, LANES), lambda b: (b, 0, 0)),
        ],
        out_shape=[
            jax.ShapeDtypeStruct((bsz, ne, t), jnp.int32),
            jax.ShapeDtypeStruct((bsz, t, LANES), jnp.int32),
            jax.ShapeDtypeStruct((bsz, t, LANES), F32),
        ],
        compiler_params=_cparams("parallel"),
        name="route",
    )(logits_t, tri)


def _expert_kernel(rank_ref, h_ref, wg_ref, wu_ref, wd_ref, y_ref, *, cap, f_chunk):
    e = pl.program_id(0)
    t = h_ref.shape[1]
    f = wg_ref.shape[2]
    rank = rank_ref[0, pl.ds(e, 1), :]
    slot = lax.broadcasted_iota(jnp.int32, (cap, t), 0)
    onehot = jnp.where(rank == slot, 1.0, 0.0).astype(BF16)
    xe = _dot(onehot, h_ref[0]).astype(BF16)
    y = None
    for j in range(0, f, f_chunk):
        g = _dot(xe, wg_ref[0, :, j:j + f_chunk])
        u = _dot(xe, wu_ref[0, :, j:j + f_chunk])
        act = (g * jax.nn.sigmoid(g) * u).astype(BF16)
        part = _dot(act, wd_ref[0, j:j + f_chunk, :])
        y = part if y is None else y + part
    y_ref[0, 0] = y.astype(y_ref.dtype)


def _experts(rank, h2, wg, wu, wd, cap):
    bsz, t, d = h2.shape
    ne, _, f = wg.shape
    f_chunk = f // 2 if f % (2 * LANES) == 0 else f
    return pl.pallas_call(
        functools.partial(_expert_kernel, cap=cap, f_chunk=f_chunk),
        grid=(ne, bsz),
        in_specs=[
            pl.BlockSpec((1, ne, t), lambda e, b: (b, 0, 0)),
            pl.BlockSpec((1, t, d), lambda e, b: (b, 0, 0)),
            pl.BlockSpec((1, d, f), lambda e, b: (e, 0, 0)),
            pl.BlockSpec((1, d, f), lambda e, b: (e, 0, 0)),
            pl.BlockSpec((1, f, d), lambda e, b: (e, 0, 0)),
        ],
        out_specs=pl.BlockSpec((1, 1, cap, d), lambda e, b: (b, e, 0, 0)),
        out_shape=jax.ShapeDtypeStruct((bsz, ne, cap, d), BF16),
        compiler_params=_cparams("arbitrary", "arbitrary"),
        name="experts",
    )(rank, h2, wg, wu, wd)


def _combine_kernel(rank_t_ref, gate_t_ref, y_ref, x1_ref, mod_ref, gf_ref, o_ref, *, cap):
    ne = y_ref.shape[1]
    tm = x1_ref.shape[1]
    slot = lax.broadcasted_iota(jnp.int32, (tm, cap), 1)
    acc = jnp.zeros(x1_ref.shape[1:], F32)
    for e in range(ne):
        r = rank_t_ref[0, :, e:e + 1]
        g = gate_t_ref[0, :, e:e + 1]
        scat = jnp.where(r == slot, g, 0.0).astype(BF16)
        acc = acc + _dot(scat, y_ref[0, e])
    x2 = x1_ref[0] + mod_ref[0, 5:6, :] * acc
    ms = jnp.mean(x2 * x2, axis=-1, keepdims=True)
    o_ref[0] = x2 * lax.rsqrt(ms + EPS) * gf_ref[...]


def _combine(rank_t, gate_t, y, x1, mod3, final_g, cap, tm):
    bsz, t, d = x1.shape
    ne = y.shape[1]
    return pl.pallas_call(
        functools.partial(_combine_kernel, cap=cap),
        grid=(bsz, t // tm),
        in_specs=[
            pl.BlockSpec((1, tm, LANES), lambda i, j: (i, j, 0)),
            pl.BlockSpec((1, tm, LANES), lambda i, j: (i, j, 0)),
            pl.BlockSpec((1, ne, cap, d), lambda i, j: (i, 0, 0, 0)),
            pl.BlockSpec((1, tm, d), lambda i, j: (i, j, 0)),
            pl.BlockSpec((1, 6, d), lambda i, j: (i, 0, 0)),
            _const_spec((1, d)),
        ],
        out_specs=pl.BlockSpec((1, tm, d), lambda i, j: (i, j, 0)),
        out_shape=jax.ShapeDtypeStruct((bsz, t, d), F32),
        compiler_params=_cparams("parallel", "arbitrary"),
        name="combine",
    )(rank_t, gate_t, y, x1, mod3, final_g.reshape(1, d))


def kernel(x, c, ctx, c_ctx, w_mod, b_mod, norm1_g, norm2_g, w_in, b_in, hy_short_w, hy_short_b, hy_skip, filt_w1, filt_b1, filt_w2, filt_b2, filt_w3, filt_freq, na_rpb, w_branch_hy, w_branch_na, w_out, w_router, w_gate, w_up, w_down, final_g):
    depth = w_mod.shape[0]
    bsz, seq, d = x.shape
    hy_width = w_branch_hy.shape[1]
    na_width = w_branch_na.shape[1]
    col_q = 3 * hy_width
    col_k = col_q + na_width
    col_g = col_q + 3 * na_width
    cap = EC_CAPACITY * seq // N_EXPERTS
    rows = seq // GRID_W
    mod_rows = -(-(bsz + 1) // 8) * 8

    hy_blk = HY_BLOCK
    cs = _dft_matrix(hy_blk)
    for i in range(depth):
        assert i == depth - 1, "only the final layer's data flow (context feeds keys/values only) is implemented"
        cc = jnp.zeros((mod_rows, d), F32).at[:bsz].set(c).at[bsz].set(c_ctx)
        mod3 = _modulation(cc, w_mod[i], b_mod[i]).reshape(mod_rows, 6, d)

        qscale = jnp.ones((w_in.shape[2],), F32).at[col_q:col_k].set(NA_HEAD_DIM ** -0.5)
        w_in_s = (w_in[i] * qscale).astype(BF16)
        b_in_s = b_in[i] * qscale
        zx = _in_proj(x, mod3, lambda b: b, norm1_g[i], w_in_s, b_in_s, tm=512)
        kvc = _in_proj(ctx, mod3, lambda b: bsz, norm1_g[i], w_in_s[:, col_k:col_g], b_in_s[col_k:col_g],
                       tm=ctx.shape[1])

        spec, nyq = _filter_spectra(seq, hy_width, filt_w1[i], filt_b1[i], filt_w2[i], filt_b2[i], filt_w3[i],
                                    filt_freq[i], cs, cb=256, blk=hy_blk)
        hy = _hyena(zx, hy_short_w[i], hy_short_b[i], hy_skip[i], spec, nyq, cs, hy_width, cb=256, blk=hy_blk)

        experts_w = (w_gate[i], w_up[i], w_down[i])
        if _cast_in_na(experts_w, bsz * (rows // NA_ROWS_PER_STEP)):
            na, experts_w = _na(zx, kvc, _na_bias_tables(na_rpb[i], rows), col_q,
                                [w.reshape(-1, w.shape[2]) for w in experts_w])
            experts_w = [w2.reshape(w.shape) for w2, w in zip(experts_w, (w_gate[i], w_up[i], w_down[i]))]
        else:
            na, _ = _na(zx, kvc, _na_bias_tables(na_rpb[i], rows), col_q, [])
            experts_w = [w.astype(BF16) for w in experts_w]

        x1, h2, logits_t = _merge(hy, na, zx, x, mod3, norm2_g[i], w_branch_hy[i].astype(BF16),
                                  w_branch_na[i].astype(BF16), w_out[i].astype(BF16), w_router[i].T, col_g, tm=512)
        rank, rank_t, gate_t = _route(logits_t, cap, grp=math.gcd(bsz, 8))
        y = _experts(rank, h2, *experts_w, cap)
        x = _combine(rank_t, gate_t, y, x1, mod3, final_g, cap, tm=512)
    return x
```

```python
import functools
import math

import jax
import jax.numpy as jnp
import numpy as np
from jax import lax
from jax.experimental import pallas as pl
from jax.experimental.pallas import tpu as pltpu

F32 = jnp.float32
BF16 = jnp.bfloat16

EPS = 1e-6
GRID_W = 64
HY_ORDER = 2
SHORT_CONV = 3
FILT_BANDS = 8
DECAY_TARGET = 1e-2
FAST_DECAY_PCT = 0.3
SLOW_DECAY_PCT = 1.5
HY_BLOCK = 512
NA_HEADS = 8
NA_HEAD_DIM = 64
NA_WIN_H = 8
NA_WIN_W = 16
NA_ROWS_PER_STEP = 4
NA_SLAB_ROWS = NA_ROWS_PER_STEP + NA_WIN_H
N_EXPERTS = 16
EC_CAPACITY = 2

MASK_VALUE = -1e30
LANES = 128
BF16_SUBLANES = 16
VMEM_LIMIT = 56 * 1024 * 1024


def _cparams(*sem):
    return pltpu.CompilerParams(dimension_semantics=sem, vmem_limit_bytes=VMEM_LIMIT)


def _const_spec(shape):
    nd = len(shape)
    return pl.BlockSpec(shape, lambda *_: (0,) * nd, pipeline_mode=pl.Buffered(1))


def _dot(a, b):
    return jnp.dot(a, b, preferred_element_type=F32)


def _dot_nt(a, b):
    return lax.dot_general(a, b, (((1,), (1,)), ((), ())), preferred_element_type=F32)


def _mod_kernel(c_ref, w_ref, b_ref, o_ref):
    c = c_ref[...]
    s = c * jax.nn.sigmoid(c)
    o_ref[...] = _dot(s, w_ref[...]) + b_ref[...]


def _modulation(cc, w_mod, b_mod):
    rows, d = cc.shape
    n = w_mod.shape[1]
    tn = 1536
    return pl.pallas_call(
        _mod_kernel,
        grid=(n // tn,),
        in_specs=[
            pl.BlockSpec((rows, d), lambda j: (0, 0)),
            pl.BlockSpec((d, tn), lambda j: (0, j)),
            pl.BlockSpec((1, tn), lambda j: (0, j)),
        ],
        out_specs=pl.BlockSpec((rows, tn), lambda j: (0, j)),
        out_shape=jax.ShapeDtypeStruct((rows, n), F32),
        compiler_params=_cparams("arbitrary"),
        name="modulation",
    )(cc, w_mod, b_mod.reshape(1, n))


def _rms_mod(x, g, shift, scale):
    ms = jnp.mean(x * x, axis=-1, keepdims=True)
    return (x * lax.rsqrt(ms + EPS) * g) * (1.0 + scale) + shift


def _in_proj_kernel(x_ref, mod_ref, g_ref, w_ref, b_ref, o_ref, *, n_chunk):
    h = _rms_mod(x_ref[0], g_ref[...], mod_ref[0, 0:1, :], mod_ref[0, 1:2, :]).astype(BF16)
    n = w_ref.shape[1]
    for j in range(0, n, n_chunk):
        z = _dot(h, w_ref[:, j:j + n_chunk]) + b_ref[:, j:j + n_chunk]
        o_ref[0, :, j:j + n_chunk] = z.astype(o_ref.dtype)


def _in_proj(x, mod3, mod_row, g, w, b, tm):
    bsz, t, d = x.shape
    n = w.shape[1]
    return pl.pallas_call(
        functools.partial(_in_proj_kernel, n_chunk=min(n, 1024)),
        grid=(bsz, t // tm),
        in_specs=[
            pl.BlockSpec((1, tm, d), lambda i, j: (i, j, 0)),
            pl.BlockSpec((1, 6, d), lambda i, j: (mod_row(i), 0, 0)),
            _const_spec((1, d)),
            _const_spec((d, n)),
            _const_spec((1, n)),
        ],
        out_specs=pl.BlockSpec((1, tm, n), lambda i, j: (i, j, 0)),
        out_shape=jax.ShapeDtypeStruct((bsz, t, n), BF16),
        compiler_params=_cparams("parallel", "parallel"),
        name="in_proj",
    )(x, mod3, g.reshape(1, d), w, b.reshape(1, n))


def _dft_matrix(blk):
    n = 2 * blk
    i = jnp.arange(blk, dtype=jnp.int32)
    ft = (i[:, None] * i[None, :]) % n
    ang = ft.astype(F32) * (2.0 * math.pi / n)
    return jnp.concatenate([jnp.cos(ang), -jnp.sin(ang)], axis=1).astype(BF16)


def _alt_sign(shape):
    row = lax.broadcasted_iota(jnp.int32, shape, 0)
    return (1 - 2 * (row & 1)).astype(F32)


def _filter_kernel(feats_ref, w1_ref, b1_ref, w2_ref, b2_ref, freq_ref, w3p_ref, w3f_ref, decay_ref, skip_ref,
                   cs_ref, spec_ref, nyq_ref, k2_ref, ar_ref, ai_ref, *, blk):
    seq2, cb = k2_ref.shape
    seq = seq2 // 2
    nblk = seq2 // blk
    n = 2 * blk
    freq = freq_ref[...]
    h = jnp.sin(freq * (_dot(feats_ref[...], w1_ref[...]) + b1_ref[...]))
    h = jnp.sin(freq * (_dot(h, w2_ref[...]) + b2_ref[...]))
    row = lax.broadcasted_iota(jnp.int32, (seq2, cb), 0)
    k2 = jnp.where(row < seq, _dot(h, w3f_ref[...]), _dot(h, w3p_ref[...])) * decay_ref[...]
    k2_ref[...] = jnp.where(row == 0, 0.0, k2)
    sign = _alt_sign((blk, cb))
    for d in range(nblk):
        a = k2_ref[d * blk:(d + 1) * blk, :]
        ab = a.astype(BF16)
        ar_ref[d] = _dot(cs_ref[:, 0:blk], ab)
        ai_ref[d] = _dot(cs_ref[:, blk:n], ab)
    lag0 = nblk // 2
    ar_ref[lag0] = ar_ref[lag0] + skip_ref[0]
    k2_ref[seq:seq + 1, :] = k2_ref[seq:seq + 1, :] + skip_ref[0]
    frow = lax.broadcasted_iota(jnp.int32, (blk, cb), 0)
    wgt = jnp.where(frow == 0, 1.0 / n, 2.0 / n)
    for d in range(1, nblk):
        a0 = k2_ref[(d - 1) * blk:(d - 1) * blk + 1, :]
        spec_ref[0, d - 1, 0] = ((ar_ref[d] + sign * (ar_ref[d - 1] - a0)) * wgt).astype(spec_ref.dtype)
        spec_ref[0, d - 1, 1] = ((ai_ref[d] + sign * ai_ref[d - 1]) * wgt).astype(spec_ref.dtype)
        cur = jnp.sum(sign * k2_ref[d * blk:(d + 1) * blk, :], axis=0, keepdims=True)
        prev = jnp.sum(sign * k2_ref[(d - 1) * blk:d * blk, :], axis=0, keepdims=True)
        nyq_ref[0, d - 1] = (cur + prev - a0) * (1.0 / n)


def _filter_spectra(seq, width, w1, b1, w2, b2, w3, freq, skip, cs, cb, blk):
    pos = jnp.abs(jnp.arange(2 * seq, dtype=F32) - seq)
    t = pos / max(seq - 1, 1)
    omega = 2.0 * math.pi * pos / seq
    bands = jnp.linspace(1e-4, FILT_BANDS - 1, FILT_BANDS, dtype=F32)
    ang = omega[:, None] * bands[None, :]
    feats = jnp.concatenate([t[:, None], jnp.cos(ang), -jnp.sin(ang)], axis=-1)
    emb, hid = w1.shape
    emb_pad = -(-emb // 8) * 8
    feats = jnp.pad(feats, ((0, 0), (0, emb_pad - emb)))
    w1 = jnp.pad(w1, ((0, emb_pad - emb), (0, 0)))
    max_decay = math.log(DECAY_TARGET) / FAST_DECAY_PCT
    min_decay = math.log(DECAY_TARGET) / SLOW_DECAY_PCT
    deltas = jnp.linspace(min_decay, max_decay, width, dtype=F32)
    decay = jnp.exp(-t[:, None] * jnp.abs(deltas)[None, :])
    ncb = width // cb
    nblk = 2 * seq // blk
    return pl.pallas_call(
        functools.partial(_filter_kernel, blk=blk),
        grid=(HY_ORDER, ncb),
        in_specs=[
            _const_spec((2 * seq, emb_pad)),
            _const_spec((emb_pad, hid)),
            _const_spec((1, hid)),
            _const_spec((hid, hid)),
            _const_spec((1, hid)),
            _const_spec((1, hid)),
            pl.BlockSpec((hid, cb), lambda o, c: (0, o * ncb + c)),
            pl.BlockSpec((hid, cb), lambda o, c: (0, (HY_ORDER + o) * ncb + c)),
            pl.BlockSpec((2 * seq, cb), lambda o, c: (0, c)),
            pl.BlockSpec((1, 1, cb), lambda o, c: (o, 0, c)),
            _const_spec((blk, 2 * blk)),
        ],
        out_specs=[
            pl.BlockSpec((1, nblk - 1, 2, blk, cb), lambda o, c: (o, 0, 0, 0, c)),
            pl.BlockSpec((1, nblk - 1, 1, cb), lambda o, c: (o, 0, 0, c)),
        ],
        out_shape=[
            jax.ShapeDtypeStruct((HY_ORDER, nblk - 1, 2, blk, width), BF16),
            jax.ShapeDtypeStruct((HY_ORDER, nblk - 1, 1, width), F32),
        ],
        scratch_shapes=[
            pltpu.VMEM((2 * seq, cb), F32),
            pltpu.VMEM((nblk, blk, cb), F32),
            pltpu.VMEM((nblk, blk, cb), F32),
        ],
        compiler_params=_cparams("arbitrary", "arbitrary"),
        name="hyena_filters",
    )(feats, w1, b1.reshape(1, hid), w2, b2.reshape(1, hid), freq.reshape(1, hid), w3, w3, decay,
      skip.reshape(HY_ORDER, 1, width), cs)


def _hyena_kernel(z_ref, sw_ref, sb_ref, spec_ref, nyq_ref, cs_ref, o_ref,
                  u_ref, g_ref, ub_ref, re_ref, im_ref, yri_ref, *, blk, cb):
    seq, width = o_ref.shape[1], o_ref.shape[2]
    nb = seq // blk
    sign = _alt_sign((blk, cb))

    def short_conv(dst_ref, g, cols):
        zc = slice(g * width + cols.start, g * width + cols.stop)
        z = z_ref[0, :, zc].astype(F32)
        w0, w1, w2 = sw_ref[g, 0:1, cols], sw_ref[g, 1:2, cols], sw_ref[g, 2:3, cols]
        dst_ref[...] = sb_ref[g, :, cols] + pltpu.roll(z, 1, 0) * w0 + z * w1 + pltpu.roll(z, seq - 1, 0) * w2
        dst_ref[0:1, :] = dst_ref[0:1, :] - z_ref[0, seq - 1:seq, zc].astype(F32) * w0
        dst_ref[seq - 1:seq, :] = dst_ref[seq - 1:seq, :] - z_ref[0, 0:1, zc].astype(F32) * w2

    def long_conv(o, g, cols):
        ub_ref[...] = u_ref[...].astype(BF16)
        nyq_in = []
        for j in range(nb):
            rows = slice(j * blk, (j + 1) * blk)
            re_ref[j] = _dot(cs_ref[:, 0:blk], ub_ref[rows, :]).astype(BF16)
            im_ref[j] = _dot(cs_ref[:, blk:2 * blk], ub_ref[rows, :]).astype(BF16)
            nyq_in.append(jnp.sum(sign * u_ref[rows, :], axis=0, keepdims=True))
        short_conv(g_ref, g, cols)
        for i in range(nb):
            yr = yi = nyq = None
            for j in range(nb):
                d = i - j + nb - 1
                gr, gi = spec_ref[o, d, 0, :, cols], spec_ref[o, d, 1, :, cols]
                re, im = re_ref[j], im_ref[j]
                tr = re * gr - im * gi
                ti = re * gi + im * gr
                tn = nyq_in[j] * nyq_ref[o, d, :, cols]
                yr, yi, nyq = (tr, ti, tn) if yr is None else (yr + tr, yi + ti, nyq + tn)
            yri_ref[0:blk, :] = yr
            yri_ref[blk:2 * blk, :] = yi
            rows = slice(i * blk, (i + 1) * blk)
            y = _dot(cs_ref[...], yri_ref[...]) + sign * nyq
            g_ref[rows, :] = g_ref[rows, :] * y
        u_ref[...] = g_ref[...]

    for c in range(0, width, cb):
        cols = slice(c, c + cb)
        short_conv(u_ref, 0, cols)
        long_conv(0, 1, cols)
        long_conv(1, 2, cols)
        o_ref[0, :, cols] = u_ref[...].astype(o_ref.dtype)


def _hyena(zx, short_w, short_b, spec, nyq, cs, width, cb, blk):
    bsz, seq, _ = zx.shape
    nb = seq // blk
    sw = short_w.reshape(SHORT_CONV, 3, width).transpose(1, 0, 2)
    sb = short_b.reshape(3, 1, width)
    return pl.pallas_call(
        functools.partial(_hyena_kernel, blk=blk, cb=cb),
        grid=(bsz,),
        in_specs=[
            pl.BlockSpec((1, seq, 3 * width), lambda i: (i, 0, 0)),
            _const_spec(sw.shape),
            _const_spec(sb.shape),
            _const_spec(spec.shape),
            _const_spec(nyq.shape),
            _const_spec((blk, 2 * blk)),
        ],
        out_specs=pl.BlockSpec((1, seq, width), lambda i: (i, 0, 0)),
        out_shape=jax.ShapeDtypeStruct((bsz, seq, width), BF16),
        scratch_shapes=[
            pltpu.VMEM((seq, cb), F32),
            pltpu.VMEM((seq, cb), F32),
            pltpu.VMEM((seq, cb), BF16),
            pltpu.VMEM((nb, blk, cb), BF16),
            pltpu.VMEM((nb, blk, cb), BF16),
            pltpu.VMEM((2 * blk, cb), BF16),
        ],
        compiler_params=_cparams("parallel"),
        name="hyena",
    )(zx, sw, sb, spec, nyq, cs)


def _na_col_bias_kernel(rpb_ref, sel_ref, mask_ref, o_ref):
    acc = jnp.zeros(o_ref.shape, F32) + mask_ref[...]
    for ci in range(sel_ref.shape[0]):
        acc = acc + rpb_ref[:, ci:ci + 1] * sel_ref[ci:ci + 1, :]
    o_ref[...] = acc


def _na_bias_tables(rpb, rows):
    rq, sl, half = NA_ROWS_PER_STEP, NA_SLAB_ROWS, NA_WIN_H // 2
    n_ri, n_ci = 2 * NA_WIN_H - 1, 2 * NA_WIN_W - 1
    qc = np.arange(GRID_W)
    kc = np.arange(GRID_W)
    ws = np.clip(qc - NA_WIN_W // 2, 0, GRID_W - NA_WIN_W)
    in_win = (kc[None, :] >= ws[:, None]) & (kc[None, :] < ws[:, None] + NA_WIN_W)
    ci = np.clip(kc[None, :] - qc[:, None] + NA_WIN_W - 1, 0, n_ci - 1)
    sel = ((ci[None] == np.arange(n_ci)[:, None, None]) & in_win[None]).astype(np.float32)
    sel = sel.reshape(n_ci, GRID_W * GRID_W)
    mask = np.where(in_win, 0.0, MASK_VALUE).astype(np.float32).reshape(1, GRID_W * GRID_W)
    col = pl.pallas_call(
        _na_col_bias_kernel,
        out_shape=jax.ShapeDtypeStruct((NA_HEADS * n_ri, GRID_W * GRID_W), F32),
        name="na_col_bias",
    )(rpb.reshape(NA_HEADS * n_ri, n_ci), jnp.asarray(sel), jnp.asarray(mask))
    col = col.reshape(NA_HEADS, n_ri, GRID_W, GRID_W)

    n_steps = rows // rq
    tables = []
    for step in (0, 1, n_steps - 1):
        start = int(np.clip(step * rq - half, 0, rows - sl))
        per_row = []
        for i in range(rq):
            r = step * rq + i
            rs = int(np.clip(r - half, 0, rows - NA_WIN_H))
            ri0 = rs - r + NA_WIN_H - 1
            blk = col[:, ri0:ri0 + NA_WIN_H].transpose(0, 2, 1, 3)
            lo = rs - start
            blk = jnp.pad(blk, ((0, 0), (0, 0), (lo, sl - NA_WIN_H - lo), (0, 0)), constant_values=MASK_VALUE)
            per_row.append(blk.reshape(NA_HEADS, GRID_W, sl * GRID_W))
        tables.append(jnp.concatenate(per_row, axis=1))
    return jnp.stack(tables)


def _na_kernel(q_ref, k_ref, v_ref, kvc_ref, bias_ref, *rest, rows):
    n_cast = (len(rest) - 1) // 2
    o_ref = rest[n_cast]
    for src, dst in zip(rest[:n_cast], rest[n_cast + 1:]):
        dst[...] = src[...].astype(dst.dtype)
    rq, sl = NA_ROWS_PER_STEP, NA_SLAB_ROWS
    width = NA_HEADS * NA_HEAD_DIM
    step = pl.program_id(1)
    start = jnp.clip(step * rq - NA_WIN_H // 2, 0, rows - sl)
    start = pl.multiple_of(start * GRID_W, GRID_W)
    lane = lax.broadcasted_iota(jnp.int32, (rq * GRID_W, LANES), 1)
    for hp in range(width // LANES):
        cols = slice(hp * LANES, (hp + 1) * LANES)
        q2 = q_ref[0, :, cols]
        k2 = k_ref[0, pl.ds(start, sl * GRID_W), cols]
        v2 = v_ref[0, pl.ds(start, sl * GRID_W), cols]
        kc2 = kvc_ref[0, :, cols]
        vc2 = kvc_ref[0, :, width + hp * LANES:width + (hp + 1) * LANES]
        outs = []
        for s in range(LANES // NA_HEAD_DIM):
            own = (lane >= s * NA_HEAD_DIM) & (lane < (s + 1) * NA_HEAD_DIM)
            qm = jnp.where(own, q2, jnp.zeros_like(q2))
            s_loc = _dot_nt(qm, k2) + bias_ref[0, hp * (LANES // NA_HEAD_DIM) + s]
            s_ctx = _dot_nt(qm, kc2)
            m = jnp.maximum(jnp.max(s_loc, axis=-1, keepdims=True), jnp.max(s_ctx, axis=-1, keepdims=True))
            p_loc = jnp.exp(s_loc - m)
            p_ctx = jnp.exp(s_ctx - m)
            den = jnp.sum(p_loc, axis=-1, keepdims=True) + jnp.sum(p_ctx, axis=-1, keepdims=True)
            o = _dot(p_loc.astype(BF16), v2) + _dot(p_ctx.astype(BF16), vc2)
            outs.append(jnp.where(own, o / den, 0.0))
        o_ref[0, :, cols] = sum(outs).astype(o_ref.dtype)


def _na(zx, kvc, bias, col_q, cast_2d):
    bsz, seq, _ = zx.shape
    rows = seq // GRID_W
    rq = NA_ROWS_PER_STEP
    assert rows % rq == 0 and rows >= NA_SLAB_ROWS and rows // rq >= 3
    n_steps = rows // rq
    width = NA_HEADS * NA_HEAD_DIM
    qb = col_q // width
    total = bsz * n_steps
    for w in cast_2d:
        assert w.shape[0] % (total * BF16_SUBLANES) == 0, (w.shape, total)

    def cfg(s):
        return jnp.minimum(s, 1) + jnp.maximum(s - (n_steps - 2), 0)

    def cast_spec(w):
        return pl.BlockSpec((w.shape[0] // total, w.shape[1]), lambda i, s: (i * n_steps + s, 0))

    outs = pl.pallas_call(
        functools.partial(_na_kernel, rows=rows),
        grid=(bsz, n_steps),
        in_specs=[
            pl.BlockSpec((1, rq * GRID_W, width), lambda i, s: (i, s, qb)),
            pl.BlockSpec((1, seq, width), lambda i, s: (i, 0, qb + 1)),
            pl.BlockSpec((1, seq, width), lambda i, s: (i, 0, qb + 2)),
            pl.BlockSpec((1,) + kvc.shape[1:], lambda i, s: (i, 0, 0)),
            pl.BlockSpec((1,) + bias.shape[1:], lambda i, s: (cfg(s), 0, 0, 0)),
        ] + [cast_spec(w) for w in cast_2d],
        out_specs=[pl.BlockSpec((1, rq * GRID_W, width), lambda i, s: (i, s, 0))] + [cast_spec(w) for w in cast_2d],
        out_shape=[jax.ShapeDtypeStruct((bsz, seq, width), BF16)]
        + [jax.ShapeDtypeStruct(w.shape, BF16) for w in cast_2d],
        compiler_params=_cparams("parallel", "arbitrary"),
        name="na",
    )(zx, zx, zx, kvc, bias, *cast_2d)
    return outs[0], outs[1:]


def _cast_in_na(weights, total_steps):
    max_block_bytes = 1 << 20
    for w in weights:
        rows = w.shape[0] * w.shape[1]
        if rows % (total_steps * BF16_SUBLANES) or rows // total_steps * w.shape[2] * 4 > max_block_bytes:
            return False
    return True


def _merge_kernel(hy_ref, na_ref, ghy_ref, gna_ref, x_ref, mod_ref, g2_ref, wbh_ref, wbn_ref, wo_ref, wr_ref,
                  x1_ref, h2_ref, lg_ref):
    a = _dot(hy_ref[0], wbh_ref[...])
    b = _dot(na_ref[0], wbn_ref[...])
    m = jax.nn.sigmoid(ghy_ref[0].astype(F32)) * a + jax.nn.sigmoid(gna_ref[0].astype(F32)) * b
    mix = _dot(m.astype(BF16), wo_ref[...])
    x1 = x_ref[0] + mod_ref[0, 2:3, :] * mix
    x1_ref[0] = x1
    h2 = _rms_mod(x1, g2_ref[...], mod_ref[0, 3:4, :], mod_ref[0, 4:5, :])
    h2_ref[0] = h2.astype(h2_ref.dtype)
    lg_ref[0] = _dot_nt(wr_ref[...], h2)


def _merge(hy, na, zx, x, mod3, g2, wbh, wbn, wo, wr_t, col_g, tm):
    bsz, seq, d = x.shape
    gb = col_g // d
    ne = wr_t.shape[0]
    return pl.pallas_call(
        _merge_kernel,
        grid=(bsz, seq // tm),
        in_specs=[
            pl.BlockSpec((1, tm, hy.shape[2]), lambda i, j: (i, j, 0)),
            pl.BlockSpec((1, tm, na.shape[2]), lambda i, j: (i, j, 0)),
            pl.BlockSpec((1, tm, d), lambda i, j: (i, j, gb)),
            pl.BlockSpec((1, tm, d), lambda i, j: (i, j, gb + 1)),
            pl.BlockSpec((1, tm, d), lambda i, j: (i, j, 0)),
            pl.BlockSpec((1, 6, d), lambda i, j: (i, 0, 0)),
            _const_spec((1, d)),
            _const_spec(wbh.shape),
            _const_spec(wbn.shape),
            _const_spec(wo.shape),
            _const_spec(wr_t.shape),
        ],
        out_specs=[
            pl.BlockSpec((1, tm, d), lambda i, j: (i, j, 0)),
            pl.BlockSpec((1, tm, d), lambda i, j: (i, j, 0)),
            pl.BlockSpec((1, ne, tm), lambda i, j: (i, 0, j)),
        ],
        out_shape=[
            jax.ShapeDtypeStruct((bsz, seq, d), F32),
            jax.ShapeDtypeStruct((bsz, seq, d), BF16),
            jax.ShapeDtypeStruct((bsz, ne, seq), F32),
        ],
        compiler_params=_cparams("parallel", "parallel"),
        name="merge",
    )(hy, na, zx, zx, x, mod3, g2.reshape(1, d), wbh, wbn, wo, wr_t)


def _route_kernel(lg_ref, tri_ref, rank_ref, rank_t_ref, gate_t_ref, *, cap):
    grp, ne, t = lg_ref.shape
    lg = lg_ref[...]
    e = jnp.exp(lg - jnp.max(lg, axis=1, keepdims=True))
    aff = (e / jnp.sum(e, axis=1, keepdims=True)).reshape(grp * ne, t)

    def bit_step(i, bits):
        cand = bits | (jnp.int32(1) << (30 - i))
        keep = jnp.sum((aff >= pltpu.bitcast(cand, F32)).astype(jnp.int32), axis=1, keepdims=True) >= cap
        return jnp.where(keep, cand, bits)

    thr = pltpu.bitcast(lax.fori_loop(0, 31, bit_step, jnp.zeros((grp * ne, 1), jnp.int32)), F32)
    above = aff > thr
    tie = aff == thr
    need = cap - jnp.sum(above.astype(jnp.int32), axis=1, keepdims=True)
    tri = tri_ref[...]
    tie_before = _dot(tie.astype(BF16), tri)
    sel = above | (tie & (tie_before < need.astype(F32)))
    sel_before = _dot(sel.astype(BF16), tri)
    rank = jnp.where(sel, sel_before, -1.0)
    gate = jnp.where(sel, aff, 0.0)
    rank_ref[...] = rank.astype(jnp.int32).reshape(grp, ne, t)
    pad_r = jnp.full((LANES - ne, t), -1.0, F32)
    pad_g = jnp.zeros((LANES - ne, t), F32)
    for g in range(grp):
        rows = slice(g * ne, (g + 1) * ne)
        rank_t_ref[g] = jnp.concatenate([rank[rows], pad_r], axis=0).T.astype(jnp.int32)
        gate_t_ref[g] = jnp.concatenate([gate[rows], pad_g], axis=0).T


def _route(logits_t, cap, grp):
    bsz, ne, t = logits_t.shape
    i = jnp.arange(t, dtype=jnp.int32)
    tri = (i[:, None] < i[None, :]).astype(BF16)
    return pl.pallas_call(
        functools.partial(_route_kernel, cap=cap),
        grid=(bsz // grp,),
        in_specs=[
            pl.BlockSpec((grp, ne, t), lambda b: (b, 0, 0)),
            _const_spec((t, t)),
        ],
        out_specs=[
            pl.BlockSpec((grp, ne, t), lambda b: (b, 0, 0)),
            pl.BlockSpec((grp, t, LANES), lambda b: (b, 0, 0)),
            pl.BlockSpec((grp, t, LANES), lambda b: (b, 0, 0)),
        ],
        out_shape=[
            jax.ShapeDtypeStruct((bsz, ne, t), jnp.int32),
            jax.ShapeDtypeStruct((bsz, t, LANES), jnp.int32),
            jax.ShapeDtypeStruct((bsz, t, LANES), F32),
        ],
        compiler_params=_cparams("parallel"),
        name="route",
    )(logits_t, tri)


def _expert_kernel(rank_ref, h_ref, wg_ref, wu_ref, wd_ref, y_ref, *, cap, f_chunk):
    e = pl.program_id(0)
    t = h_ref.shape[1]
    f = wg_ref.shape[2]
    rank = rank_ref[0, pl.ds(e, 1), :]
    slot = lax.broadcasted_iota(jnp.int32, (cap, t), 0)
    onehot = jnp.where(rank == slot, 1.0, 0.0).astype(BF16)
    xe = _dot(onehot, h_ref[0]).astype(BF16)
    y = None
    for j in range(0, f, f_chunk):
        g = _dot(xe, wg_ref[0, :, j:j + f_chunk])
        u = _dot(xe, wu_ref[0, :, j:j + f_chunk])
        act = (g * jax.nn.sigmoid(g) * u).astype(BF16)
        part = _dot(act, wd_ref[0, j:j + f_chunk, :])
        y = part if y is None else y + part
    y_ref[0, 0] = y.astype(y_ref.dtype)


def _experts(rank, h2, wg, wu, wd, cap):
    bsz, t, d = h2.shape
    ne, _, f = wg.shape
    f_chunk = f
    return pl.pallas_call(
        functools.partial(_expert_kernel, cap=cap, f_chunk=f_chunk),
        grid=(ne, bsz),
        in_specs=[
            pl.BlockSpec((1, ne, t), lambda e, b: (b, 0, 0)),
            pl.BlockSpec((1, t, d), lambda e, b: (b, 0, 0)),
            pl.BlockSpec((1, d, f), lambda e, b: (e, 0, 0)),
            pl.BlockSpec((1, d, f), lambda e, b: (e, 0, 0)),
            pl.BlockSpec((1, f, d), lambda e, b: (e, 0, 0)),
        ],
        out_specs=pl.BlockSpec((1, 1, cap, d), lambda e, b: (b, e, 0, 0)),
        out_shape=jax.ShapeDtypeStruct((bsz, ne, cap, d), BF16),
        compiler_params=_cparams("arbitrary", "arbitrary"),
        name="experts",
    )(rank, h2, wg, wu, wd)


def _combine_kernel(rank_t_ref, gate_t_ref, y_ref, x1_ref, mod_ref, gf_ref, o_ref, *, cap):
    ne = y_ref.shape[1]
    tm = x1_ref.shape[1]
    slot = lax.broadcasted_iota(jnp.int32, (tm, cap), 1)
    acc = jnp.zeros(x1_ref.shape[1:], F32)
    for e in range(ne):
        r = rank_t_ref[0, :, e:e + 1]
        g = gate_t_ref[0, :, e:e + 1]
        scat = jnp.where(r == slot, g, 0.0).astype(BF16)
        acc = acc + _dot(scat, y_ref[0, e])
    x2 = x1_ref[0] + mod_ref[0, 5:6, :] * acc
    ms = jnp.mean(x2 * x2, axis=-1, keepdims=True)
    o_ref[0] = x2 * lax.rsqrt(ms + EPS) * gf_ref[...]


def _combine(rank_t, gate_t, y, x1, mod3, final_g, cap, tm):
    bsz, t, d = x1.shape
    ne = y.shape[1]
    return pl.pallas_call(
        functools.partial(_combine_kernel, cap=cap),
        grid=(bsz, t // tm),
        in_specs=[
            pl.BlockSpec((1, tm, LANES), lambda i, j: (i, j, 0)),
            pl.BlockSpec((1, tm, LANES), lambda i, j: (i, j, 0)),
            pl.BlockSpec((1, ne, cap, d), lambda i, j: (i, 0, 0, 0)),
            pl.BlockSpec((1, tm, d), lambda i, j: (i, j, 0)),
            pl.BlockSpec((1, 6, d), lambda i, j: (i, 0, 0)),
            _const_spec((1, d)),
        ],
        out_specs=pl.BlockSpec((1, tm, d), lambda i, j: (i, j, 0)),
        out_shape=jax.ShapeDtypeStruct((bsz, t, d), F32),
        compiler_params=_cparams("parallel", "arbitrary"),
        name="combine",
    )(rank_t, gate_t, y, x1, mod3, final_g.reshape(1, d))


def kernel(x, c, ctx, c_ctx, w_mod, b_mod, norm1_g, norm2_g, w_in, b_in, hy_short_w, hy_short_b, hy_skip, filt_w1, filt_b1, filt_w2, filt_b2, filt_w3, filt_freq, na_rpb, w_branch_hy, w_branch_na, w_out, w_router, w_gate, w_up, w_down, final_g):
    depth = w_mod.shape[0]
    bsz, seq, d = x.shape
    hy_width = w_branch_hy.shape[1]
    na_width = w_branch_na.shape[1]
    col_q = 3 * hy_width
    col_k = col_q + na_width
    col_g = col_q + 3 * na_width
    cap = EC_CAPACITY * seq // N_EXPERTS
    rows = seq // GRID_W
    mod_rows = -(-(bsz + 1) // 8) * 8

    hy_blk = HY_BLOCK
    cs = _dft_matrix(hy_blk)
    for i in range(depth):
        assert i == depth - 1, "only the final layer's data flow (context feeds keys/values only) is implemented"
        cc = jnp.zeros((mod_rows, d), F32).at[:bsz].set(c).at[bsz].set(c_ctx)
        mod3 = _modulation(cc, w_mod[i], b_mod[i]).reshape(mod_rows, 6, d)

        qscale = jnp.ones((w_in.shape[2],), F32).at[col_q:col_k].set(NA_HEAD_DIM ** -0.5)
        w_in_s = (w_in[i] * qscale).astype(BF16)
        b_in_s = b_in[i] * qscale
        zx = _in_proj(x, mod3, lambda b: b, norm1_g[i], w_in_s, b_in_s, tm=1024)
        kvc = _in_proj(ctx, mod3, lambda b: bsz, norm1_g[i], w_in_s[:, col_k:col_g], b_in_s[col_k:col_g],
                       tm=ctx.shape[1])

        spec, nyq = _filter_spectra(seq, hy_width, filt_w1[i], filt_b1[i], filt_w2[i], filt_b2[i], filt_w3[i],
                                    filt_freq[i], hy_skip[i], cs, cb=256, blk=hy_blk)
        hy = _hyena(zx, hy_short_w[i], hy_short_b[i], spec, nyq, cs, hy_width, cb=256, blk=hy_blk)

        experts_w = (w_gate[i], w_up[i], w_down[i])
        if _cast_in_na(experts_w, bsz * (rows // NA_ROWS_PER_STEP)):
            na, experts_w = _na(zx, kvc, _na_bias_tables(na_rpb[i], rows), col_q,
                                [w.reshape(-1, w.shape[2]) for w in experts_w])
            experts_w = [w2.reshape(w.shape) for w2, w in zip(experts_w, (w_gate[i], w_up[i], w_down[i]))]
        else:
            na, _ = _na(zx, kvc, _na_bias_tables(na_rpb[i], rows), col_q, [])
            experts_w = [w.astype(BF16) for w in experts_w]

        x1, h2, logits_t = _merge(hy, na, zx, x, mod3, norm2_g[i], w_branch_hy[i].astype(BF16),
                                  w_branch_na[i].astype(BF16), w_out[i].astype(BF16), w_router[i].T, col_g, tm=1024)
        rank, rank_t, gate_t = _route(logits_t, cap, grp=math.gcd(bsz, 8))
        y = _experts(rank, h2, *experts_w, cap)
        x = _combine(rank_t, gate_t, y, x1, mod3, final_g, cap, tm=1024)
    return x
```

```python
import functools
import math

import jax
import jax.numpy as jnp
import numpy as np
from jax import lax
from jax.experimental import pallas as pl
from jax.experimental.pallas import tpu as pltpu

F32 = jnp.float32
BF16 = jnp.bfloat16

EPS = 1e-6
GRID_W = 64
HY_ORDER = 2
SHORT_CONV = 3
FILT_BANDS = 8
DECAY_TARGET = 1e-2
FAST_DECAY_PCT = 0.3
SLOW_DECAY_PCT = 1.5
HY_BLOCK = 512
NA_HEADS = 8
NA_HEAD_DIM = 64
NA_WIN_H = 8
NA_WIN_W = 16
NA_ROWS_PER_STEP = 4
NA_SLAB_ROWS = NA_ROWS_PER_STEP + NA_WIN_H
N_EXPERTS = 16
EC_CAPACITY = 2

MASK_VALUE = -1e30
LOG2_E = 1.4426950408889634
LANES = 128
BF16_SUBLANES = 16
VMEM_LIMIT = 56 * 1024 * 1024


def _cparams(*sem):
    return pltpu.CompilerParams(dimension_semantics=sem, vmem_limit_bytes=VMEM_LIMIT)


def _const_spec(shape):
    nd = len(shape)
    return pl.BlockSpec(shape, lambda *_: (0,) * nd, pipeline_mode=pl.Buffered(1))


def _dot(a, b):
    return jnp.dot(a, b, preferred_element_type=F32)


def _dot_nt(a, b):
    return lax.dot_general(a, b, (((1,), (1,)), ((), ())), preferred_element_type=F32)


def _mod_kernel(c_ref, w_ref, b_ref, o_ref):
    c = c_ref[...]
    s = c * jax.nn.sigmoid(c)
    o_ref[...] = _dot(s, w_ref[...]) + b_ref[...]


def _modulation(cc, w_mod, b_mod):
    rows, d = cc.shape
    n = w_mod.shape[1]
    tn = 1536
    return pl.pallas_call(
        _mod_kernel,
        grid=(n // tn,),
        in_specs=[
            pl.BlockSpec((rows, d), lambda j: (0, 0)),
            pl.BlockSpec((d, tn), lambda j: (0, j)),
            pl.BlockSpec((1, tn), lambda j: (0, j)),
        ],
        out_specs=pl.BlockSpec((rows, tn), lambda j: (0, j)),
        out_shape=jax.ShapeDtypeStruct((rows, n), F32),
        compiler_params=_cparams("arbitrary"),
        name="modulation",
    )(cc, w_mod, b_mod.reshape(1, n))


def _rms_mod(x, g, shift, scale):
    ms = jnp.mean(x * x, axis=-1, keepdims=True)
    return (x * lax.rsqrt(ms + EPS) * g) * (1.0 + scale) + shift


def _in_proj_kernel(x_ref, mod_ref, g_ref, w_ref, b_ref, o_ref, *, n_chunk):
    h = _rms_mod(x_ref[0], g_ref[...], mod_ref[0, 0:1, :], mod_ref[0, 1:2, :]).astype(BF16)
    n = w_ref.shape[1]
    for j in range(0, n, n_chunk):
        z = _dot(h, w_ref[:, j:j + n_chunk]) + b_ref[:, j:j + n_chunk]
        o_ref[0, :, j:j + n_chunk] = z.astype(o_ref.dtype)


def _in_proj(x, mod3, mod_row, g, w, b, tm):
    bsz, t, d = x.shape
    n = w.shape[1]
    return pl.pallas_call(
        functools.partial(_in_proj_kernel, n_chunk=min(n, 1024)),
        grid=(bsz, t // tm),
        in_specs=[
            pl.BlockSpec((1, tm, d), lambda i, j: (i, j, 0)),
            pl.BlockSpec((1, 6, d), lambda i, j: (mod_row(i), 0, 0)),
            _const_spec((1, d)),
            _const_spec((d, n)),
            _const_spec((1, n)),
        ],
        out_specs=pl.BlockSpec((1, tm, n), lambda i, j: (i, j, 0)),
        out_shape=jax.ShapeDtypeStruct((bsz, t, n), BF16),
        compiler_params=_cparams("parallel", "parallel"),
        name="in_proj",
    )(x, mod3, g.reshape(1, d), w, b.reshape(1, n))


def _dft_matrix(blk):
    n = 2 * blk
    i = jnp.arange(blk, dtype=jnp.int32)
    ft = (i[:, None] * i[None, :]) % n
    ang = ft.astype(F32) * (2.0 * math.pi / n)
    return jnp.concatenate([jnp.cos(ang), -jnp.sin(ang)], axis=1).astype(BF16)


def _alt_sign(shape):
    row = lax.broadcasted_iota(jnp.int32, shape, 0)
    return (1 - 2 * (row & 1)).astype(F32)


def _filter_kernel(feats_ref, w1_ref, b1_ref, w2_ref, b2_ref, freq_ref, w3p_ref, w3f_ref, decay_ref, skip_ref,
                   cs_ref, spec_ref, nyq_ref, h_ref, k2_ref, ar_ref, ai_ref, *, blk):
    seq2, cb = k2_ref.shape
    seq = seq2 // 2
    nblk = seq2 // blk
    n = 2 * blk

    @pl.when((pl.program_id(0) == 0) & (pl.program_id(1) == 0))
    def _():
        freq = freq_ref[...]
        h1 = jnp.sin(freq * (_dot(feats_ref[...], w1_ref[...]) + b1_ref[...]))
        h_ref[...] = jnp.sin(freq * (_dot(h1, w2_ref[...]) + b2_ref[...]))

    h = h_ref[...]
    row = lax.broadcasted_iota(jnp.int32, (seq2, cb), 0)
    k2 = jnp.where(row < seq, _dot(h, w3f_ref[...]), _dot(h, w3p_ref[...])) * decay_ref[...]
    k2_ref[...] = jnp.where(row == 0, 0.0, k2)
    sign = _alt_sign((blk, cb))
    for d in range(nblk):
        a = k2_ref[d * blk:(d + 1) * blk, :]
        ab = a.astype(BF16)
        ar_ref[d] = _dot(cs_ref[:, 0:blk], ab)
        ai_ref[d] = _dot(cs_ref[:, blk:n], ab)
    lag0 = nblk // 2
    ar_ref[lag0] = ar_ref[lag0] + skip_ref[0]
    k2_ref[seq:seq + 1, :] = k2_ref[seq:seq + 1, :] + skip_ref[0]
    frow = lax.broadcasted_iota(jnp.int32, (blk, cb), 0)
    wgt = jnp.where(frow == 0, 1.0 / n, 2.0 / n)
    for d in range(1, nblk):
        a0 = k2_ref[(d - 1) * blk:(d - 1) * blk + 1, :]
        spec_ref[0, d - 1, 0] = ((ar_ref[d] + sign * (ar_ref[d - 1] - a0)) * wgt).astype(spec_ref.dtype)
        spec_ref[0, d - 1, 1] = ((ai_ref[d] + sign * ai_ref[d - 1]) * wgt).astype(spec_ref.dtype)
        cur = jnp.sum(sign * k2_ref[d * blk:(d + 1) * blk, :], axis=0, keepdims=True)
        prev = jnp.sum(sign * k2_ref[(d - 1) * blk:d * blk, :], axis=0, keepdims=True)
        nyq_ref[0, d - 1] = (cur + prev - a0) * (1.0 / n)


def _filter_spectra(seq, width, w1, b1, w2, b2, w3, freq, skip, cs, cb, blk):
    pos = jnp.abs(jnp.arange(2 * seq, dtype=F32) - seq)
    t = pos / max(seq - 1, 1)
    omega = 2.0 * math.pi * pos / seq
    bands = jnp.linspace(1e-4, FILT_BANDS - 1, FILT_BANDS, dtype=F32)
    ang = omega[:, None] * bands[None, :]
    feats = jnp.concatenate([t[:, None], jnp.cos(ang), -jnp.sin(ang)], axis=-1)
    emb, hid = w1.shape
    emb_pad = -(-emb // 8) * 8
    feats = jnp.pad(feats, ((0, 0), (0, emb_pad - emb)))
    w1 = jnp.pad(w1, ((0, emb_pad - emb), (0, 0)))
    max_decay = math.log(DECAY_TARGET) / FAST_DECAY_PCT
    min_decay = math.log(DECAY_TARGET) / SLOW_DECAY_PCT
    deltas = jnp.linspace(min_decay, max_decay, width, dtype=F32)
    decay = jnp.exp(-t[:, None] * jnp.abs(deltas)[None, :])
    ncb = width // cb
    nblk = 2 * seq // blk
    return pl.pallas_call(
        functools.partial(_filter_kernel, blk=blk),
        grid=(HY_ORDER, ncb),
        in_specs=[
            _const_spec((2 * seq, emb_pad)),
            _const_spec((emb_pad, hid)),
            _const_spec((1, hid)),
            _const_spec((hid, hid)),
            _const_spec((1, hid)),
            _const_spec((1, hid)),
            pl.BlockSpec((hid, cb), lambda o, c: (0, o * ncb + c)),
            pl.BlockSpec((hid, cb), lambda o, c: (0, (HY_ORDER + o) * ncb + c)),
            pl.BlockSpec((2 * seq, cb), lambda o, c: (0, c)),
            pl.BlockSpec((1, 1, cb), lambda o, c: (o, 0, c)),
            _const_spec((blk, 2 * blk)),
        ],
        out_specs=[
            pl.BlockSpec((1, nblk - 1, 2, blk, cb), lambda o, c: (o, 0, 0, 0, c)),
            pl.BlockSpec((1, nblk - 1, 1, cb), lambda o, c: (o, 0, 0, c)),
        ],
        out_shape=[
            jax.ShapeDtypeStruct((HY_ORDER, nblk - 1, 2, blk, width), BF16),
            jax.ShapeDtypeStruct((HY_ORDER, nblk - 1, 1, width), F32),
        ],
        scratch_shapes=[
            pltpu.VMEM((2 * seq, hid), F32),
            pltpu.VMEM((2 * seq, cb), F32),
            pltpu.VMEM((nblk, blk, cb), F32),
            pltpu.VMEM((nblk, blk, cb), F32),
        ],
        compiler_params=_cparams("arbitrary", "arbitrary"),
        name="hyena_filters",
    )(feats, w1, b1.reshape(1, hid), w2, b2.reshape(1, hid), freq.reshape(1, hid), w3, w3, decay,
      skip.reshape(HY_ORDER, 1, width), cs)


def _hyena_kernel(z_ref, sw_ref, sb_ref, spec_ref, nyq_ref, cs_ref, o_ref,
                  u_ref, g_ref, ub_ref, re_ref, im_ref, yri_ref, *, blk, cb):
    seq, width = o_ref.shape[1], o_ref.shape[2]
    nb = seq // blk
    sign = _alt_sign((blk, cb))

    def short_conv(dst_ref, g, cols):
        zc = slice(g * width + cols.start, g * width + cols.stop)
        z = z_ref[0, :, zc].astype(F32)
        w0, w1, w2 = sw_ref[g, 0:1, cols], sw_ref[g, 1:2, cols], sw_ref[g, 2:3, cols]
        dst_ref[...] = sb_ref[g, :, cols] + pltpu.roll(z, 1, 0) * w0 + z * w1 + pltpu.roll(z, seq - 1, 0) * w2
        dst_ref[0:1, :] = dst_ref[0:1, :] - z_ref[0, seq - 1:seq, zc].astype(F32) * w0
        dst_ref[seq - 1:seq, :] = dst_ref[seq - 1:seq, :] - z_ref[0, 0:1, zc].astype(F32) * w2

    def long_conv(o, g, cols):
        ub_ref[...] = u_ref[...].astype(BF16)
        nyq_in = []
        for j in range(nb):
            rows = slice(j * blk, (j + 1) * blk)
            re_ref[j] = _dot(cs_ref[:, 0:blk], ub_ref[rows, :]).astype(BF16)
            im_ref[j] = _dot(cs_ref[:, blk:2 * blk], ub_ref[rows, :]).astype(BF16)
            nyq_in.append(jnp.sum(sign * u_ref[rows, :], axis=0, keepdims=True))
        short_conv(g_ref, g, cols)
        for i in range(nb):
            yr = yi = nyq = None
            for j in range(nb):
                d = i - j + nb - 1
                gr, gi = spec_ref[o, d, 0, :, cols], spec_ref[o, d, 1, :, cols]
                re, im = re_ref[j], im_ref[j]
                tr = re * gr - im * gi
                ti = re * gi + im * gr
                tn = nyq_in[j] * nyq_ref[o, d, :, cols]
                yr, yi, nyq = (tr, ti, tn) if yr is None else (yr + tr, yi + ti, nyq + tn)
            yri_ref[0:blk, :] = yr
            yri_ref[blk:2 * blk, :] = yi
            rows = slice(i * blk, (i + 1) * blk)
            y = _dot(cs_ref[...], yri_ref[...]) + sign * nyq
            g_ref[rows, :] = g_ref[rows, :] * y
        u_ref[...] = g_ref[...]

    for c in range(0, width, cb):
        cols = slice(c, c + cb)
        short_conv(u_ref, 0, cols)
        long_conv(0, 1, cols)
        long_conv(1, 2, cols)
        o_ref[0, :, cols] = u_ref[...].astype(o_ref.dtype)


def _hyena(zx, short_w, short_b, spec, nyq, cs, width, cb, blk):
    bsz, seq, _ = zx.shape
    nb = seq // blk
    sw = short_w.reshape(SHORT_CONV, 3, width).transpose(1, 0, 2)
    sb = short_b.reshape(3, 1, width)
    return pl.pallas_call(
        functools.partial(_hyena_kernel, blk=blk, cb=cb),
        grid=(bsz,),
        in_specs=[
            pl.BlockSpec((1, seq, 3 * width), lambda i: (i, 0, 0)),
            _const_spec(sw.shape),
            _const_spec(sb.shape),
            _const_spec(spec.shape),
            _const_spec(nyq.shape),
            _const_spec((blk, 2 * blk)),
        ],
        out_specs=pl.BlockSpec((1, seq, width), lambda i: (i, 0, 0)),
        out_shape=jax.ShapeDtypeStruct((bsz, seq, width), BF16),
        scratch_shapes=[
            pltpu.VMEM((seq, cb), F32),
            pltpu.VMEM((seq, cb), F32),
            pltpu.VMEM((seq, cb), BF16),
            pltpu.VMEM((nb, blk, cb), BF16),
            pltpu.VMEM((nb, blk, cb), BF16),
            pltpu.VMEM((2 * blk, cb), BF16),
        ],
        compiler_params=_cparams("parallel"),
        name="hyena",
    )(zx, sw, sb, spec, nyq, cs)


def _na_col_bias_kernel(rpb_ref, sel_ref, mask_ref, rowmask_ref, o_ref):
    acc = jnp.zeros(o_ref.shape, F32) + mask_ref[...] + rowmask_ref[...]
    for ci in range(sel_ref.shape[0]):
        acc = acc + (rpb_ref[:, ci:ci + 1] * LOG2_E) * sel_ref[ci:ci + 1, :]
    o_ref[...] = acc


def _na_bias_tables(rpb, rows):
    rq, sl, half = NA_ROWS_PER_STEP, NA_SLAB_ROWS, NA_WIN_H // 2
    n_ri, n_ci = 2 * NA_WIN_H - 1, 2 * NA_WIN_W - 1
    qc = np.arange(GRID_W)
    kc = np.arange(GRID_W)
    ws = np.clip(qc - NA_WIN_W // 2, 0, GRID_W - NA_WIN_W)
    in_win = (kc[None, :] >= ws[:, None]) & (kc[None, :] < ws[:, None] + NA_WIN_W)
    ci = np.clip(kc[None, :] - qc[:, None] + NA_WIN_W - 1, 0, n_ci - 1)
    sel = ((ci[None] == np.arange(n_ci)[:, None, None]) & in_win[None]).astype(np.float32)
    sel = sel.reshape(n_ci, GRID_W * GRID_W)
    mask = np.where(in_win, 0.0, MASK_VALUE).astype(np.float32).reshape(1, GRID_W * GRID_W)
    rowmask = np.tile(np.where(np.arange(n_ri + 1) == n_ri, MASK_VALUE, 0.0).astype(np.float32), NA_HEADS)
    rpb_x = jnp.pad(rpb, ((0, 0), (0, 1), (0, 0))).reshape(NA_HEADS * (n_ri + 1), n_ci)
    col = pl.pallas_call(
        _na_col_bias_kernel,
        out_shape=jax.ShapeDtypeStruct((NA_HEADS * (n_ri + 1), GRID_W * GRID_W), F32),
        name="na_col_bias",
    )(rpb_x, jnp.asarray(sel), jnp.asarray(mask), jnp.asarray(rowmask.reshape(-1, 1)))
    col = col.reshape(NA_HEADS, n_ri + 1, GRID_W, GRID_W)

    n_steps = rows // rq
    steps = (0, 1, n_steps - 1)
    ri = np.full((len(steps), rq, sl), n_ri, np.int32)
    for c, step in enumerate(steps):
        start = int(np.clip(step * rq - half, 0, rows - sl))
        for i in range(rq):
            r = step * rq + i
            rs = int(np.clip(r - half, 0, rows - NA_WIN_H))
            for j in range(NA_WIN_H):
                ri[c, i, rs - start + j] = rs + j - r + NA_WIN_H - 1
    tab = jnp.take(col, jnp.asarray(ri.reshape(-1)), axis=1)
    tab = tab.reshape(NA_HEADS, len(steps), rq, sl, GRID_W, GRID_W).transpose(1, 0, 2, 4, 3, 5)
    return tab.reshape(len(steps), NA_HEADS, rq * GRID_W, sl * GRID_W)


def _na_kernel(q_ref, k_ref, v_ref, kvc_ref, bias_ref, *rest, rows):
    n_cast = (len(rest) - 1) // 2
    o_ref = rest[n_cast]
    for src, dst in zip(rest[:n_cast], rest[n_cast + 1:]):
        dst[...] = src[...].astype(dst.dtype)
    rq, sl = NA_ROWS_PER_STEP, NA_SLAB_ROWS
    width = NA_HEADS * NA_HEAD_DIM
    step = pl.program_id(1)
    start = jnp.clip(step * rq - NA_WIN_H // 2, 0, rows - sl)
    start = pl.multiple_of(start * GRID_W, GRID_W)
    lane = lax.broadcasted_iota(jnp.int32, (rq * GRID_W, LANES), 1)

    def own_keys(s):
        masks = []
        for n in (sl * GRID_W, kvc_ref.shape[1]):
            kl = lax.broadcasted_iota(jnp.int32, (n, LANES), 1)
            masks.append((kl >= s * NA_HEAD_DIM) & (kl < (s + 1) * NA_HEAD_DIM))
        return masks

    for hp in range(width // LANES):
        cols = slice(hp * LANES, (hp + 1) * LANES)
        q2 = q_ref[0, :, cols]
        k2 = k_ref[0, pl.ds(start, sl * GRID_W), cols]
        v2 = v_ref[0, pl.ds(start, sl * GRID_W), cols]
        kc2 = kvc_ref[0, :, cols]
        vc2 = kvc_ref[0, :, width + hp * LANES:width + (hp + 1) * LANES]
        outs = []
        for s in range(LANES // NA_HEAD_DIM):
            own = (lane >= s * NA_HEAD_DIM) & (lane < (s + 1) * NA_HEAD_DIM)
            qm = jnp.where(own, q2, jnp.zeros_like(q2))
            s_loc = _dot_nt(qm, k2) + bias_ref[0, hp * (LANES // NA_HEAD_DIM) + s]
            s_ctx = _dot_nt(qm, kc2)
            m = jnp.maximum(jnp.max(s_loc, axis=-1, keepdims=True), jnp.max(s_ctx, axis=-1, keepdims=True))
            p_loc = jnp.exp2(s_loc - m)
            p_ctx = jnp.exp2(s_ctx - m)
            own_loc, own_ctx = own_keys(s)
            o = (_dot(p_loc.astype(BF16), jnp.where(own_loc, v2, jnp.ones_like(v2)))
                 + _dot(p_ctx.astype(BF16), jnp.where(own_ctx, vc2, jnp.ones_like(vc2))))
            den = pltpu.roll(o, NA_HEAD_DIM, 1)
            outs.append(jnp.where(own, o / den, 0.0))
        o_ref[0, :, cols] = sum(outs).astype(o_ref.dtype)


def _na(zx, kvc, bias, col_q, cast_2d):
    bsz, seq, _ = zx.shape
    rows = seq // GRID_W
    rq = NA_ROWS_PER_STEP
    assert rows % rq == 0 and rows >= NA_SLAB_ROWS and rows // rq >= 3
    n_steps = rows // rq
    width = NA_HEADS * NA_HEAD_DIM
    qb = col_q // width
    total = bsz * n_steps
    for w in cast_2d:
        assert w.shape[0] % (total * BF16_SUBLANES) == 0, (w.shape, total)

    def cfg(s):
        return jnp.minimum(s, 1) + jnp.maximum(s - (n_steps - 2), 0)

    def cast_spec(w):
        return pl.BlockSpec((w.shape[0] // total, w.shape[1]), lambda i, s: (i * n_steps + s, 0))

    outs = pl.pallas_call(
        functools.partial(_na_kernel, rows=rows),
        grid=(bsz, n_steps),
        in_specs=[
            pl.BlockSpec((1, rq * GRID_W, width), lambda i, s: (i, s, qb)),
            pl.BlockSpec((1, seq, width), lambda i, s: (i, 0, qb + 1)),
            pl.BlockSpec((1, seq, width), lambda i, s: (i, 0, qb + 2)),
            pl.BlockSpec((1,) + kvc.shape[1:], lambda i, s: (i, 0, 0)),
            pl.BlockSpec((1,) + bias.shape[1:], lambda i, s: (cfg(s), 0, 0, 0)),
        ] + [cast_spec(w) for w in cast_2d],
        out_specs=[pl.BlockSpec((1, rq * GRID_W, width), lambda i, s: (i, s, 0))] + [cast_spec(w) for w in cast_2d],
        out_shape=[jax.ShapeDtypeStruct((bsz, seq, width), BF16)]
        + [jax.ShapeDtypeStruct(w.shape, BF16) for w in cast_2d],
        compiler_params=_cparams("parallel", "arbitrary"),
        name="na",
    )(zx, zx, zx, kvc, bias, *cast_2d)
    return outs[0], outs[1:]


def _cast_in_na(weights, total_steps):
    max_block_bytes = 1 << 20
    for w in weights:
        rows = w.shape[0] * w.shape[1]
        if rows % (total_steps * BF16_SUBLANES) or rows // total_steps * w.shape[2] * 4 > max_block_bytes:
            return False
    return True


def _merge_kernel(hy_ref, na_ref, ghy_ref, gna_ref, x_ref, mod_ref, g2_ref, wbh_ref, wbn_ref, wo_ref, wr_ref,
                  x1_ref, h2_ref, lg_ref):
    a = _dot(hy_ref[0], wbh_ref[...])
    b = _dot(na_ref[0], wbn_ref[...])
    m = jax.nn.sigmoid(ghy_ref[0].astype(F32)) * a + jax.nn.sigmoid(gna_ref[0].astype(F32)) * b
    mix = _dot(m.astype(BF16), wo_ref[...])
    x1 = x_ref[0] + mod_ref[0, 2:3, :] * mix
    x1_ref[0] = x1
    h2 = _rms_mod(x1, g2_ref[...], mod_ref[0, 3:4, :], mod_ref[0, 4:5, :])
    h2_ref[0] = h2.astype(h2_ref.dtype)
    lg_ref[0] = _dot_nt(wr_ref[...], h2)


def _merge(hy, na, zx, x, mod3, g2, wbh, wbn, wo, wr_t, col_g, tm):
    bsz, seq, d = x.shape
    gb = col_g // d
    ne = wr_t.shape[0]
    return pl.pallas_call(
        _merge_kernel,
        grid=(bsz, seq // tm),
        in_specs=[
            pl.BlockSpec((1, tm, hy.shape[2]), lambda i, j: (i, j, 0)),
            pl.BlockSpec((1, tm, na.shape[2]), lambda i, j: (i, j, 0)),
            pl.BlockSpec((1, tm, d), lambda i, j: (i, j, gb)),
            pl.BlockSpec((1, tm, d), lambda i, j: (i, j, gb + 1)),
            pl.BlockSpec((1, tm, d), lambda i, j: (i, j, 0)),
            pl.BlockSpec((1, 6, d), lambda i, j: (i, 0, 0)),
            _const_spec((1, d)),
            _const_spec(wbh.shape),
            _const_spec(wbn.shape),
            _const_spec(wo.shape),
            _const_spec(wr_t.shape),
        ],
        out_specs=[
            pl.BlockSpec((1, tm, d), lambda i, j: (i, j, 0)),
            pl.BlockSpec((1, tm, d), lambda i, j: (i, j, 0)),
            pl.BlockSpec((1, ne, tm), lambda i, j: (i, 0, j)),
        ],
        out_shape=[
            jax.ShapeDtypeStruct((bsz, seq, d), F32),
            jax.ShapeDtypeStruct((bsz, seq, d), BF16),
            jax.ShapeDtypeStruct((bsz, ne, seq), F32),
        ],
        compiler_params=_cparams("parallel", "parallel"),
        name="merge",
    )(hy, na, zx, zx, x, mod3, g2.reshape(1, d), wbh, wbn, wo, wr_t)


def _route_kernel(lg_ref, tri_ref, rank_ref, rank_t_ref, gate_t_ref, *, cap):
    grp, ne, t = lg_ref.shape
    lg = lg_ref[...]
    e = jnp.exp(lg - jnp.max(lg, axis=1, keepdims=True))
    aff = (e / jnp.sum(e, axis=1, keepdims=True)).reshape(grp * ne, t)

    def bit_step(i, bits):
        cand = bits | (jnp.int32(1) << (30 - i))
        keep = jnp.sum((aff >= pltpu.bitcast(cand, F32)).astype(jnp.int32), axis=1, keepdims=True) >= cap
        return jnp.where(keep, cand, bits)

    thr = pltpu.bitcast(lax.fori_loop(0, 31, bit_step, jnp.zeros((grp * ne, 1), jnp.int32)), F32)
    above = aff > thr
    tie = aff == thr
    need = cap - jnp.sum(above.astype(jnp.int32), axis=1, keepdims=True)
    tri = tri_ref[...]
    tie_before = _dot(tie.astype(BF16), tri)
    sel = above | (tie & (tie_before < need.astype(F32)))
    sel_before = _dot(sel.astype(BF16), tri)
    rank = jnp.where(sel, sel_before, -1.0)
    gate = jnp.where(sel, aff, 0.0)
    rank_ref[...] = rank.astype(jnp.int32).reshape(grp, ne, t)
    pad_r = jnp.full((LANES - ne, t), -1.0, F32)
    pad_g = jnp.zeros((LANES - ne, t), F32)
    for g in range(grp):
        rows = slice(g * ne, (g + 1) * ne)
        rank_t_ref[g] = jnp.concatenate([rank[rows], pad_r], axis=0).T.astype(jnp.int32)
        gate_t_ref[g] = jnp.concatenate([gate[rows], pad_g], axis=0).T


def _route(logits_t, cap, grp):
    bsz, ne, t = logits_t.shape
    i = jnp.arange(t, dtype=jnp.int32)
    tri = (i[:, None] < i[None, :]).astype(BF16)
    return pl.pallas_call(
        functools.partial(_route_kernel, cap=cap),
        grid=(bsz // grp,),
        in_specs=[
            pl.BlockSpec((grp, ne, t), lambda b: (b, 0, 0)),
            _const_spec((t, t)),
        ],
        out_specs=[
            pl.BlockSpec((grp, ne, t), lambda b: (b, 0, 0)),
            pl.BlockSpec((grp, t, LANES), lambda b: (b, 0, 0)),
            pl.BlockSpec((grp, t, LANES), lambda b: (b, 0, 0)),
        ],
        out_shape=[
            jax.ShapeDtypeStruct((bsz, ne, t), jnp.int32),
            jax.ShapeDtypeStruct((bsz, t, LANES), jnp.int32),
            jax.ShapeDtypeStruct((bsz, t, LANES), F32),
        ],
        compiler_params=_cparams("parallel"),
        name="route",
    )(logits_t, tri)


def _expert_kernel(rank_ref, h_ref, wg_ref, wu_ref, wd_ref, y_ref, *, cap, f_chunk):
    e = pl.program_id(0)
    t = h_ref.shape[1]
    f = wg_ref.shape[2]
    rank = rank_ref[0, pl.ds(e, 1), :]
    slot = lax.broadcasted_iota(jnp.int32, (cap, t), 0)
    onehot = jnp.where(rank == slot, 1.0, 0.0).astype(BF16)
    xe = _dot(onehot, h_ref[0]).astype(BF16)
    y = None
    for j in range(0, f, f_chunk):
        g = _dot(xe, wg_ref[0, :, j:j + f_chunk])
        u = _dot(xe, wu_ref[0, :, j:j + f_chunk])
        act = (g * jax.nn.sigmoid(g) * u).astype(BF16)
        part = _dot(act, wd_ref[0, j:j + f_chunk, :])
        y = part if y is None else y + part
    y_ref[0, 0] = y.astype(y_ref.dtype)


def _experts(rank, h2, wg, wu, wd, cap):
    bsz, t, d = h2.shape
    ne, _, f = wg.shape
    f_chunk = f
    return pl.pallas_call(
        functools.partial(_expert_kernel, cap=cap, f_chunk=f_chunk),
        grid=(ne, bsz),
        in_specs=[
            pl.BlockSpec((1, ne, t), lambda e, b: (b, 0, 0)),
            pl.BlockSpec((1, t, d), lambda e, b: (b, 0, 0)),
            pl.BlockSpec((1, d, f), lambda e, b: (e, 0, 0)),
            pl.BlockSpec((1, d, f), lambda e, b: (e, 0, 0)),
            pl.BlockSpec((1, f, d), lambda e, b: (e, 0, 0)),
        ],
        out_specs=pl.BlockSpec((1, 1, cap, d), lambda e, b: (b, e, 0, 0)),
        out_shape=jax.ShapeDtypeStruct((bsz, ne, cap, d), BF16),
        compiler_params=_cparams("arbitrary", "arbitrary"),
        name="experts",
    )(rank, h2, wg, wu, wd)


def _combine_kernel(rank_t_ref, gate_t_ref, y_ref, x1_ref, mod_ref, gf_ref, o_ref, *, cap):
    ne = y_ref.shape[1]
    tm = x1_ref.shape[1]
    slot = lax.broadcasted_iota(jnp.int32, (tm, cap), 1)
    acc = jnp.zeros(x1_ref.shape[1:], F32)
    for e in range(ne):
        r = rank_t_ref[0, :, e:e + 1]
        g = gate_t_ref[0, :, e:e + 1]
        scat = jnp.where(r == slot, g, 0.0).astype(BF16)
        acc = acc + _dot(scat, y_ref[0, e])
    x2 = x1_ref[0] + mod_ref[0, 5:6, :] * acc
    ms = jnp.mean(x2 * x2, axis=-1, keepdims=True)
    o_ref[0] = x2 * lax.rsqrt(ms + EPS) * gf_ref[...]


def _combine(rank_t, gate_t, y, x1, mod3, final_g, cap, tm):
    bsz, t, d = x1.shape
    ne = y.shape[1]
    return pl.pallas_call(
        functools.partial(_combine_kernel, cap=cap),
        grid=(bsz, t // tm),
        in_specs=[
            pl.BlockSpec((1, tm, LANES), lambda i, j: (i, j, 0)),
            pl.BlockSpec((1, tm, LANES), lambda i, j: (i, j, 0)),
            pl.BlockSpec((1, ne, cap, d), lambda i, j: (i, 0, 0, 0)),
            pl.BlockSpec((1, tm, d), lambda i, j: (i, j, 0)),
            pl.BlockSpec((1, 6, d), lambda i, j: (i, 0, 0)),
            _const_spec((1, d)),
        ],
        out_specs=pl.BlockSpec((1, tm, d), lambda i, j: (i, j, 0)),
        out_shape=jax.ShapeDtypeStruct((bsz, t, d), F32),
        compiler_params=_cparams("parallel", "arbitrary"),
        name="combine",
    )(rank_t, gate_t, y, x1, mod3, final_g.reshape(1, d))


def kernel(x, c, ctx, c_ctx, w_mod, b_mod, norm1_g, norm2_g, w_in, b_in, hy_short_w, hy_short_b, hy_skip, filt_w1, filt_b1, filt_w2, filt_b2, filt_w3, filt_freq, na_rpb, w_branch_hy, w_branch_na, w_out, w_router, w_gate, w_up, w_down, final_g):
    depth = w_mod.shape[0]
    bsz, seq, d = x.shape
    hy_width = w_branch_hy.shape[1]
    na_width = w_branch_na.shape[1]
    col_q = 3 * hy_width
    col_k = col_q + na_width
    col_g = col_q + 3 * na_width
    cap = EC_CAPACITY * seq // N_EXPERTS
    rows = seq // GRID_W
    mod_rows = -(-(bsz + 1) // 8) * 8

    hy_blk = HY_BLOCK
    cs = _dft_matrix(hy_blk)
    for i in range(depth):
        assert i == depth - 1, "only the final layer's data flow (context feeds keys/values only) is implemented"
        cc = jnp.zeros((mod_rows, d), F32).at[:bsz].set(c).at[bsz].set(c_ctx)
        mod3 = _modulation(cc, w_mod[i], b_mod[i]).reshape(mod_rows, 6, d)

        qscale = jnp.ones((w_in.shape[2],), F32).at[col_q:col_k].set(NA_HEAD_DIM ** -0.5 * LOG2_E)
        w_in_s = (w_in[i] * qscale).astype(BF16)
        b_in_s = b_in[i] * qscale
        zx = _in_proj(x, mod3, lambda b: b, norm1_g[i], w_in_s, b_in_s, tm=1024)
        kvc = _in_proj(ctx.reshape(1, -1, d), mod3, lambda b: bsz, norm1_g[i], w_in_s[:, col_k:col_g],
                       b_in_s[col_k:col_g], tm=math.gcd(bsz * ctx.shape[1], 1024)).reshape(bsz, ctx.shape[1], -1)

        spec, nyq = _filter_spectra(seq, hy_width, filt_w1[i], filt_b1[i], filt_w2[i], filt_b2[i], filt_w3[i],
                                    filt_freq[i], hy_skip[i], cs, cb=256, blk=hy_blk)
        hy = _hyena(zx, hy_short_w[i], hy_short_b[i], spec, nyq, cs, hy_width, cb=256, blk=hy_blk)

        experts_w = (w_gate[i], w_up[i], w_down[i])
        if _cast_in_na(experts_w, bsz * (rows // NA_ROWS_PER_STEP)):
            na, experts_w = _na(zx, kvc, _na_bias_tables(na_rpb[i], rows), col_q,
                                [w.reshape(-1, w.shape[2]) for w in experts_w])
            experts_w = [w2.reshape(w.shape) for w2, w in zip(experts_w, (w_gate[i], w_up[i], w_down[i]))]
        else:
            na, _ = _na(zx, kvc, _na_bias_tables(na_rpb[i], rows), col_q, [])
            experts_w = [w.astype(BF16) for w in experts_w]

        x1, h2, logits_t = _merge(hy, na, zx, x, mod3, norm2_g[i], w_branch_hy[i].astype(BF16),
                                  w_branch_na[i].astype(BF16), w_out[i].astype(BF16), w_router[i].T, col_g, tm=1024)
        rank, rank_t, gate_t = _route(logits_t, cap, grp=math.gcd(bsz, 8))
        y = _experts(rank, h2, *experts_w, cap)
        x = _combine(rank_t, gate_t, y, x1, mod3, final_g, cap, tm=1024)
    return x
```

```python
import functools
import math

import jax
import jax.numpy as jnp
import numpy as np
from jax import lax
from jax.experimental import pallas as pl
from jax.experimental.pallas import tpu as pltpu

F32 = jnp.float32
BF16 = jnp.bfloat16

EPS = 1e-6
GRID_W = 64
HY_ORDER = 2
SHORT_CONV = 3
FILT_BANDS = 8
DECAY_TARGET = 1e-2
FAST_DECAY_PCT = 0.3
SLOW_DECAY_PCT = 1.5
HY_BLOCK = 512
NA_HEADS = 8
NA_HEAD_DIM = 64
NA_WIN_H = 8
NA_WIN_W = 16
NA_ROWS_PER_STEP = 4
NA_SLAB_ROWS = NA_ROWS_PER_STEP + NA_WIN_H
N_EXPERTS = 16
EC_CAPACITY = 2

MASK_VALUE = -1e30
LOG2_E = 1.4426950408889634
LANES = 128
BF16_SUBLANES = 16
VMEM_LIMIT = 56 * 1024 * 1024


def _cparams(*sem):
    return pltpu.CompilerParams(dimension_semantics=sem, vmem_limit_bytes=VMEM_LIMIT)


def _const_spec(shape):
    nd = len(shape)
    return pl.BlockSpec(shape, lambda *_: (0,) * nd, pipeline_mode=pl.Buffered(1))


def _dot(a, b):
    return jnp.dot(a, b, preferred_element_type=F32)


def _dot_nt(a, b):
    return lax.dot_general(a, b, (((1,), (1,)), ((), ())), preferred_element_type=F32)


def _mod_kernel(c_ref, w_ref, b_ref, o_ref):
    c = c_ref[...]
    s = c * jax.nn.sigmoid(c)
    o_ref[...] = _dot(s, w_ref[...]) + b_ref[...]


def _modulation(cc, w_mod, b_mod):
    rows, d = cc.shape
    n = w_mod.shape[1]
    tn = 1536
    return pl.pallas_call(
        _mod_kernel,
        grid=(n // tn,),
        in_specs=[
            pl.BlockSpec((rows, d), lambda j: (0, 0)),
            pl.BlockSpec((d, tn), lambda j: (0, j)),
            pl.BlockSpec((1, tn), lambda j: (0, j)),
        ],
        out_specs=pl.BlockSpec((rows, tn), lambda j: (0, j)),
        out_shape=jax.ShapeDtypeStruct((rows, n), F32),
        compiler_params=_cparams("arbitrary"),
        name="modulation",
    )(cc, w_mod, b_mod.reshape(1, n))


def _rms_mod(x, g, shift, scale):
    ms = jnp.mean(x * x, axis=-1, keepdims=True)
    return (x * lax.rsqrt(ms + EPS) * g) * (1.0 + scale) + shift


def _in_proj_kernel(x_ref, mod_ref, g_ref, w_ref, b_ref, o_ref, *, n_chunk):
    h = _rms_mod(x_ref[0], g_ref[...], mod_ref[0, 0:1, :], mod_ref[0, 1:2, :]).astype(BF16)
    n = w_ref.shape[1]
    for j in range(0, n, n_chunk):
        z = _dot(h, w_ref[:, j:j + n_chunk]) + b_ref[:, j:j + n_chunk]
        o_ref[0, :, j:j + n_chunk] = z.astype(o_ref.dtype)


def _in_proj(x, mod3, mod_row, g, w, b, tm):
    bsz, t, d = x.shape
    n = w.shape[1]
    return pl.pallas_call(
        functools.partial(_in_proj_kernel, n_chunk=min(n, 1024)),
        grid=(bsz, t // tm),
        in_specs=[
            pl.BlockSpec((1, tm, d), lambda i, j: (i, j, 0)),
            pl.BlockSpec((1, 6, d), lambda i, j: (mod_row(i), 0, 0)),
            _const_spec((1, d)),
            _const_spec((d, n)),
            _const_spec((1, n)),
        ],
        out_specs=pl.BlockSpec((1, tm, n), lambda i, j: (i, j, 0)),
        out_shape=jax.ShapeDtypeStruct((bsz, t, n), BF16),
        compiler_params=_cparams("parallel", "parallel"),
        name="in_proj",
    )(x, mod3, g.reshape(1, d), w, b.reshape(1, n))


def _dft_matrix(blk):
    n = 2 * blk
    i = jnp.arange(blk, dtype=jnp.int32)
    ft = (i[:, None] * i[None, :]) % n
    ang = ft.astype(F32) * (2.0 * math.pi / n)
    return jnp.concatenate([jnp.cos(ang), -jnp.sin(ang)], axis=1).astype(BF16)


def _alt_sign(shape):
    row = lax.broadcasted_iota(jnp.int32, shape, 0)
    return (1 - 2 * (row & 1)).astype(F32)


def _filter_kernel(feats_ref, w1_ref, b1_ref, w2_ref, b2_ref, freq_ref, w3p_ref, w3f_ref, decay_ref, skip_ref,
                   cs_ref, spec_ref, nyq_ref, h_ref, k2_ref, ar_ref, ai_ref, *, blk):
    seq2, cb = k2_ref.shape
    seq = seq2 // 2
    nblk = seq2 // blk
    n = 2 * blk

    @pl.when((pl.program_id(0) == 0) & (pl.program_id(1) == 0))
    def _():
        freq = freq_ref[...]
        h1 = jnp.sin(freq * (_dot(feats_ref[...], w1_ref[...]) + b1_ref[...]))
        h_ref[...] = jnp.sin(freq * (_dot(h1, w2_ref[...]) + b2_ref[...]))

    h = h_ref[...]
    row = lax.broadcasted_iota(jnp.int32, (seq2, cb), 0)
    k2 = jnp.where(row < seq, _dot(h, w3f_ref[...]), _dot(h, w3p_ref[...])) * decay_ref[...]
    k2_ref[...] = jnp.where(row == 0, 0.0, k2)
    sign = _alt_sign((blk, cb))
    for d in range(nblk):
        a = k2_ref[d * blk:(d + 1) * blk, :]
        ab = a.astype(BF16)
        ar_ref[d] = _dot(cs_ref[:, 0:blk], ab)
        ai_ref[d] = _dot(cs_ref[:, blk:n], ab)
    lag0 = nblk // 2
    ar_ref[lag0] = ar_ref[lag0] + skip_ref[0]
    k2_ref[seq:seq + 1, :] = k2_ref[seq:seq + 1, :] + skip_ref[0]
    frow = lax.broadcasted_iota(jnp.int32, (blk, cb), 0)
    wgt = jnp.where(frow == 0, 1.0 / n, 2.0 / n)
    for d in range(1, nblk):
        a0 = k2_ref[(d - 1) * blk:(d - 1) * blk + 1, :]
        spec_ref[0, d - 1, 0] = ((ar_ref[d] + sign * (ar_ref[d - 1] - a0)) * wgt).astype(spec_ref.dtype)
        spec_ref[0, d - 1, 1] = ((ai_ref[d] + sign * ai_ref[d - 1]) * wgt).astype(spec_ref.dtype)
        cur = jnp.sum(sign * k2_ref[d * blk:(d + 1) * blk, :], axis=0, keepdims=True)
        prev = jnp.sum(sign * k2_ref[(d - 1) * blk:d * blk, :], axis=0, keepdims=True)
        nyq_ref[0, d - 1] = (cur + prev - a0) * (1.0 / n)


def _filter_spectra(seq, width, w1, b1, w2, b2, w3, freq, skip, cs, cb, blk):
    pos = jnp.abs(jnp.arange(2 * seq, dtype=F32) - seq)
    t = pos / max(seq - 1, 1)
    omega = 2.0 * math.pi * pos / seq
    bands = jnp.linspace(1e-4, FILT_BANDS - 1, FILT_BANDS, dtype=F32)
    ang = omega[:, None] * bands[None, :]
    feats = jnp.concatenate([t[:, None], jnp.cos(ang), -jnp.sin(ang)], axis=-1)
    emb, hid = w1.shape
    emb_pad = -(-emb // 8) * 8
    feats = jnp.pad(feats, ((0, 0), (0, emb_pad - emb)))
    w1 = jnp.pad(w1, ((0, emb_pad - emb), (0, 0)))
    max_decay = math.log(DECAY_TARGET) / FAST_DECAY_PCT
    min_decay = math.log(DECAY_TARGET) / SLOW_DECAY_PCT
    deltas = jnp.linspace(min_decay, max_decay, width, dtype=F32)
    decay = jnp.exp(-t[:, None] * jnp.abs(deltas)[None, :])
    ncb = width // cb
    nblk = 2 * seq // blk
    return pl.pallas_call(
        functools.partial(_filter_kernel, blk=blk),
        grid=(HY_ORDER, ncb),
        in_specs=[
            _const_spec((2 * seq, emb_pad)),
            _const_spec((emb_pad, hid)),
            _const_spec((1, hid)),
            _const_spec((hid, hid)),
            _const_spec((1, hid)),
            _const_spec((1, hid)),
            pl.BlockSpec((hid, cb), lambda o, c: (0, o * ncb + c)),
            pl.BlockSpec((hid, cb), lambda o, c: (0, (HY_ORDER + o) * ncb + c)),
            pl.BlockSpec((2 * seq, cb), lambda o, c: (0, c)),
            pl.BlockSpec((1, 1, cb), lambda o, c: (o, 0, c)),
            _const_spec((blk, 2 * blk)),
        ],
        out_specs=[
            pl.BlockSpec((1, nblk - 1, 2, blk, cb), lambda o, c: (o, 0, 0, 0, c)),
            pl.BlockSpec((1, nblk - 1, 1, cb), lambda o, c: (o, 0, 0, c)),
        ],
        out_shape=[
            jax.ShapeDtypeStruct((HY_ORDER, nblk - 1, 2, blk, width), BF16),
            jax.ShapeDtypeStruct((HY_ORDER, nblk - 1, 1, width), F32),
        ],
        scratch_shapes=[
            pltpu.VMEM((2 * seq, hid), F32),
            pltpu.VMEM((2 * seq, cb), F32),
            pltpu.VMEM((nblk, blk, cb), F32),
            pltpu.VMEM((nblk, blk, cb), F32),
        ],
        compiler_params=_cparams("arbitrary", "arbitrary"),
        name="hyena_filters",
    )(feats, w1, b1.reshape(1, hid), w2, b2.reshape(1, hid), freq.reshape(1, hid), w3, w3, decay,
      skip.reshape(HY_ORDER, 1, width), cs)


def _hyena_kernel(z_ref, sw_ref, sb_ref, spec_ref, nyq_ref, cs_ref, o_ref,
                  u_ref, g_ref, ub_ref, re_ref, im_ref, yri_ref, *, blk, cb):
    seq, width = o_ref.shape[1], o_ref.shape[2]
    nb = seq // blk
    sign = _alt_sign((blk, cb))

    def short_conv(dst_ref, g, cols):
        zc = slice(g * width + cols.start, g * width + cols.stop)
        z = z_ref[0, :, zc].astype(F32)
        w0, w1, w2 = sw_ref[g, 0:1, cols], sw_ref[g, 1:2, cols], sw_ref[g, 2:3, cols]
        dst_ref[...] = sb_ref[g, :, cols] + pltpu.roll(z, 1, 0) * w0 + z * w1 + pltpu.roll(z, seq - 1, 0) * w2
        dst_ref[0:1, :] = dst_ref[0:1, :] - z_ref[0, seq - 1:seq, zc].astype(F32) * w0
        dst_ref[seq - 1:seq, :] = dst_ref[seq - 1:seq, :] - z_ref[0, 0:1, zc].astype(F32) * w2

    def long_conv(o, g, cols):
        ub_ref[...] = u_ref[...].astype(BF16)
        nyq_in = []
        for j in range(nb):
            rows = slice(j * blk, (j + 1) * blk)
            re_ref[j] = _dot(cs_ref[:, 0:blk], ub_ref[rows, :]).astype(BF16)
            im_ref[j] = _dot(cs_ref[:, blk:2 * blk], ub_ref[rows, :]).astype(BF16)
            nyq_in.append(jnp.sum(sign * u_ref[rows, :], axis=0, keepdims=True))
        short_conv(g_ref, g, cols)
        for i in range(nb):
            yr = yi = nyq = None
            for j in range(nb):
                d = i - j + nb - 1
                gr, gi = spec_ref[o, d, 0, :, cols], spec_ref[o, d, 1, :, cols]
                re, im = re_ref[j], im_ref[j]
                tr = re * gr - im * gi
                ti = re * gi + im * gr
                tn = nyq_in[j] * nyq_ref[o, d, :, cols]
                yr, yi, nyq = (tr, ti, tn) if yr is None else (yr + tr, yi + ti, nyq + tn)
            yri_ref[0:blk, :] = yr
            yri_ref[blk:2 * blk, :] = yi
            rows = slice(i * blk, (i + 1) * blk)
            y = _dot(cs_ref[...], yri_ref[...]) + sign * nyq
            g_ref[rows, :] = g_ref[rows, :] * y
        u_ref[...] = g_ref[...]

    for c in range(0, width, cb):
        cols = slice(c, c + cb)
        short_conv(u_ref, 0, cols)
        long_conv(0, 1, cols)
        long_conv(1, 2, cols)
        o_ref[0, :, cols] = u_ref[...].astype(o_ref.dtype)


def _hyena(zx, short_w, short_b, spec, nyq, cs, width, cb, blk):
    bsz, seq, _ = zx.shape
    nb = seq // blk
    sw = short_w.reshape(SHORT_CONV, 3, width).transpose(1, 0, 2)
    sb = short_b.reshape(3, 1, width)
    return pl.pallas_call(
        functools.partial(_hyena_kernel, blk=blk, cb=cb),
        grid=(bsz,),
        in_specs=[
            pl.BlockSpec((1, seq, 3 * width), lambda i: (i, 0, 0)),
            _const_spec(sw.shape),
            _const_spec(sb.shape),
            _const_spec(spec.shape),
            _const_spec(nyq.shape),
            _const_spec((blk, 2 * blk)),
        ],
        out_specs=pl.BlockSpec((1, seq, width), lambda i: (i, 0, 0)),
        out_shape=jax.ShapeDtypeStruct((bsz, seq, width), BF16),
        scratch_shapes=[
            pltpu.VMEM((seq, cb), F32),
            pltpu.VMEM((seq, cb), F32),
            pltpu.VMEM((seq, cb), BF16),
            pltpu.VMEM((nb, blk, cb), BF16),
            pltpu.VMEM((nb, blk, cb), BF16),
            pltpu.VMEM((2 * blk, cb), BF16),
        ],
        compiler_params=_cparams("parallel"),
        name="hyena",
    )(zx, sw, sb, spec, nyq, cs)


def _na_col_bias_kernel(rpb_ref, sel_ref, mask_ref, rowmask_ref, o_ref):
    acc = jnp.zeros(o_ref.shape, F32) + mask_ref[...] + rowmask_ref[...]
    for ci in range(sel_ref.shape[0]):
        acc = acc + (rpb_ref[:, ci:ci + 1] * LOG2_E) * sel_ref[ci:ci + 1, :]
    o_ref[...] = acc


def _na_bias_tables(rpb, rows):
    rq, sl, half = NA_ROWS_PER_STEP, NA_SLAB_ROWS, NA_WIN_H // 2
    n_ri, n_ci = 2 * NA_WIN_H - 1, 2 * NA_WIN_W - 1
    qc = np.arange(GRID_W)
    kc = np.arange(GRID_W)
    ws = np.clip(qc - NA_WIN_W // 2, 0, GRID_W - NA_WIN_W)
    in_win = (kc[None, :] >= ws[:, None]) & (kc[None, :] < ws[:, None] + NA_WIN_W)
    ci = np.clip(kc[None, :] - qc[:, None] + NA_WIN_W - 1, 0, n_ci - 1)
    sel = ((ci[None] == np.arange(n_ci)[:, None, None]) & in_win[None]).astype(np.float32)
    sel = sel.reshape(n_ci, GRID_W * GRID_W)
    mask = np.where(in_win, 0.0, MASK_VALUE).astype(np.float32).reshape(1, GRID_W * GRID_W)
    rowmask = np.tile(np.where(np.arange(n_ri + 1) == n_ri, MASK_VALUE, 0.0).astype(np.float32), NA_HEADS)
    rpb_x = jnp.pad(rpb, ((0, 0), (0, 1), (0, 0))).reshape(NA_HEADS * (n_ri + 1), n_ci)
    col = pl.pallas_call(
        _na_col_bias_kernel,
        out_shape=jax.ShapeDtypeStruct((NA_HEADS * (n_ri + 1), GRID_W * GRID_W), F32),
        name="na_col_bias",
    )(rpb_x, jnp.asarray(sel), jnp.asarray(mask), jnp.asarray(rowmask.reshape(-1, 1)))
    col = col.reshape(NA_HEADS, n_ri + 1, GRID_W, GRID_W)

    n_steps = rows // rq
    steps = (0, 1, n_steps - 1)
    ri = np.full((len(steps), rq, sl), n_ri, np.int32)
    for c, step in enumerate(steps):
        start = int(np.clip(step * rq - half, 0, rows - sl))
        for i in range(rq):
            r = step * rq + i
            rs = int(np.clip(r - half, 0, rows - NA_WIN_H))
            for j in range(NA_WIN_H):
                ri[c, i, rs - start + j] = rs + j - r + NA_WIN_H - 1
    tab = jnp.take(col, jnp.asarray(ri.reshape(-1)), axis=1)
    tab = tab.reshape(NA_HEADS, len(steps), rq, sl, GRID_W, GRID_W).transpose(1, 0, 2, 4, 3, 5)
    return tab.reshape(len(steps), NA_HEADS, rq * GRID_W, sl * GRID_W)


def _na_kernel(q_ref, k_ref, v_ref, kvc_ref, bias_ref, *rest, rows):
    n_cast = (len(rest) - 1) // 2
    o_ref = rest[n_cast]
    for src, dst in zip(rest[:n_cast], rest[n_cast + 1:]):
        dst[...] = src[...].astype(dst.dtype)
    rq, sl = NA_ROWS_PER_STEP, NA_SLAB_ROWS
    width = NA_HEADS * NA_HEAD_DIM
    step = pl.program_id(1)
    start = jnp.clip(step * rq - NA_WIN_H // 2, 0, rows - sl)
    start = pl.multiple_of(start * GRID_W, GRID_W)
    nh = LANES // NA_HEAD_DIM
    nq = rq * GRID_W
    lane = lax.broadcasted_iota(jnp.int32, (nq, LANES), 1)
    owns = [(lane >= s * NA_HEAD_DIM) & (lane < (s + 1) * NA_HEAD_DIM) for s in range(nh)]

    for hp in range(width // LANES):
        cols = slice(hp * LANES, (hp + 1) * LANES)
        q2 = q_ref[0, :, cols]
        k2 = k_ref[0, pl.ds(start, sl * GRID_W), cols]
        v2 = v_ref[0, pl.ds(start, sl * GRID_W), cols]
        kc2 = kvc_ref[0, :, cols]
        vc2 = kvc_ref[0, :, width + hp * LANES:width + (hp + 1) * LANES]
        qm = jnp.concatenate([jnp.where(own, q2, jnp.zeros_like(q2)) for own in owns], axis=0)
        s_loc = _dot_nt(qm, k2) + bias_ref[0, hp * nh:(hp + 1) * nh].reshape(nh * nq, sl * GRID_W)
        s_ctx = _dot_nt(qm, kc2)
        m = jnp.maximum(jnp.max(s_loc, axis=-1, keepdims=True), jnp.max(s_ctx, axis=-1, keepdims=True))
        p_loc = jnp.exp2(s_loc - m)
        p_ctx = jnp.exp2(s_ctx - m)
        den = jnp.sum(p_loc, axis=-1, keepdims=True) + jnp.sum(p_ctx, axis=-1, keepdims=True)
        o = (_dot(p_loc.astype(BF16), v2) + _dot(p_ctx.astype(BF16), vc2)) / den
        o_ref[0, :, cols] = sum(jnp.where(owns[s], o[s * nq:(s + 1) * nq], 0.0) for s in range(nh)).astype(o_ref.dtype)


def _na(zx, kvc, bias, col_q, cast_2d):
    bsz, seq, _ = zx.shape
    rows = seq // GRID_W
    rq = NA_ROWS_PER_STEP
    assert rows % rq == 0 and rows >= NA_SLAB_ROWS and rows // rq >= 3
    n_steps = rows // rq
    width = NA_HEADS * NA_HEAD_DIM
    qb = col_q // width
    total = bsz * n_steps
    for w in cast_2d:
        assert w.shape[0] % (total * BF16_SUBLANES) == 0, (w.shape, total)

    def cfg(s):
        return jnp.minimum(s, 1) + jnp.maximum(s - (n_steps - 2), 0)

    def cast_spec(w):
        return pl.BlockSpec((w.shape[0] // total, w.shape[1]), lambda i, s: (i * n_steps + s, 0))

    outs = pl.pallas_call(
        functools.partial(_na_kernel, rows=rows),
        grid=(bsz, n_steps),
        in_specs=[
            pl.BlockSpec((1, rq * GRID_W, width), lambda i, s: (i, s, qb)),
            pl.BlockSpec((1, seq, width), lambda i, s: (i, 0, qb + 1)),
            pl.BlockSpec((1, seq, width), lambda i, s: (i, 0, qb + 2)),
            pl.BlockSpec((1,) + kvc.shape[1:], lambda i, s: (i, 0, 0)),
            pl.BlockSpec((1,) + bias.shape[1:], lambda i, s: (cfg(s), 0, 0, 0)),
        ] + [cast_spec(w) for w in cast_2d],
        out_specs=[pl.BlockSpec((1, rq * GRID_W, width), lambda i, s: (i, s, 0))] + [cast_spec(w) for w in cast_2d],
        out_shape=[jax.ShapeDtypeStruct((bsz, seq, width), BF16)]
        + [jax.ShapeDtypeStruct(w.shape, BF16) for w in cast_2d],
        compiler_params=_cparams("parallel", "arbitrary"),
        name="na",
    )(zx, zx, zx, kvc, bias, *cast_2d)
    return outs[0], outs[1:]


def _cast_in_na(weights, total_steps):
    max_block_bytes = 1 << 20
    for w in weights:
        rows = w.shape[0] * w.shape[1]
        if rows % (total_steps * BF16_SUBLANES) or rows // total_steps * w.shape[2] * 4 > max_block_bytes:
            return False
    return True


def _merge_kernel(hy_ref, na_ref, ghy_ref, gna_ref, x_ref, mod_ref, g2_ref, wbh_ref, wbn_ref, wo_ref, wr_ref,
                  x1_ref, h2_ref, lg_ref):
    a = _dot(hy_ref[0], wbh_ref[...])
    b = _dot(na_ref[0], wbn_ref[...])
    m = jax.nn.sigmoid(ghy_ref[0].astype(F32)) * a + jax.nn.sigmoid(gna_ref[0].astype(F32)) * b
    mix = _dot(m.astype(BF16), wo_ref[...])
    x1 = x_ref[0] + mod_ref[0, 2:3, :] * mix
    x1_ref[0] = x1
    h2 = _rms_mod(x1, g2_ref[...], mod_ref[0, 3:4, :], mod_ref[0, 4:5, :])
    h2_ref[0] = h2.astype(h2_ref.dtype)
    lg_ref[0] = _dot_nt(wr_ref[...], h2)


def _merge(hy, na, zx, x, mod3, g2, wbh, wbn, wo, wr_t, col_g, tm):
    bsz, seq, d = x.shape
    gb = col_g // d
    ne = wr_t.shape[0]
    return pl.pallas_call(
        _merge_kernel,
        grid=(bsz, seq // tm),
        in_specs=[
            pl.BlockSpec((1, tm, hy.shape[2]), lambda i, j: (i, j, 0)),
            pl.BlockSpec((1, tm, na.shape[2]), lambda i, j: (i, j, 0)),
            pl.BlockSpec((1, tm, d), lambda i, j: (i, j, gb)),
            pl.BlockSpec((1, tm, d), lambda i, j: (i, j, gb + 1)),
            pl.BlockSpec((1, tm, d), lambda i, j: (i, j, 0)),
            pl.BlockSpec((1, 6, d), lambda i, j: (i, 0, 0)),
            _const_spec((1, d)),
            _const_spec(wbh.shape),
            _const_spec(wbn.shape),
            _const_spec(wo.shape),
            _const_spec(wr_t.shape),
        ],
        out_specs=[
            pl.BlockSpec((1, tm, d), lambda i, j: (i, j, 0)),
            pl.BlockSpec((1, tm, d), lambda i, j: (i, j, 0)),
            pl.BlockSpec((1, ne, tm), lambda i, j: (i, 0, j)),
        ],
        out_shape=[
            jax.ShapeDtypeStruct((bsz, seq, d), F32),
            jax.ShapeDtypeStruct((bsz, seq, d), BF16),
            jax.ShapeDtypeStruct((bsz, ne, seq), F32),
        ],
        compiler_params=_cparams("parallel", "parallel"),
        name="merge",
    )(hy, na, zx, zx, x, mod3, g2.reshape(1, d), wbh, wbn, wo, wr_t)


def _route_kernel(lg_ref, tri_ref, rank_ref, rank_t_ref, gate_t_ref, *, cap):
    grp, ne, t = lg_ref.shape
    lg = lg_ref[...]
    e = jnp.exp(lg - jnp.max(lg, axis=1, keepdims=True))
    aff = (e / jnp.sum(e, axis=1, keepdims=True)).reshape(grp * ne, t)

    def bit_step(i, bits):
        cand = bits | (jnp.int32(1) << (30 - i))
        keep = jnp.sum((aff >= pltpu.bitcast(cand, F32)).astype(jnp.int32), axis=1, keepdims=True) >= cap
        return jnp.where(keep, cand, bits)

    thr = pltpu.bitcast(lax.fori_loop(0, 31, bit_step, jnp.zeros((grp * ne, 1), jnp.int32)), F32)
    above = aff > thr
    tie = aff == thr
    need = cap - jnp.sum(above.astype(jnp.int32), axis=1, keepdims=True)
    tri = tri_ref[...]
    tie_before = _dot(tie.astype(BF16), tri)
    sel = above | (tie & (tie_before < need.astype(F32)))
    sel_before = _dot(sel.astype(BF16), tri)
    rank = jnp.where(sel, sel_before, -1.0)
    gate = jnp.where(sel, aff, 0.0)
    rank_ref[...] = rank.astype(jnp.int32).reshape(grp, ne, t)
    pad_r = jnp.full((LANES - ne, t), -1.0, F32)
    pad_g = jnp.zeros((LANES - ne, t), F32)
    for g in range(grp):
        rows = slice(g * ne, (g + 1) * ne)
        rank_t_ref[g] = jnp.concatenate([rank[rows], pad_r], axis=0).T.astype(jnp.int32)
        gate_t_ref[g] = jnp.concatenate([gate[rows], pad_g], axis=0).T


def _route(logits_t, cap, grp):
    bsz, ne, t = logits_t.shape
    i = jnp.arange(t, dtype=jnp.int32)
    tri = (i[:, None] < i[None, :]).astype(BF16)
    return pl.pallas_call(
        functools.partial(_route_kernel, cap=cap),
        grid=(bsz // grp,),
        in_specs=[
            pl.BlockSpec((grp, ne, t), lambda b: (b, 0, 0)),
            _const_spec((t, t)),
        ],
        out_specs=[
            pl.BlockSpec((grp, ne, t), lambda b: (b, 0, 0)),
            pl.BlockSpec((grp, t, LANES), lambda b: (b, 0, 0)),
            pl.BlockSpec((grp, t, LANES), lambda b: (b, 0, 0)),
        ],
        out_shape=[
            jax.ShapeDtypeStruct((bsz, ne, t), jnp.int32),
            jax.ShapeDtypeStruct((bsz, t, LANES), jnp.int32),
            jax.ShapeDtypeStruct((bsz, t, LANES), F32),
        ],
        compiler_params=_cparams("parallel"),
        name="route",
    )(logits_t, tri)


def _expert_kernel(rank_ref, h_ref, wg_ref, wu_ref, wd_ref, y_ref, *, cap, f_chunk):
    e = pl.program_id(0)
    t = h_ref.shape[1]
    f = wg_ref.shape[2]
    rank = rank_ref[0, pl.ds(e, 1), :]
    slot = lax.broadcasted_iota(jnp.int32, (cap, t), 0)
    onehot = jnp.where(rank == slot, 1.0, 0.0).astype(BF16)
    xe = _dot(onehot, h_ref[0]).astype(BF16)
    y = None
    for j in range(0, f, f_chunk):
        g = _dot(xe, wg_ref[0, :, j:j + f_chunk])
        u = _dot(xe, wu_ref[0, :, j:j + f_chunk])
        act = (g * jax.nn.sigmoid(g) * u).astype(BF16)
        part = _dot(act, wd_ref[0, j:j + f_chunk, :])
        y = part if y is None else y + part
    y_ref[0, 0] = y.astype(y_ref.dtype)


def _experts(rank, h2, wg, wu, wd, cap):
    bsz, t, d = h2.shape
    ne, _, f = wg.shape
    f_chunk = f
    return pl.pallas_call(
        functools.partial(_expert_kernel, cap=cap, f_chunk=f_chunk),
        grid=(ne, bsz),
        in_specs=[
            pl.BlockSpec((1, ne, t), lambda e, b: (b, 0, 0)),
            pl.BlockSpec((1, t, d), lambda e, b: (b, 0, 0)),
            pl.BlockSpec((1, d, f), lambda e, b: (e, 0, 0)),
            pl.BlockSpec((1, d, f), lambda e, b: (e, 0, 0)),
            pl.BlockSpec((1, f, d), lambda e, b: (e, 0, 0)),
        ],
        out_specs=pl.BlockSpec((1, 1, cap, d), lambda e, b: (b, e, 0, 0)),
        out_shape=jax.ShapeDtypeStruct((bsz, ne, cap, d), BF16),
        compiler_params=_cparams("arbitrary", "arbitrary"),
        name="experts",
    )(rank, h2, wg, wu, wd)


def _combine_kernel(rank_t_ref, gate_t_ref, y_ref, x1_ref, mod_ref, gf_ref, o_ref, *, cap):
    ne = y_ref.shape[1]
    tm = x1_ref.shape[1]
    slot = lax.broadcasted_iota(jnp.int32, (tm, cap), 1)
    acc = jnp.zeros(x1_ref.shape[1:], F32)
    for e in range(ne):
        r = rank_t_ref[0, :, e:e + 1]
        g = gate_t_ref[0, :, e:e + 1]
        scat = jnp.where(r == slot, g, 0.0).astype(BF16)
        acc = acc + _dot(scat, y_ref[0, e])
    x2 = x1_ref[0] + mod_ref[0, 5:6, :] * acc
    ms = jnp.mean(x2 * x2, axis=-1, keepdims=True)
    o_ref[0] = x2 * lax.rsqrt(ms + EPS) * gf_ref[...]


def _combine(rank_t, gate_t, y, x1, mod3, final_g, cap, tm):
    bsz, t, d = x1.shape
    ne = y.shape[1]
    return pl.pallas_call(
        functools.partial(_combine_kernel, cap=cap),
        grid=(bsz, t // tm),
        in_specs=[
            pl.BlockSpec((1, tm, LANES), lambda i, j: (i, j, 0)),
            pl.BlockSpec((1, tm, LANES), lambda i, j: (i, j, 0)),
            pl.BlockSpec((1, ne, cap, d), lambda i, j: (i, 0, 0, 0)),
            pl.BlockSpec((1, tm, d), lambda i, j: (i, j, 0)),
            pl.BlockSpec((1, 6, d), lambda i, j: (i, 0, 0)),
            _const_spec((1, d)),
        ],
        out_specs=pl.BlockSpec((1, tm, d), lambda i, j: (i, j, 0)),
        out_shape=jax.ShapeDtypeStruct((bsz, t, d), F32),
        compiler_params=_cparams("parallel", "arbitrary"),
        name="combine",
    )(rank_t, gate_t, y, x1, mod3, final_g.reshape(1, d))


def kernel(x, c, ctx, c_ctx, w_mod, b_mod, norm1_g, norm2_g, w_in, b_in, hy_short_w, hy_short_b, hy_skip, filt_w1, filt_b1, filt_w2, filt_b2, filt_w3, filt_freq, na_rpb, w_branch_hy, w_branch_na, w_out, w_router, w_gate, w_up, w_down, final_g):
    depth = w_mod.shape[0]
    bsz, seq, d = x.shape
    hy_width = w_branch_hy.shape[1]
    na_width = w_branch_na.shape[1]
    col_q = 3 * hy_width
    col_k = col_q + na_width
    col_g = col_q + 3 * na_width
    cap = EC_CAPACITY * seq // N_EXPERTS
    rows = seq // GRID_W
    mod_rows = -(-(bsz + 1) // 8) * 8

    hy_blk = HY_BLOCK
    cs = _dft_matrix(hy_blk)
    for i in range(depth):
        assert i == depth - 1, "only the final layer's data flow (context feeds keys/values only) is implemented"
        cc = jnp.zeros((mod_rows, d), F32).at[:bsz].set(c).at[bsz].set(c_ctx)
        mod3 = _modulation(cc, w_mod[i], b_mod[i]).reshape(mod_rows, 6, d)

        qscale = jnp.ones((w_in.shape[2],), F32).at[col_q:col_k].set(NA_HEAD_DIM ** -0.5 * LOG2_E)
        w_in_s = (w_in[i] * qscale).astype(BF16)
        b_in_s = b_in[i] * qscale
        zx = _in_proj(x, mod3, lambda b: b, norm1_g[i], w_in_s, b_in_s, tm=1024)
        kvc = _in_proj(ctx.reshape(1, -1, d), mod3, lambda b: bsz, norm1_g[i], w_in_s[:, col_k:col_g],
                       b_in_s[col_k:col_g], tm=math.gcd(bsz * ctx.shape[1], 1024)).reshape(bsz, ctx.shape[1], -1)

        spec, nyq = _filter_spectra(seq, hy_width, filt_w1[i], filt_b1[i], filt_w2[i], filt_b2[i], filt_w3[i],
                                    filt_freq[i], hy_skip[i], cs, cb=256, blk=hy_blk)
        hy = _hyena(zx, hy_short_w[i], hy_short_b[i], spec, nyq, cs, hy_width, cb=256, blk=hy_blk)

        experts_w = (w_gate[i], w_up[i], w_down[i])
        if _cast_in_na(experts_w, bsz * (rows // NA_ROWS_PER_STEP)):
            na, experts_w = _na(zx, kvc, _na_bias_tables(na_rpb[i], rows), col_q,
                                [w.reshape(-1, w.shape[2]) for w in experts_w])
            experts_w = [w2.reshape(w.shape) for w2, w in zip(experts_w, (w_gate[i], w_up[i], w_down[i]))]
        else:
            na, _ = _na(zx, kvc, _na_bias_tables(na_rpb[i], rows), col_q, [])
            experts_w = [w.astype(BF16) for w in experts_w]

        x1, h2, logits_t = _merge(hy, na, zx, x, mod3, norm2_g[i], w_branch_hy[i].astype(BF16),
                                  w_branch_na[i].astype(BF16), w_out[i].astype(BF16), w_router[i].T, col_g, tm=1024)
        rank, rank_t, gate_t = _route(logits_t, cap, grp=math.gcd(bsz, 8))
        y = _experts(rank, h2, *experts_w, cap)
        x = _combine(rank_t, gate_t, y, x1, mod3, final_g, cap, tm=1024)
    return x
```

```python
import functools
import math

import jax
import jax.numpy as jnp
import numpy as np
from jax import lax
from jax.experimental import pallas as pl
from jax.experimental.pallas import tpu as pltpu

F32 = jnp.float32
BF16 = jnp.bfloat16

EPS = 1e-6
GRID_W = 64
HY_ORDER = 2
SHORT_CONV = 3
FILT_BANDS = 8
DECAY_TARGET = 1e-2
FAST_DECAY_PCT = 0.3
SLOW_DECAY_PCT = 1.5
HY_BLOCK = 512
NA_HEADS = 8
NA_HEAD_DIM = 64
NA_WIN_H = 8
NA_WIN_W = 16
NA_ROWS_PER_STEP = 4
NA_SLAB_ROWS = NA_ROWS_PER_STEP + NA_WIN_H
N_EXPERTS = 16
EC_CAPACITY = 2

MASK_VALUE = -1e30
LOG2_E = 1.4426950408889634
LANES = 128
BF16_SUBLANES = 16
VMEM_LIMIT = 56 * 1024 * 1024


def _cparams(*sem):
    return pltpu.CompilerParams(dimension_semantics=sem, vmem_limit_bytes=VMEM_LIMIT)


def _const_spec(shape):
    nd = len(shape)
    return pl.BlockSpec(shape, lambda *_: (0,) * nd, pipeline_mode=pl.Buffered(1))


def _dot(a, b):
    return jnp.dot(a, b, preferred_element_type=F32)


def _dot_nt(a, b):
    return lax.dot_general(a, b, (((1,), (1,)), ((), ())), preferred_element_type=F32)


def _mod_kernel(c_ref, w_ref, b_ref, o_ref):
    c = c_ref[...]
    s = c * jax.nn.sigmoid(c)
    o_ref[...] = _dot(s, w_ref[...]) + b_ref[...]


def _modulation(cc, w_mod, b_mod):
    rows, d = cc.shape
    n = w_mod.shape[1]
    tn = 1536
    return pl.pallas_call(
        _mod_kernel,
        grid=(n // tn,),
        in_specs=[
            pl.BlockSpec((rows, d), lambda j: (0, 0)),
            pl.BlockSpec((d, tn), lambda j: (0, j)),
            pl.BlockSpec((1, tn), lambda j: (0, j)),
        ],
        out_specs=pl.BlockSpec((rows, tn), lambda j: (0, j)),
        out_shape=jax.ShapeDtypeStruct((rows, n), F32),
        compiler_params=_cparams("arbitrary"),
        name="modulation",
    )(cc, w_mod, b_mod.reshape(1, n))


def _rms_mod(x, g, shift, scale):
    ms = jnp.mean(x * x, axis=-1, keepdims=True)
    return (x * lax.rsqrt(ms + EPS) * g) * (1.0 + scale) + shift


def _in_proj_kernel(x_ref, mod_ref, g_ref, w_ref, b_ref, o_ref, *, n_chunk):
    h = _rms_mod(x_ref[0], g_ref[...], mod_ref[0, 0:1, :], mod_ref[0, 1:2, :]).astype(BF16)
    n = w_ref.shape[1]
    for j in range(0, n, n_chunk):
        z = _dot(h, w_ref[:, j:j + n_chunk]) + b_ref[:, j:j + n_chunk]
        o_ref[0, :, j:j + n_chunk] = z.astype(o_ref.dtype)


def _in_proj(x, mod3, mod_row, g, w, b, tm):
    bsz, t, d = x.shape
    n = w.shape[1]
    return pl.pallas_call(
        functools.partial(_in_proj_kernel, n_chunk=min(n, 1024)),
        grid=(bsz, t // tm),
        in_specs=[
            pl.BlockSpec((1, tm, d), lambda i, j: (i, j, 0)),
            pl.BlockSpec((1, 6, d), lambda i, j: (mod_row(i), 0, 0)),
            _const_spec((1, d)),
            _const_spec((d, n)),
            _const_spec((1, n)),
        ],
        out_specs=pl.BlockSpec((1, tm, n), lambda i, j: (i, j, 0)),
        out_shape=jax.ShapeDtypeStruct((bsz, t, n), BF16),
        compiler_params=_cparams("parallel", "parallel"),
        name="in_proj",
    )(x, mod3, g.reshape(1, d), w, b.reshape(1, n))


def _dft_matrix(blk):
    n = 2 * blk
    i = jnp.arange(blk, dtype=jnp.int32)
    ft = (i[:, None] * i[None, :]) % n
    ang = ft.astype(F32) * (2.0 * math.pi / n)
    return jnp.concatenate([jnp.cos(ang), -jnp.sin(ang)], axis=1).astype(BF16)


def _alt_sign(shape):
    row = lax.broadcasted_iota(jnp.int32, shape, 0)
    return (1 - 2 * (row & 1)).astype(F32)


def _filter_kernel(feats_ref, w1_ref, b1_ref, w2_ref, b2_ref, freq_ref, w3p_ref, w3f_ref, decay_ref, skip_ref,
                   cs_ref, spec_ref, nyq_ref, h_ref, k2_ref, ar_ref, ai_ref, *, blk):
    seq2, cb = k2_ref.shape
    seq = seq2 // 2
    nblk = seq2 // blk
    n = 2 * blk

    @pl.when((pl.program_id(0) == 0) & (pl.program_id(1) == 0))
    def _():
        freq = freq_ref[...]
        h1 = jnp.sin(freq * (_dot(feats_ref[...], w1_ref[...]) + b1_ref[...]))
        h_ref[...] = jnp.sin(freq * (_dot(h1, w2_ref[...]) + b2_ref[...]))

    h = h_ref[...]
    row = lax.broadcasted_iota(jnp.int32, (seq2, cb), 0)
    k2 = jnp.where(row < seq, _dot(h, w3f_ref[...]), _dot(h, w3p_ref[...])) * decay_ref[...]
    k2_ref[...] = jnp.where(row == 0, 0.0, k2)
    sign = _alt_sign((blk, cb))
    for d in range(nblk):
        a = k2_ref[d * blk:(d + 1) * blk, :]
        ab = a.astype(BF16)
        ar_ref[d] = _dot(cs_ref[:, 0:blk], ab)
        ai_ref[d] = _dot(cs_ref[:, blk:n], ab)
    lag0 = nblk // 2
    ar_ref[lag0] = ar_ref[lag0] + skip_ref[0]
    k2_ref[seq:seq + 1, :] = k2_ref[seq:seq + 1, :] + skip_ref[0]
    frow = lax.broadcasted_iota(jnp.int32, (blk, cb), 0)
    wgt = jnp.where(frow == 0, 1.0 / n, 2.0 / n)
    for d in range(1, nblk):
        a0 = k2_ref[(d - 1) * blk:(d - 1) * blk + 1, :]
        spec_ref[0, d - 1, 0] = ((ar_ref[d] + sign * (ar_ref[d - 1] - a0)) * wgt).astype(spec_ref.dtype)
        spec_ref[0, d - 1, 1] = ((ai_ref[d] + sign * ai_ref[d - 1]) * wgt).astype(spec_ref.dtype)
        cur = jnp.sum(sign * k2_ref[d * blk:(d + 1) * blk, :], axis=0, keepdims=True)
        prev = jnp.sum(sign * k2_ref[(d - 1) * blk:d * blk, :], axis=0, keepdims=True)
        nyq_ref[0, d - 1] = (cur + prev - a0) * (1.0 / n)


def _filter_spectra(seq, width, w1, b1, w2, b2, w3, freq, skip, cs, cb, blk):
    pos = jnp.abs(jnp.arange(2 * seq, dtype=F32) - seq)
    t = pos / max(seq - 1, 1)
    omega = 2.0 * math.pi * pos / seq
    bands = jnp.linspace(1e-4, FILT_BANDS - 1, FILT_BANDS, dtype=F32)
    ang = omega[:, None] * bands[None, :]
    feats = jnp.concatenate([t[:, None], jnp.cos(ang), -jnp.sin(ang)], axis=-1)
    emb, hid = w1.shape
    emb_pad = -(-emb // 8) * 8
    feats = jnp.pad(feats, ((0, 0), (0, emb_pad - emb)))
    w1 = jnp.pad(w1, ((0, emb_pad - emb), (0, 0)))
    max_decay = math.log(DECAY_TARGET) / FAST_DECAY_PCT
    min_decay = math.log(DECAY_TARGET) / SLOW_DECAY_PCT
    deltas = jnp.linspace(min_decay, max_decay, width, dtype=F32)
    decay = jnp.exp(-t[:, None] * jnp.abs(deltas)[None, :])
    ncb = width // cb
    nblk = 2 * seq // blk
    return pl.pallas_call(
        functools.partial(_filter_kernel, blk=blk),
        grid=(HY_ORDER, ncb),
        in_specs=[
            _const_spec((2 * seq, emb_pad)),
            _const_spec((emb_pad, hid)),
            _const_spec((1, hid)),
            _const_spec((hid, hid)),
            _const_spec((1, hid)),
            _const_spec((1, hid)),
            pl.BlockSpec((hid, cb), lambda o, c: (0, o * ncb + c)),
            pl.BlockSpec((hid, cb), lambda o, c: (0, (HY_ORDER + o) * ncb + c)),
            pl.BlockSpec((2 * seq, cb), lambda o, c: (0, c)),
            pl.BlockSpec((1, 1, cb), lambda o, c: (o, 0, c)),
            _const_spec((blk, 2 * blk)),
        ],
        out_specs=[
            pl.BlockSpec((1, nblk - 1, 2, blk, cb), lambda o, c: (o, 0, 0, 0, c)),
            pl.BlockSpec((1, nblk - 1, 1, cb), lambda o, c: (o, 0, 0, c)),
        ],
        out_shape=[
            jax.ShapeDtypeStruct((HY_ORDER, nblk - 1, 2, blk, width), BF16),
            jax.ShapeDtypeStruct((HY_ORDER, nblk - 1, 1, width), F32),
        ],
        scratch_shapes=[
            pltpu.VMEM((2 * seq, hid), F32),
            pltpu.VMEM((2 * seq, cb), F32),
            pltpu.VMEM((nblk, blk, cb), F32),
            pltpu.VMEM((nblk, blk, cb), F32),
        ],
        compiler_params=_cparams("arbitrary", "arbitrary"),
        name="hyena_filters",
    )(feats, w1, b1.reshape(1, hid), w2, b2.reshape(1, hid), freq.reshape(1, hid), w3, w3, decay,
      skip.reshape(HY_ORDER, 1, width), cs)


def _hyena_kernel(z_ref, sw_ref, sb_ref, spec_ref, nyq_ref, cs_ref, o_ref,
                  u_ref, g_ref, ub_ref, re_ref, im_ref, yri_ref, *, blk, cb):
    seq, width = o_ref.shape[1], o_ref.shape[2]
    nb = seq // blk
    sign = _alt_sign((blk, cb))

    def short_conv(dst_ref, g, cols):
        zc = slice(g * width + cols.start, g * width + cols.stop)
        z = z_ref[0, :, zc].astype(F32)
        w0, w1, w2 = sw_ref[g, 0:1, cols], sw_ref[g, 1:2, cols], sw_ref[g, 2:3, cols]
        dst_ref[...] = sb_ref[g, :, cols] + pltpu.roll(z, 1, 0) * w0 + z * w1 + pltpu.roll(z, seq - 1, 0) * w2
        dst_ref[0:1, :] = dst_ref[0:1, :] - z_ref[0, seq - 1:seq, zc].astype(F32) * w0
        dst_ref[seq - 1:seq, :] = dst_ref[seq - 1:seq, :] - z_ref[0, 0:1, zc].astype(F32) * w2

    def long_conv(o, g, cols):
        ub_ref[...] = u_ref[...].astype(BF16)
        nyq_in = []
        for j in range(nb):
            rows = slice(j * blk, (j + 1) * blk)
            re_ref[j] = _dot(cs_ref[:, 0:blk], ub_ref[rows, :]).astype(BF16)
            im_ref[j] = _dot(cs_ref[:, blk:2 * blk], ub_ref[rows, :]).astype(BF16)
            nyq_in.append(jnp.sum(sign * u_ref[rows, :], axis=0, keepdims=True))
        short_conv(g_ref, g, cols)
        for i in range(nb):
            yr = yi = nyq = None
            for j in range(nb):
                d = i - j + nb - 1
                gr, gi = spec_ref[o, d, 0, :, cols], spec_ref[o, d, 1, :, cols]
                re, im = re_ref[j], im_ref[j]
                tr = re * gr - im * gi
                ti = re * gi + im * gr
                tn = nyq_in[j] * nyq_ref[o, d, :, cols]
                yr, yi, nyq = (tr, ti, tn) if yr is None else (yr + tr, yi + ti, nyq + tn)
            yri_ref[0:blk, :] = yr
            yri_ref[blk:2 * blk, :] = yi
            rows = slice(i * blk, (i + 1) * blk)
            y = _dot(cs_ref[...], yri_ref[...]) + sign * nyq
            g_ref[rows, :] = g_ref[rows, :] * y
        u_ref[...] = g_ref[...]

    for c in range(0, width, cb):
        cols = slice(c, c + cb)
        short_conv(u_ref, 0, cols)
        long_conv(0, 1, cols)
        long_conv(1, 2, cols)
        o_ref[0, :, cols] = u_ref[...].astype(o_ref.dtype)


def _hyena(zx, short_w, short_b, spec, nyq, cs, width, cb, blk):
    bsz, seq, _ = zx.shape
    nb = seq // blk
    sw = short_w.reshape(SHORT_CONV, 3, width).transpose(1, 0, 2)
    sb = short_b.reshape(3, 1, width)
    return pl.pallas_call(
        functools.partial(_hyena_kernel, blk=blk, cb=cb),
        grid=(bsz,),
        in_specs=[
            pl.BlockSpec((1, seq, 3 * width), lambda i: (i, 0, 0)),
            _const_spec(sw.shape),
            _const_spec(sb.shape),
            _const_spec(spec.shape),
            _const_spec(nyq.shape),
            _const_spec((blk, 2 * blk)),
        ],
        out_specs=pl.BlockSpec((1, seq, width), lambda i: (i, 0, 0)),
        out_shape=jax.ShapeDtypeStruct((bsz, seq, width), BF16),
        scratch_shapes=[
            pltpu.VMEM((seq, cb), F32),
            pltpu.VMEM((seq, cb), F32),
            pltpu.VMEM((seq, cb), BF16),
            pltpu.VMEM((nb, blk, cb), BF16),
            pltpu.VMEM((nb, blk, cb), BF16),
            pltpu.VMEM((2 * blk, cb), BF16),
        ],
        compiler_params=_cparams("parallel"),
        name="hyena",
    )(zx, sw, sb, spec, nyq, cs)


def _na_col_bias_kernel(rpb_ref, sel_ref, mask_ref, rowmask_ref, o_ref):
    acc = jnp.zeros(o_ref.shape, F32) + mask_ref[...] + rowmask_ref[...]
    for ci in range(sel_ref.shape[0]):
        acc = acc + (rpb_ref[:, ci:ci + 1] * LOG2_E) * sel_ref[ci:ci + 1, :]
    o_ref[...] = acc


def _na_bias_tables(rpb, rows):
    rq, sl, half = NA_ROWS_PER_STEP, NA_SLAB_ROWS, NA_WIN_H // 2
    n_ri, n_ci = 2 * NA_WIN_H - 1, 2 * NA_WIN_W - 1
    qc = np.arange(GRID_W)
    kc = np.arange(GRID_W)
    ws = np.clip(qc - NA_WIN_W // 2, 0, GRID_W - NA_WIN_W)
    in_win = (kc[None, :] >= ws[:, None]) & (kc[None, :] < ws[:, None] + NA_WIN_W)
    ci = np.clip(kc[None, :] - qc[:, None] + NA_WIN_W - 1, 0, n_ci - 1)
    sel = ((ci[None] == np.arange(n_ci)[:, None, None]) & in_win[None]).astype(np.float32)
    sel = sel.reshape(n_ci, GRID_W * GRID_W)
    mask = np.where(in_win, 0.0, MASK_VALUE).astype(np.float32).reshape(1, GRID_W * GRID_W)
    rowmask = np.tile(np.where(np.arange(n_ri + 1) == n_ri, MASK_VALUE, 0.0).astype(np.float32), NA_HEADS)
    rpb_x = jnp.pad(rpb, ((0, 0), (0, 1), (0, 0))).reshape(NA_HEADS * (n_ri + 1), n_ci)
    col = pl.pallas_call(
        _na_col_bias_kernel,
        out_shape=jax.ShapeDtypeStruct((NA_HEADS * (n_ri + 1), GRID_W * GRID_W), F32),
        name="na_col_bias",
    )(rpb_x, jnp.asarray(sel), jnp.asarray(mask), jnp.asarray(rowmask.reshape(-1, 1)))
    col = col.reshape(NA_HEADS, n_ri + 1, GRID_W, GRID_W)

    n_steps = rows // rq
    steps = (0, 1, n_steps - 1)
    ri = np.full((len(steps), rq, sl), n_ri, np.int32)
    for c, step in enumerate(steps):
        start = int(np.clip(step * rq - half, 0, rows - sl))
        for i in range(rq):
            r = step * rq + i
            rs = int(np.clip(r - half, 0, rows - NA_WIN_H))
            for j in range(NA_WIN_H):
                ri[c, i, rs - start + j] = rs + j - r + NA_WIN_H - 1
    tab = jnp.take(col, jnp.asarray(ri.reshape(-1)), axis=1)
    tab = tab.reshape(NA_HEADS, len(steps), rq, sl, GRID_W, GRID_W).transpose(1, 0, 2, 4, 3, 5)
    return tab.reshape(len(steps), NA_HEADS, rq * GRID_W, sl * GRID_W)


def _na_kernel(q_ref, k_ref, v_ref, kvc_ref, bias_ref, *rest, rows):
    n_cast = (len(rest) - 1) // 2
    o_ref = rest[n_cast]
    for src, dst in zip(rest[:n_cast], rest[n_cast + 1:]):
        dst[...] = src[...].astype(dst.dtype)
    rq, sl = NA_ROWS_PER_STEP, NA_SLAB_ROWS
    width = NA_HEADS * NA_HEAD_DIM
    step = pl.program_id(1)
    start = jnp.clip(step * rq - NA_WIN_H // 2, 0, rows - sl)
    start = pl.multiple_of(start * GRID_W, GRID_W)
    nh = LANES // NA_HEAD_DIM
    nq = rq * GRID_W
    lane = lax.broadcasted_iota(jnp.int32, (nq, LANES), 1)
    owns = [(lane >= s * NA_HEAD_DIM) & (lane < (s + 1) * NA_HEAD_DIM) for s in range(nh)]

    for hp in range(width // LANES):
        cols = slice(hp * LANES, (hp + 1) * LANES)
        q2 = q_ref[0, :, cols]
        k2 = k_ref[0, pl.ds(start, sl * GRID_W), cols]
        v2 = v_ref[0, pl.ds(start, sl * GRID_W), cols]
        kc2 = kvc_ref[0, :, cols]
        vc2 = kvc_ref[0, :, width + hp * LANES:width + (hp + 1) * LANES]
        qm = jnp.concatenate([jnp.where(own, q2, jnp.zeros_like(q2)) for own in owns], axis=0)
        s_loc = _dot_nt(qm, k2) + bias_ref[0, hp * nh:(hp + 1) * nh].reshape(nh * nq, sl * GRID_W)
        s_ctx = _dot_nt(qm, kc2)
        m = jnp.maximum(jnp.max(s_loc, axis=-1, keepdims=True), jnp.max(s_ctx, axis=-1, keepdims=True))
        p_loc = jnp.exp2(s_loc - m)
        p_ctx = jnp.exp2(s_ctx - m)
        den = jnp.sum(p_loc, axis=-1, keepdims=True) + jnp.sum(p_ctx, axis=-1, keepdims=True)
        o = (_dot(p_loc.astype(BF16), v2) + _dot(p_ctx.astype(BF16), vc2)) / den
        o_ref[0, :, cols] = sum(jnp.where(owns[s], o[s * nq:(s + 1) * nq], 0.0) for s in range(nh)).astype(o_ref.dtype)


def _na(zx, kvc, bias, col_q, cast_2d):
    bsz, seq, _ = zx.shape
    rows = seq // GRID_W
    rq = NA_ROWS_PER_STEP
    assert rows % rq == 0 and rows >= NA_SLAB_ROWS and rows // rq >= 3
    n_steps = rows // rq
    width = NA_HEADS * NA_HEAD_DIM
    qb = col_q // width
    total = bsz * n_steps
    for w in cast_2d:
        assert w.shape[0] % (total * BF16_SUBLANES) == 0, (w.shape, total)

    def cfg(s):
        return jnp.minimum(s, 1) + jnp.maximum(s - (n_steps - 2), 0)

    def cast_spec(w):
        return pl.BlockSpec((w.shape[0] // total, w.shape[1]), lambda i, s: (i * n_steps + s, 0))

    outs = pl.pallas_call(
        functools.partial(_na_kernel, rows=rows),
        grid=(bsz, n_steps),
        in_specs=[
            pl.BlockSpec((1, rq * GRID_W, width), lambda i, s: (i, s, qb)),
            pl.BlockSpec((1, seq, width), lambda i, s: (i, 0, qb + 1)),
            pl.BlockSpec((1, seq, width), lambda i, s: (i, 0, qb + 2)),
            pl.BlockSpec((1,) + kvc.shape[1:], lambda i, s: (i, 0, 0)),
            pl.BlockSpec((1,) + bias.shape[1:], lambda i, s: (cfg(s), 0, 0, 0)),
        ] + [cast_spec(w) for w in cast_2d],
        out_specs=[pl.BlockSpec((1, rq * GRID_W, width), lambda i, s: (i, s, 0))] + [cast_spec(w) for w in cast_2d],
        out_shape=[jax.ShapeDtypeStruct((bsz, seq, width), BF16)]
        + [jax.ShapeDtypeStruct(w.shape, BF16) for w in cast_2d],
        compiler_params=_cparams("parallel", "arbitrary"),
        name="na",
    )(zx, zx, zx, kvc, bias, *cast_2d)
    return outs[0], outs[1:]


def _cast_in_na(weights, total_steps):
    max_block_bytes = 1 << 20
    for w in weights:
        rows = w.shape[0] * w.shape[1]
        if rows % (total_steps * BF16_SUBLANES) or rows // total_steps * w.shape[2] * 4 > max_block_bytes:
            return False
    return True


def _merge_kernel(hy_ref, na_ref, ghy_ref, gna_ref, x_ref, mod_ref, g2_ref, wbh_ref, wbn_ref, wo_ref, wr_ref,
                  x1_ref, h2_ref, lg_ref):
    a = _dot(hy_ref[0], wbh_ref[...])
    b = _dot(na_ref[0], wbn_ref[...])
    m = jax.nn.sigmoid(ghy_ref[0].astype(F32)) * a + jax.nn.sigmoid(gna_ref[0].astype(F32)) * b
    mix = _dot(m.astype(BF16), wo_ref[...])
    x1 = x_ref[0] + mod_ref[0, 2:3, :] * mix
    x1_ref[0] = x1
    h2 = _rms_mod(x1, g2_ref[...], mod_ref[0, 3:4, :], mod_ref[0, 4:5, :])
    h2_ref[0] = h2.astype(h2_ref.dtype)
    lg_ref[0] = _dot_nt(wr_ref[...], h2)


def _merge(hy, na, zx, x, mod3, g2, wbh, wbn, wo, wr_t, col_g, tm):
    bsz, seq, d = x.shape
    gb = col_g // d
    ne = wr_t.shape[0]
    return pl.pallas_call(
        _merge_kernel,
        grid=(bsz, seq // tm),
        in_specs=[
            pl.BlockSpec((1, tm, hy.shape[2]), lambda i, j: (i, j, 0)),
            pl.BlockSpec((1, tm, na.shape[2]), lambda i, j: (i, j, 0)),
            pl.BlockSpec((1, tm, d), lambda i, j: (i, j, gb)),
            pl.BlockSpec((1, tm, d), lambda i, j: (i, j, gb + 1)),
            pl.BlockSpec((1, tm, d), lambda i, j: (i, j, 0)),
            pl.BlockSpec((1, 6, d), lambda i, j: (i, 0, 0)),
            _const_spec((1, d)),
            _const_spec(wbh.shape),
            _const_spec(wbn.shape),
            _const_spec(wo.shape),
            _const_spec(wr_t.shape),
        ],
        out_specs=[
            pl.BlockSpec((1, tm, d), lambda i, j: (i, j, 0)),
            pl.BlockSpec((1, tm, d), lambda i, j: (i, j, 0)),
            pl.BlockSpec((1, ne, tm), lambda i, j: (i, 0, j)),
        ],
        out_shape=[
            jax.ShapeDtypeStruct((bsz, seq, d), F32),
            jax.ShapeDtypeStruct((bsz, seq, d), BF16),
            jax.ShapeDtypeStruct((bsz, ne, seq), F32),
        ],
        compiler_params=_cparams("parallel", "parallel"),
        name="merge",
    )(hy, na, zx, zx, x, mod3, g2.reshape(1, d), wbh, wbn, wo, wr_t)


def _route_kernel(lg_ref, tri_ref, rank_ref, rank_t_ref, gate_t_ref, *, cap):
    grp, ne, t = lg_ref.shape
    lg = lg_ref[...]
    e = jnp.exp(lg - jnp.max(lg, axis=1, keepdims=True))
    aff = (e / jnp.sum(e, axis=1, keepdims=True)).reshape(grp * ne, t)

    def bit_step(i, bits):
        cand = bits | (jnp.int32(1) << (30 - i))
        keep = jnp.sum((aff >= pltpu.bitcast(cand, F32)).astype(jnp.int32), axis=1, keepdims=True) >= cap
        return jnp.where(keep, cand, bits)

    thr = pltpu.bitcast(lax.fori_loop(0, 31, bit_step, jnp.zeros((grp * ne, 1), jnp.int32)), F32)
    above = aff > thr
    tie = aff == thr
    need = cap - jnp.sum(above.astype(jnp.int32), axis=1, keepdims=True)
    tri = tri_ref[...]
    tie_before = _dot(tie.astype(BF16), tri)
    sel = above | (tie & (tie_before < need.astype(F32)))
    sel_before = _dot(sel.astype(BF16), tri)
    rank = jnp.where(sel, sel_before, -1.0)
    gate = jnp.where(sel, aff, 0.0)
    rank_ref[...] = rank.astype(jnp.int32).reshape(grp, ne, t)
    pad_r = jnp.full((LANES - ne, t), -1.0, F32)
    pad_g = jnp.zeros((LANES - ne, t), F32)
    for g in range(grp):
        rows = slice(g * ne, (g + 1) * ne)
        rank_t_ref[g] = jnp.concatenate([rank[rows], pad_r], axis=0).T.astype(jnp.int32)
        gate_t_ref[g] = jnp.concatenate([gate[rows], pad_g], axis=0).T


def _route(logits_t, cap, grp):
    bsz, ne, t = logits_t.shape
    i = jnp.arange(t, dtype=jnp.int32)
    tri = (i[:, None] < i[None, :]).astype(BF16)
    return pl.pallas_call(
        functools.partial(_route_kernel, cap=cap),
        grid=(bsz // grp,),
        in_specs=[
            pl.BlockSpec((grp, ne, t), lambda b: (b, 0, 0)),
            _const_spec((t, t)),
        ],
        out_specs=[
            pl.BlockSpec((grp, ne, t), lambda b: (b, 0, 0)),
            pl.BlockSpec((grp, t, LANES), lambda b: (b, 0, 0)),
            pl.BlockSpec((grp, t, LANES), lambda b: (b, 0, 0)),
        ],
        out_shape=[
            jax.ShapeDtypeStruct((bsz, ne, t), jnp.int32),
            jax.ShapeDtypeStruct((bsz, t, LANES), jnp.int32),
            jax.ShapeDtypeStruct((bsz, t, LANES), F32),
        ],
        compiler_params=_cparams("parallel"),
        name="route",
    )(logits_t, tri)


def _expert_kernel(rank_ref, h_ref, wg_ref, wu_ref, wd_ref, y_ref, *, cap):
    e = pl.program_id(0)
    grp, t, _ = h_ref.shape
    slot = lax.broadcasted_iota(jnp.int32, (cap, t), 0)
    xe = []
    for b in range(grp):
        rank = rank_ref[b, pl.ds(e, 1), :]
        onehot = jnp.where(rank == slot, 1.0, 0.0).astype(BF16)
        xe.append(_dot(onehot, h_ref[b]).astype(BF16))
    xe = jnp.concatenate(xe, axis=0)
    g = _dot(xe, wg_ref[0])
    u = _dot(xe, wu_ref[0])
    act = (g * jax.nn.sigmoid(g) * u).astype(BF16)
    y = _dot(act, wd_ref[0])
    for b in range(grp):
        y_ref[b, 0] = y[b * cap:(b + 1) * cap].astype(y_ref.dtype)


def _experts(rank, h2, wg, wu, wd, cap, grp):
    bsz, t, d = h2.shape
    ne, _, f = wg.shape

    def wspec(shape):
        return pl.BlockSpec((1,) + shape, lambda e, b: (e, 0, 0), pipeline_mode=pl.Buffered(1))

    return pl.pallas_call(
        functools.partial(_expert_kernel, cap=cap),
        grid=(ne, bsz // grp),
        in_specs=[
            pl.BlockSpec((grp, ne, t), lambda e, b: (b, 0, 0)),
            pl.BlockSpec((grp, t, d), lambda e, b: (b, 0, 0)),
            wspec((d, f)), wspec((d, f)), wspec((f, d)),
        ],
        out_specs=pl.BlockSpec((grp, 1, cap, d), lambda e, b: (b, e, 0, 0)),
        out_shape=jax.ShapeDtypeStruct((bsz, ne, cap, d), BF16),
        compiler_params=_cparams("arbitrary", "arbitrary"),
        name="experts",
    )(rank, h2, wg, wu, wd)


def _combine_kernel(rank_t_ref, gate_t_ref, y_ref, x1_ref, mod_ref, gf_ref, o_ref, *, cap):
    ne = y_ref.shape[1]
    tm = x1_ref.shape[1]
    slot = lax.broadcasted_iota(jnp.int32, (tm, cap), 1)
    acc = jnp.zeros(x1_ref.shape[1:], F32)
    for e in range(ne):
        r = rank_t_ref[0, :, e:e + 1]
        g = gate_t_ref[0, :, e:e + 1]
        scat = jnp.where(r == slot, g, 0.0).astype(BF16)
        acc = acc + _dot(scat, y_ref[0, e])
    x2 = x1_ref[0] + mod_ref[0, 5:6, :] * acc
    ms = jnp.mean(x2 * x2, axis=-1, keepdims=True)
    o_ref[0] = x2 * lax.rsqrt(ms + EPS) * gf_ref[...]


def _combine(rank_t, gate_t, y, x1, mod3, final_g, cap, tm):
    bsz, t, d = x1.shape
    ne = y.shape[1]
    return pl.pallas_call(
        functools.partial(_combine_kernel, cap=cap),
        grid=(bsz, t // tm),
        in_specs=[
            pl.BlockSpec((1, tm, LANES), lambda i, j: (i, j, 0)),
            pl.BlockSpec((1, tm, LANES), lambda i, j: (i, j, 0)),
            pl.BlockSpec((1, ne, cap, d), lambda i, j: (i, 0, 0, 0)),
            pl.BlockSpec((1, tm, d), lambda i, j: (i, j, 0)),
            pl.BlockSpec((1, 6, d), lambda i, j: (i, 0, 0)),
            _const_spec((1, d)),
        ],
        out_specs=pl.BlockSpec((1, tm, d), lambda i, j: (i, j, 0)),
        out_shape=jax.ShapeDtypeStruct((bsz, t, d), F32),
        compiler_params=_cparams("parallel", "arbitrary"),
        name="combine",
    )(rank_t, gate_t, y, x1, mod3, final_g.reshape(1, d))


def kernel(x, c, ctx, c_ctx, w_mod, b_mod, norm1_g, norm2_g, w_in, b_in, hy_short_w, hy_short_b, hy_skip, filt_w1, filt_b1, filt_w2, filt_b2, filt_w3, filt_freq, na_rpb, w_branch_hy, w_branch_na, w_out, w_router, w_gate, w_up, w_down, final_g):
    depth = w_mod.shape[0]
    bsz, seq, d = x.shape
    hy_width = w_branch_hy.shape[1]
    na_width = w_branch_na.shape[1]
    col_q = 3 * hy_width
    col_k = col_q + na_width
    col_g = col_q + 3 * na_width
    cap = EC_CAPACITY * seq // N_EXPERTS
    rows = seq // GRID_W
    mod_rows = -(-(bsz + 1) // 8) * 8

    hy_blk = HY_BLOCK
    cs = _dft_matrix(hy_blk)
    for i in range(depth):
        assert i == depth - 1, "only the final layer's data flow (context feeds keys/values only) is implemented"
        cc = jnp.zeros((mod_rows, d), F32).at[:bsz].set(c).at[bsz].set(c_ctx)
        mod3 = _modulation(cc, w_mod[i], b_mod[i]).reshape(mod_rows, 6, d)

        qscale = jnp.ones((w_in.shape[2],), F32).at[col_q:col_k].set(NA_HEAD_DIM ** -0.5 * LOG2_E)
        w_in_s = (w_in[i] * qscale).astype(BF16)
        b_in_s = b_in[i] * qscale
        zx = _in_proj(x, mod3, lambda b: b, norm1_g[i], w_in_s, b_in_s, tm=1024)
        kvc = _in_proj(ctx.reshape(1, -1, d), mod3, lambda b: bsz, norm1_g[i], w_in_s[:, col_k:col_g],
                       b_in_s[col_k:col_g], tm=math.gcd(bsz * ctx.shape[1], 1024)).reshape(bsz, ctx.shape[1], -1)

        spec, nyq = _filter_spectra(seq, hy_width, filt_w1[i], filt_b1[i], filt_w2[i], filt_b2[i], filt_w3[i],
                                    filt_freq[i], hy_skip[i], cs, cb=256, blk=hy_blk)
        hy = _hyena(zx, hy_short_w[i], hy_short_b[i], spec, nyq, cs, hy_width, cb=256, blk=hy_blk)

        experts_w = (w_gate[i], w_up[i], w_down[i])
        if _cast_in_na(experts_w, bsz * (rows // NA_ROWS_PER_STEP)):
            na, experts_w = _na(zx, kvc, _na_bias_tables(na_rpb[i], rows), col_q,
                                [w.reshape(-1, w.shape[2]) for w in experts_w])
            experts_w = [w2.reshape(w.shape) for w2, w in zip(experts_w, (w_gate[i], w_up[i], w_down[i]))]
        else:
            na, _ = _na(zx, kvc, _na_bias_tables(na_rpb[i], rows), col_q, [])
            experts_w = [w.astype(BF16) for w in experts_w]

        x1, h2, logits_t = _merge(hy, na, zx, x, mod3, norm2_g[i], w_branch_hy[i].astype(BF16),
                                  w_branch_na[i].astype(BF16), w_out[i].astype(BF16), w_router[i].T, col_g, tm=1024)
        rank, rank_t, gate_t = _route(logits_t, cap, grp=math.gcd(bsz, 8))
        y = _experts(rank, h2, *experts_w, cap, grp=math.gcd(bsz, 2))
        x = _combine(rank_t, gate_t, y, x1, mod3, final_g, cap, tm=1024)
    return x
```

```python
import functools
import math

import jax
import jax.numpy as jnp
import numpy as np
from jax import lax
from jax.experimental import pallas as pl
from jax.experimental.pallas import tpu as pltpu

F32 = jnp.float32
BF16 = jnp.bfloat16

EPS = 1e-6
GRID_W = 64
HY_ORDER = 2
SHORT_CONV = 3
FILT_BANDS = 8
DECAY_TARGET = 1e-2
FAST_DECAY_PCT = 0.3
SLOW_DECAY_PCT = 1.5
HY_BLOCK = 512
NA_HEADS = 8
NA_HEAD_DIM = 64
NA_WIN_H = 8
NA_WIN_W = 16
NA_ROWS_PER_STEP = 4
NA_SLAB_ROWS = NA_ROWS_PER_STEP + NA_WIN_H
N_EXPERTS = 16
EC_CAPACITY = 2

MASK_VALUE = -1e30
LOG2_E = 1.4426950408889634
LANES = 128
BF16_SUBLANES = 16
VMEM_LIMIT = 56 * 1024 * 1024
TOKEN_TILE = 1024
PROJ_COL_CHUNK = 1024
MOD_COL_TILE = 1536
HY_CHANNEL_BLOCK = 256


def _cparams(*sem):
    return pltpu.CompilerParams(dimension_semantics=sem, vmem_limit_bytes=VMEM_LIMIT)


def _const_spec(shape):
    nd = len(shape)
    return pl.BlockSpec(shape, lambda *_: (0,) * nd, pipeline_mode=pl.Buffered(1))


def _dot(a, b):
    return jnp.dot(a, b, preferred_element_type=F32)


def _dot_nt(a, b):
    return lax.dot_general(a, b, (((1,), (1,)), ((), ())), preferred_element_type=F32)


def _mod_kernel(c_ref, w_ref, b_ref, o_ref):
    c = c_ref[...]
    s = c * jax.nn.sigmoid(c)
    o_ref[...] = _dot(s, w_ref[...]) + b_ref[...]


def _modulation(cc, w_mod, b_mod):
    rows, d = cc.shape
    n = w_mod.shape[1]
    tn = MOD_COL_TILE
    return pl.pallas_call(
        _mod_kernel,
        grid=(n // tn,),
        in_specs=[
            pl.BlockSpec((rows, d), lambda j: (0, 0)),
            pl.BlockSpec((d, tn), lambda j: (0, j)),
            pl.BlockSpec((1, tn), lambda j: (0, j)),
        ],
        out_specs=pl.BlockSpec((rows, tn), lambda j: (0, j)),
        out_shape=jax.ShapeDtypeStruct((rows, n), F32),
        compiler_params=_cparams("arbitrary"),
        name="modulation",
    )(cc, w_mod, b_mod.reshape(1, n))


def _rms_mod(x, g, shift, scale):
    ms = jnp.mean(x * x, axis=-1, keepdims=True)
    return (x * lax.rsqrt(ms + EPS) * g) * (1.0 + scale) + shift


def _in_proj_kernel(x_ref, mod_ref, g_ref, w_ref, b_ref, o_ref, *, n_chunk):
    h = _rms_mod(x_ref[0], g_ref[...], mod_ref[0, 0:1, :], mod_ref[0, 1:2, :]).astype(BF16)
    n = w_ref.shape[1]
    for j in range(0, n, n_chunk):
        z = _dot(h, w_ref[:, j:j + n_chunk]) + b_ref[:, j:j + n_chunk]
        o_ref[0, :, j:j + n_chunk] = z.astype(o_ref.dtype)


def _in_proj(x, mod3, mod_row, g, w, b, tm):
    bsz, t, d = x.shape
    n = w.shape[1]
    return pl.pallas_call(
        functools.partial(_in_proj_kernel, n_chunk=min(n, PROJ_COL_CHUNK)),
        grid=(bsz, t // tm),
        in_specs=[
            pl.BlockSpec((1, tm, d), lambda i, j: (i, j, 0)),
            pl.BlockSpec((1, 6, d), lambda i, j: (mod_row(i), 0, 0)),
            _const_spec((1, d)),
            _const_spec((d, n)),
            _const_spec((1, n)),
        ],
        out_specs=pl.BlockSpec((1, tm, n), lambda i, j: (i, j, 0)),
        out_shape=jax.ShapeDtypeStruct((bsz, t, n), BF16),
        compiler_params=_cparams("parallel", "parallel"),
        name="in_proj",
    )(x, mod3, g.reshape(1, d), w, b.reshape(1, n))


def _dft_matrix(blk):
    n = 2 * blk
    i = jnp.arange(blk, dtype=jnp.int32)
    ft = (i[:, None] * i[None, :]) % n
    ang = ft.astype(F32) * (2.0 * math.pi / n)
    return jnp.concatenate([jnp.cos(ang), -jnp.sin(ang)], axis=1).astype(BF16)


def _alt_sign(shape):
    row = lax.broadcasted_iota(jnp.int32, shape, 0)
    return (1 - 2 * (row & 1)).astype(F32)


def _filter_kernel(feats_ref, w1_ref, b1_ref, w2_ref, b2_ref, freq_ref, w3p_ref, w3f_ref, decay_ref, skip_ref,
                   cs_ref, spec_ref, nyq_ref, h_ref, k2_ref, ar_ref, ai_ref, *, blk):
    seq2, cb = k2_ref.shape
    seq = seq2 // 2
    nblk = seq2 // blk
    n = 2 * blk

    @pl.when((pl.program_id(0) == 0) & (pl.program_id(1) == 0))
    def _():
        freq = freq_ref[...]
        h1 = jnp.sin(freq * (_dot(feats_ref[...], w1_ref[...]) + b1_ref[...]))
        h_ref[...] = jnp.sin(freq * (_dot(h1, w2_ref[...]) + b2_ref[...]))

    h = h_ref[...]
    row = lax.broadcasted_iota(jnp.int32, (seq2, cb), 0)
    k2 = jnp.where(row < seq, _dot(h, w3f_ref[...]), _dot(h, w3p_ref[...])) * decay_ref[...]
    k2_ref[...] = jnp.where(row == 0, 0.0, k2)
    sign = _alt_sign((blk, cb))
    for d in range(nblk):
        a = k2_ref[d * blk:(d + 1) * blk, :]
        ab = a.astype(BF16)
        ar_ref[d] = _dot(cs_ref[:, 0:blk], ab)
        ai_ref[d] = _dot(cs_ref[:, blk:n], ab)
    lag0 = nblk // 2
    ar_ref[lag0] = ar_ref[lag0] + skip_ref[0]
    k2_ref[seq:seq + 1, :] = k2_ref[seq:seq + 1, :] + skip_ref[0]
    frow = lax.broadcasted_iota(jnp.int32, (blk, cb), 0)
    wgt = jnp.where(frow == 0, 1.0 / n, 2.0 / n)
    for d in range(1, nblk):
        a0 = k2_ref[(d - 1) * blk:(d - 1) * blk + 1, :]
        spec_ref[0, d - 1, 0] = ((ar_ref[d] + sign * (ar_ref[d - 1] - a0)) * wgt).astype(spec_ref.dtype)
        spec_ref[0, d - 1, 1] = ((ai_ref[d] + sign * ai_ref[d - 1]) * wgt).astype(spec_ref.dtype)
        cur = jnp.sum(sign * k2_ref[d * blk:(d + 1) * blk, :], axis=0, keepdims=True)
        prev = jnp.sum(sign * k2_ref[(d - 1) * blk:d * blk, :], axis=0, keepdims=True)
        nyq_ref[0, d - 1] = (cur + prev - a0) * (1.0 / n)


def _filter_spectra(seq, width, w1, b1, w2, b2, w3, freq, skip, cs, cb, blk):
    pos = jnp.abs(jnp.arange(2 * seq, dtype=F32) - seq)
    t = pos / max(seq - 1, 1)
    omega = 2.0 * math.pi * pos / seq
    bands = jnp.linspace(1e-4, FILT_BANDS - 1, FILT_BANDS, dtype=F32)
    ang = omega[:, None] * bands[None, :]
    feats = jnp.concatenate([t[:, None], jnp.cos(ang), -jnp.sin(ang)], axis=-1)
    emb, hid = w1.shape
    emb_pad = -(-emb // 8) * 8
    feats = jnp.pad(feats, ((0, 0), (0, emb_pad - emb)))
    w1 = jnp.pad(w1, ((0, emb_pad - emb), (0, 0)))
    max_decay = math.log(DECAY_TARGET) / FAST_DECAY_PCT
    min_decay = math.log(DECAY_TARGET) / SLOW_DECAY_PCT
    deltas = jnp.linspace(min_decay, max_decay, width, dtype=F32)
    decay = jnp.exp(-t[:, None] * jnp.abs(deltas)[None, :])
    ncb = width // cb
    nblk = 2 * seq // blk
    return pl.pallas_call(
        functools.partial(_filter_kernel, blk=blk),
        grid=(HY_ORDER, ncb),
        in_specs=[
            _const_spec((2 * seq, emb_pad)),
            _const_spec((emb_pad, hid)),
            _const_spec((1, hid)),
            _const_spec((hid, hid)),
            _const_spec((1, hid)),
            _const_spec((1, hid)),
            pl.BlockSpec((hid, cb), lambda o, c: (0, o * ncb + c)),
            pl.BlockSpec((hid, cb), lambda o, c: (0, (HY_ORDER + o) * ncb + c)),
            pl.BlockSpec((2 * seq, cb), lambda o, c: (0, c)),
            pl.BlockSpec((1, 1, cb), lambda o, c: (o, 0, c)),
            _const_spec((blk, 2 * blk)),
        ],
        out_specs=[
            pl.BlockSpec((1, nblk - 1, 2, blk, cb), lambda o, c: (o, 0, 0, 0, c)),
            pl.BlockSpec((1, nblk - 1, 1, cb), lambda o, c: (o, 0, 0, c)),
        ],
        out_shape=[
            jax.ShapeDtypeStruct((HY_ORDER, nblk - 1, 2, blk, width), BF16),
            jax.ShapeDtypeStruct((HY_ORDER, nblk - 1, 1, width), F32),
        ],
        scratch_shapes=[
            pltpu.VMEM((2 * seq, hid), F32),
            pltpu.VMEM((2 * seq, cb), F32),
            pltpu.VMEM((nblk, blk, cb), F32),
            pltpu.VMEM((nblk, blk, cb), F32),
        ],
        compiler_params=_cparams("arbitrary", "arbitrary"),
        name="hyena_filters",
    )(feats, w1, b1.reshape(1, hid), w2, b2.reshape(1, hid), freq.reshape(1, hid), w3, w3, decay,
      skip.reshape(HY_ORDER, 1, width), cs)


def _hyena_kernel(z_ref, sw_ref, sb_ref, spec_ref, nyq_ref, cs_ref, o_ref,
                  u_ref, g_ref, ub_ref, re_ref, im_ref, yri_ref, *, blk, cb):
    seq, width = o_ref.shape[1], o_ref.shape[2]
    nb = seq // blk
    sign = _alt_sign((blk, cb))

    def short_conv(dst_ref, g, cols):
        zc = slice(g * width + cols.start, g * width + cols.stop)
        z = z_ref[0, :, zc].astype(F32)
        w0, w1, w2 = sw_ref[g, 0:1, cols], sw_ref[g, 1:2, cols], sw_ref[g, 2:3, cols]
        dst_ref[...] = sb_ref[g, :, cols] + pltpu.roll(z, 1, 0) * w0 + z * w1 + pltpu.roll(z, seq - 1, 0) * w2
        dst_ref[0:1, :] = dst_ref[0:1, :] - z_ref[0, seq - 1:seq, zc].astype(F32) * w0
        dst_ref[seq - 1:seq, :] = dst_ref[seq - 1:seq, :] - z_ref[0, 0:1, zc].astype(F32) * w2

    def long_conv(o, g, cols):
        ub_ref[...] = u_ref[...].astype(BF16)
        nyq_in = []
        for j in range(nb):
            rows = slice(j * blk, (j + 1) * blk)
            re_ref[j] = _dot(cs_ref[:, 0:blk], ub_ref[rows, :]).astype(BF16)
            im_ref[j] = _dot(cs_ref[:, blk:2 * blk], ub_ref[rows, :]).astype(BF16)
            nyq_in.append(jnp.sum(sign * u_ref[rows, :], axis=0, keepdims=True))
        short_conv(g_ref, g, cols)
        for i in range(nb):
            yr = yi = nyq = None
            for j in range(nb):
                d = i - j + nb - 1
                gr, gi = spec_ref[o, d, 0, :, cols], spec_ref[o, d, 1, :, cols]
                re, im = re_ref[j], im_ref[j]
                tr = re * gr - im * gi
                ti = re * gi + im * gr
                tn = nyq_in[j] * nyq_ref[o, d, :, cols]
                yr, yi, nyq = (tr, ti, tn) if yr is None else (yr + tr, yi + ti, nyq + tn)
            yri_ref[0:blk, :] = yr
            yri_ref[blk:2 * blk, :] = yi
            rows = slice(i * blk, (i + 1) * blk)
            y = _dot(cs_ref[...], yri_ref[...]) + sign * nyq
            g_ref[rows, :] = g_ref[rows, :] * y
        u_ref[...] = g_ref[...]

    for c in range(0, width, cb):
        cols = slice(c, c + cb)
        short_conv(u_ref, 0, cols)
        long_conv(0, 1, cols)
        long_conv(1, 2, cols)
        o_ref[0, :, cols] = u_ref[...].astype(o_ref.dtype)


def _hyena(zx, short_w, short_b, spec, nyq, cs, width, cb, blk):
    bsz, seq, _ = zx.shape
    nb = seq // blk
    sw = short_w.reshape(SHORT_CONV, 3, width).transpose(1, 0, 2)
    sb = short_b.reshape(3, 1, width)
    return pl.pallas_call(
        functools.partial(_hyena_kernel, blk=blk, cb=cb),
        grid=(bsz,),
        in_specs=[
            pl.BlockSpec((1, seq, 3 * width), lambda i: (i, 0, 0)),
            _const_spec(sw.shape),
            _const_spec(sb.shape),
            _const_spec(spec.shape),
            _const_spec(nyq.shape),
            _const_spec((blk, 2 * blk)),
        ],
        out_specs=pl.BlockSpec((1, seq, width), lambda i: (i, 0, 0)),
        out_shape=jax.ShapeDtypeStruct((bsz, seq, width), BF16),
        scratch_shapes=[
            pltpu.VMEM((seq, cb), F32),
            pltpu.VMEM((seq, cb), F32),
            pltpu.VMEM((seq, cb), BF16),
            pltpu.VMEM((nb, blk, cb), BF16),
            pltpu.VMEM((nb, blk, cb), BF16),
            pltpu.VMEM((2 * blk, cb), BF16),
        ],
        compiler_params=_cparams("parallel"),
        name="hyena",
    )(zx, sw, sb, spec, nyq, cs)


def _na_col_bias_kernel(rpb_ref, sel_ref, mask_ref, rowmask_ref, o_ref):
    acc = jnp.zeros(o_ref.shape, F32) + mask_ref[...] + rowmask_ref[...]
    for ci in range(sel_ref.shape[0]):
        acc = acc + (rpb_ref[:, ci:ci + 1] * LOG2_E) * sel_ref[ci:ci + 1, :]
    o_ref[...] = acc


def _na_bias_tables(rpb, rows):
    rq, sl, half = NA_ROWS_PER_STEP, NA_SLAB_ROWS, NA_WIN_H // 2
    n_ri, n_ci = 2 * NA_WIN_H - 1, 2 * NA_WIN_W - 1
    qc = np.arange(GRID_W)
    kc = np.arange(GRID_W)
    ws = np.clip(qc - NA_WIN_W // 2, 0, GRID_W - NA_WIN_W)
    in_win = (kc[None, :] >= ws[:, None]) & (kc[None, :] < ws[:, None] + NA_WIN_W)
    ci = np.clip(kc[None, :] - qc[:, None] + NA_WIN_W - 1, 0, n_ci - 1)
    sel = ((ci[None] == np.arange(n_ci)[:, None, None]) & in_win[None]).astype(np.float32)
    sel = sel.reshape(n_ci, GRID_W * GRID_W)
    mask = np.where(in_win, 0.0, MASK_VALUE).astype(np.float32).reshape(1, GRID_W * GRID_W)
    rowmask = np.tile(np.where(np.arange(n_ri + 1) == n_ri, MASK_VALUE, 0.0).astype(np.float32), NA_HEADS)
    rpb_x = jnp.pad(rpb, ((0, 0), (0, 1), (0, 0))).reshape(NA_HEADS * (n_ri + 1), n_ci)
    col = pl.pallas_call(
        _na_col_bias_kernel,
        out_shape=jax.ShapeDtypeStruct((NA_HEADS * (n_ri + 1), GRID_W * GRID_W), F32),
        name="na_col_bias",
    )(rpb_x, jnp.asarray(sel), jnp.asarray(mask), jnp.asarray(rowmask.reshape(-1, 1)))
    col = col.reshape(NA_HEADS, n_ri + 1, GRID_W, GRID_W)

    n_steps = rows // rq
    steps = (0, 1, n_steps - 1)
    ri = np.full((len(steps), rq, sl), n_ri, np.int32)
    for c, step in enumerate(steps):
        start = int(np.clip(step * rq - half, 0, rows - sl))
        for i in range(rq):
            r = step * rq + i
            rs = int(np.clip(r - half, 0, rows - NA_WIN_H))
            for j in range(NA_WIN_H):
                ri[c, i, rs - start + j] = rs + j - r + NA_WIN_H - 1
    return pl.pallas_call(
        functools.partial(_na_bias_assemble_kernel, ri=ri),
        out_shape=jax.ShapeDtypeStruct((len(steps), NA_HEADS, rq * GRID_W, sl * GRID_W), F32),
        compiler_params=pltpu.CompilerParams(vmem_limit_bytes=VMEM_LIMIT),
        name="na_bias_tables",
    )(col)


def _na_bias_assemble_kernel(col_ref, o_ref, *, ri):
    n_cfg, rq, sl = ri.shape
    per_tile = LANES // GRID_W
    for c in range(n_cfg):
        for i in range(rq):
            for j in range(0, sl, per_tile):
                tile = jnp.concatenate([col_ref[:, int(ri[c, i, j + jj])] for jj in range(per_tile)], axis=-1)
                o_ref[c, :, i * GRID_W:(i + 1) * GRID_W, j * GRID_W:(j + per_tile) * GRID_W] = tile


def _na_kernel(q_ref, k_ref, v_ref, kvc_ref, bias_ref, *rest, rows):
    n_cast = (len(rest) - 1) // 2
    o_ref = rest[n_cast]
    for src, dst in zip(rest[:n_cast], rest[n_cast + 1:]):
        dst[...] = src[...].astype(dst.dtype)
    rq, sl = NA_ROWS_PER_STEP, NA_SLAB_ROWS
    width = NA_HEADS * NA_HEAD_DIM
    step = pl.program_id(1)
    start = jnp.clip(step * rq - NA_WIN_H // 2, 0, rows - sl)
    start = pl.multiple_of(start * GRID_W, GRID_W)
    nh = LANES // NA_HEAD_DIM
    nq = rq * GRID_W
    lane = lax.broadcasted_iota(jnp.int32, (nq, LANES), 1)
    owns = [(lane >= s * NA_HEAD_DIM) & (lane < (s + 1) * NA_HEAD_DIM) for s in range(nh)]

    for hp in range(width // LANES):
        cols = slice(hp * LANES, (hp + 1) * LANES)
        q2 = q_ref[0, :, cols]
        k2 = k_ref[0, pl.ds(start, sl * GRID_W), cols]
        v2 = v_ref[0, pl.ds(start, sl * GRID_W), cols]
        kc2 = kvc_ref[0, :, cols]
        vc2 = kvc_ref[0, :, width + hp * LANES:width + (hp + 1) * LANES]
        qm = jnp.concatenate([jnp.where(own, q2, jnp.zeros_like(q2)) for own in owns], axis=0)
        s_loc = _dot_nt(qm, k2) + bias_ref[0, hp * nh:(hp + 1) * nh].reshape(nh * nq, sl * GRID_W)
        s_ctx = _dot_nt(qm, kc2)
        m = jnp.maximum(jnp.max(s_loc, axis=-1, keepdims=True), jnp.max(s_ctx, axis=-1, keepdims=True))
        p_loc = jnp.exp2(s_loc - m)
        p_ctx = jnp.exp2(s_ctx - m)
        den = jnp.sum(p_loc, axis=-1, keepdims=True) + jnp.sum(p_ctx, axis=-1, keepdims=True)
        o = (_dot(p_loc.astype(BF16), v2) + _dot(p_ctx.astype(BF16), vc2)) / den
        o_ref[0, :, cols] = sum(jnp.where(owns[s], o[s * nq:(s + 1) * nq], 0.0) for s in range(nh)).astype(o_ref.dtype)


def _na(zx, kvc, bias, col_q, cast_2d):
    bsz, seq, _ = zx.shape
    rows = seq // GRID_W
    rq = NA_ROWS_PER_STEP
    assert rows % rq == 0 and rows >= NA_SLAB_ROWS and rows // rq >= 3
    n_steps = rows // rq
    width = NA_HEADS * NA_HEAD_DIM
    qb = col_q // width
    total = bsz * n_steps
    for w in cast_2d:
        assert w.shape[0] % (total * BF16_SUBLANES) == 0, (w.shape, total)

    def cfg(s):
        return jnp.minimum(s, 1) + jnp.maximum(s - (n_steps - 2), 0)

    def cast_spec(w):
        return pl.BlockSpec((w.shape[0] // total, w.shape[1]), lambda i, s: (i * n_steps + s, 0))

    outs = pl.pallas_call(
        functools.partial(_na_kernel, rows=rows),
        grid=(bsz, n_steps),
        in_specs=[
            pl.BlockSpec((1, rq * GRID_W, width), lambda i, s: (i, s, qb)),
            pl.BlockSpec((1, seq, width), lambda i, s: (i, 0, qb + 1)),
            pl.BlockSpec((1, seq, width), lambda i, s: (i, 0, qb + 2)),
            pl.BlockSpec((1,) + kvc.shape[1:], lambda i, s: (i, 0, 0)),
            pl.BlockSpec((1,) + bias.shape[1:], lambda i, s: (cfg(s), 0, 0, 0)),
        ] + [cast_spec(w) for w in cast_2d],
        out_specs=[pl.BlockSpec((1, rq * GRID_W, width), lambda i, s: (i, s, 0))] + [cast_spec(w) for w in cast_2d],
        out_shape=[jax.ShapeDtypeStruct((bsz, seq, width), BF16)]
        + [jax.ShapeDtypeStruct(w.shape, BF16) for w in cast_2d],
        compiler_params=_cparams("parallel", "arbitrary"),
        name="na",
    )(zx, zx, zx, kvc, bias, *cast_2d)
    return outs[0], outs[1:]


def _cast_in_na(weights, total_steps):
    max_block_bytes = 1 << 20
    for w in weights:
        rows = w.shape[0] * w.shape[1]
        if rows % (total_steps * BF16_SUBLANES) or rows // total_steps * w.shape[2] * 4 > max_block_bytes:
            return False
    return True


def _merge_kernel(hy_ref, na_ref, ghy_ref, gna_ref, x_ref, mod_ref, g2_ref, wbh_ref, wbn_ref, wo_ref, wr_ref,
                  x1_ref, h2_ref, lg_ref):
    a = _dot(hy_ref[0], wbh_ref[...])
    b = _dot(na_ref[0], wbn_ref[...])
    m = jax.nn.sigmoid(ghy_ref[0].astype(F32)) * a + jax.nn.sigmoid(gna_ref[0].astype(F32)) * b
    mix = _dot(m.astype(BF16), wo_ref[...])
    x1 = x_ref[0] + mod_ref[0, 2:3, :] * mix
    x1_ref[0] = x1
    h2 = _rms_mod(x1, g2_ref[...], mod_ref[0, 3:4, :], mod_ref[0, 4:5, :])
    h2_ref[0] = h2.astype(h2_ref.dtype)
    lg_ref[0] = _dot_nt(wr_ref[...], h2)


def _merge(hy, na, zx, x, mod3, g2, wbh, wbn, wo, wr_t, col_g, tm):
    bsz, seq, d = x.shape
    gb = col_g // d
    ne = wr_t.shape[0]
    return pl.pallas_call(
        _merge_kernel,
        grid=(bsz, seq // tm),
        in_specs=[
            pl.BlockSpec((1, tm, hy.shape[2]), lambda i, j: (i, j, 0)),
            pl.BlockSpec((1, tm, na.shape[2]), lambda i, j: (i, j, 0)),
            pl.BlockSpec((1, tm, d), lambda i, j: (i, j, gb)),
            pl.BlockSpec((1, tm, d), lambda i, j: (i, j, gb + 1)),
            pl.BlockSpec((1, tm, d), lambda i, j: (i, j, 0)),
            pl.BlockSpec((1, 6, d), lambda i, j: (i, 0, 0)),
            _const_spec((1, d)),
            _const_spec(wbh.shape),
            _const_spec(wbn.shape),
            _const_spec(wo.shape),
            _const_spec(wr_t.shape),
        ],
        out_specs=[
            pl.BlockSpec((1, tm, d), lambda i, j: (i, j, 0)),
            pl.BlockSpec((1, tm, d), lambda i, j: (i, j, 0)),
            pl.BlockSpec((1, ne, tm), lambda i, j: (i, 0, j)),
        ],
        out_shape=[
            jax.ShapeDtypeStruct((bsz, seq, d), F32),
            jax.ShapeDtypeStruct((bsz, seq, d), BF16),
            jax.ShapeDtypeStruct((bsz, ne, seq), F32),
        ],
        compiler_params=_cparams("parallel", "parallel"),
        name="merge",
    )(hy, na, zx, zx, x, mod3, g2.reshape(1, d), wbh, wbn, wo, wr_t)


def _route_kernel(lg_ref, tri_ref, rank_ref, rank_t_ref, gate_t_ref, *, cap):
    grp, ne, t = lg_ref.shape
    lg = lg_ref[...]
    e = jnp.exp(lg - jnp.max(lg, axis=1, keepdims=True))
    aff = (e / jnp.sum(e, axis=1, keepdims=True)).reshape(grp * ne, t)

    def bit_step(i, bits):
        cand = bits | (jnp.int32(1) << (30 - i))
        keep = jnp.sum((aff >= pltpu.bitcast(cand, F32)).astype(jnp.int32), axis=1, keepdims=True) >= cap
        return jnp.where(keep, cand, bits)

    thr = pltpu.bitcast(lax.fori_loop(0, 31, bit_step, jnp.zeros((grp * ne, 1), jnp.int32)), F32)
    above = aff > thr
    tie = aff == thr
    need = cap - jnp.sum(above.astype(jnp.int32), axis=1, keepdims=True)
    tri = tri_ref[...]
    tie_before = _dot(tie.astype(BF16), tri)
    sel = above | (tie & (tie_before < need.astype(F32)))
    sel_before = _dot(sel.astype(BF16), tri)
    rank = jnp.where(sel, sel_before, -1.0)
    gate = jnp.where(sel, aff, 0.0)
    rank_ref[...] = rank.astype(jnp.int32).reshape(grp, ne, t)
    pad_r = jnp.full((LANES - ne, t), -1.0, F32)
    pad_g = jnp.zeros((LANES - ne, t), F32)
    for g in range(grp):
        rows = slice(g * ne, (g + 1) * ne)
        rank_t_ref[g] = jnp.concatenate([rank[rows], pad_r], axis=0).T.astype(jnp.int32)
        gate_t_ref[g] = jnp.concatenate([gate[rows], pad_g], axis=0).T


def _route(logits_t, cap, grp):
    bsz, ne, t = logits_t.shape
    i = jnp.arange(t, dtype=jnp.int32)
    tri = (i[:, None] < i[None, :]).astype(BF16)
    return pl.pallas_call(
        functools.partial(_route_kernel, cap=cap),
        grid=(bsz // grp,),
        in_specs=[
            pl.BlockSpec((grp, ne, t), lambda b: (b, 0, 0)),
            _const_spec((t, t)),
        ],
        out_specs=[
            pl.BlockSpec((grp, ne, t), lambda b: (b, 0, 0)),
            pl.BlockSpec((grp, t, LANES), lambda b: (b, 0, 0)),
            pl.BlockSpec((grp, t, LANES), lambda b: (b, 0, 0)),
        ],
        out_shape=[
            jax.ShapeDtypeStruct((bsz, ne, t), jnp.int32),
            jax.ShapeDtypeStruct((bsz, t, LANES), jnp.int32),
            jax.ShapeDtypeStruct((bsz, t, LANES), F32),
        ],
        compiler_params=_cparams("parallel"),
        name="route",
    )(logits_t, tri)


def _expert_kernel(rank_ref, h_ref, wg_ref, wu_ref, wd_ref, y_ref, *, cap):
    e = pl.program_id(0)
    grp, t, _ = h_ref.shape
    slot = lax.broadcasted_iota(jnp.int32, (cap, t), 0)
    xe = []
    for b in range(grp):
        rank = rank_ref[b, pl.ds(e, 1), :]
        onehot = jnp.where(rank == slot, 1.0, 0.0).astype(BF16)
        xe.append(_dot(onehot, h_ref[b]).astype(BF16))
    xe = jnp.concatenate(xe, axis=0)
    g = _dot(xe, wg_ref[0])
    u = _dot(xe, wu_ref[0])
    act = (g * jax.nn.sigmoid(g) * u).astype(BF16)
    y = _dot(act, wd_ref[0])
    for b in range(grp):
        y_ref[b, 0] = y[b * cap:(b + 1) * cap].astype(y_ref.dtype)


def _experts(rank, h2, wg, wu, wd, cap, grp):
    bsz, t, d = h2.shape
    ne, _, f = wg.shape

    def wspec(shape):
        return pl.BlockSpec((1,) + shape, lambda e, b: (e, 0, 0), pipeline_mode=pl.Buffered(1))

    return pl.pallas_call(
        functools.partial(_expert_kernel, cap=cap),
        grid=(ne, bsz // grp),
        in_specs=[
            pl.BlockSpec((grp, ne, t), lambda e, b: (b, 0, 0)),
            pl.BlockSpec((grp, t, d), lambda e, b: (b, 0, 0)),
            wspec((d, f)), wspec((d, f)), wspec((f, d)),
        ],
        out_specs=pl.BlockSpec((grp, 1, cap, d), lambda e, b: (b, e, 0, 0)),
        out_shape=jax.ShapeDtypeStruct((bsz, ne, cap, d), BF16),
        compiler_params=_cparams("arbitrary", "arbitrary"),
        name="experts",
    )(rank, h2, wg, wu, wd)


def _combine_kernel(rank_t_ref, gate_t_ref, y_ref, x1_ref, mod_ref, gf_ref, o_ref, *, cap):
    ne = y_ref.shape[1]
    tm = x1_ref.shape[1]
    slot = lax.broadcasted_iota(jnp.int32, (tm, cap), 1)
    acc = jnp.zeros(x1_ref.shape[1:], F32)
    for e in range(ne):
        r = rank_t_ref[0, :, e:e + 1]
        g = gate_t_ref[0, :, e:e + 1]
        scat = jnp.where(r == slot, g, 0.0).astype(BF16)
        acc = acc + _dot(scat, y_ref[0, e])
    x2 = x1_ref[0] + mod_ref[0, 5:6, :] * acc
    ms = jnp.mean(x2 * x2, axis=-1, keepdims=True)
    o_ref[0] = x2 * lax.rsqrt(ms + EPS) * gf_ref[...]


def _combine(rank_t, gate_t, y, x1, mod3, final_g, cap, tm):
    bsz, t, d = x1.shape
    ne = y.shape[1]
    return pl.pallas_call(
        functools.partial(_combine_kernel, cap=cap),
        grid=(bsz, t // tm),
        in_specs=[
            pl.BlockSpec((1, tm, LANES), lambda i, j: (i, j, 0)),
            pl.BlockSpec((1, tm, LANES), lambda i, j: (i, j, 0)),
            pl.BlockSpec((1, ne, cap, d), lambda i, j: (i, 0, 0, 0)),
            pl.BlockSpec((1, tm, d), lambda i, j: (i, j, 0)),
            pl.BlockSpec((1, 6, d), lambda i, j: (i, 0, 0)),
            _const_spec((1, d)),
        ],
        out_specs=pl.BlockSpec((1, tm, d), lambda i, j: (i, j, 0)),
        out_shape=jax.ShapeDtypeStruct((bsz, t, d), F32),
        compiler_params=_cparams("parallel", "arbitrary"),
        name="combine",
    )(rank_t, gate_t, y, x1, mod3, final_g.reshape(1, d))


def kernel(x, c, ctx, c_ctx, w_mod, b_mod, norm1_g, norm2_g, w_in, b_in, hy_short_w, hy_short_b, hy_skip, filt_w1, filt_b1, filt_w2, filt_b2, filt_w3, filt_freq, na_rpb, w_branch_hy, w_branch_na, w_out, w_router, w_gate, w_up, w_down, final_g):
    depth = w_mod.shape[0]
    bsz, seq, d = x.shape
    hy_width = w_branch_hy.shape[1]
    na_width = w_branch_na.shape[1]
    col_q = 3 * hy_width
    col_k = col_q + na_width
    col_g = col_q + 3 * na_width
    cap = EC_CAPACITY * seq // N_EXPERTS
    rows = seq // GRID_W
    mod_rows = -(-(bsz + 1) // 8) * 8

    hy_blk = HY_BLOCK
    cs = _dft_matrix(hy_blk)
    for i in range(depth):
        assert i == depth - 1, "only the final layer's data flow (context feeds keys/values only) is implemented"
        cc = jnp.zeros((mod_rows, d), F32).at[:bsz].set(c).at[bsz].set(c_ctx)
        mod3 = _modulation(cc, w_mod[i], b_mod[i]).reshape(mod_rows, 6, d)

        qscale = jnp.ones((w_in.shape[2],), F32).at[col_q:col_k].set(NA_HEAD_DIM ** -0.5 * LOG2_E)
        w_in_s = (w_in[i] * qscale).astype(BF16)
        b_in_s = b_in[i] * qscale
        zx = _in_proj(x, mod3, lambda b: b, norm1_g[i], w_in_s, b_in_s, tm=TOKEN_TILE)
        kvc = _in_proj(ctx.reshape(1, -1, d), mod3, lambda b: bsz, norm1_g[i], w_in_s[:, col_k:col_g],
                       b_in_s[col_k:col_g], tm=math.gcd(bsz * ctx.shape[1], TOKEN_TILE)).reshape(bsz, ctx.shape[1], -1)

        spec, nyq = _filter_spectra(seq, hy_width, filt_w1[i], filt_b1[i], filt_w2[i], filt_b2[i], filt_w3[i],
                                    filt_freq[i], hy_skip[i], cs, cb=HY_CHANNEL_BLOCK, blk=hy_blk)
        hy = _hyena(zx, hy_short_w[i], hy_short_b[i], spec, nyq, cs, hy_width, cb=HY_CHANNEL_BLOCK, blk=hy_blk)

        experts_w = (w_gate[i], w_up[i], w_down[i])
        if _cast_in_na(experts_w, bsz * (rows // NA_ROWS_PER_STEP)):
            na, experts_w = _na(zx, kvc, _na_bias_tables(na_rpb[i], rows), col_q,
                                [w.reshape(-1, w.shape[2]) for w in experts_w])
            experts_w = [w2.reshape(w.shape) for w2, w in zip(experts_w, (w_gate[i], w_up[i], w_down[i]))]
        else:
            na, _ = _na(zx, kvc, _na_bias_tables(na_rpb[i], rows), col_q, [])
            experts_w = [w.astype(BF16) for w in experts_w]

        x1, h2, logits_t = _merge(hy, na, zx, x, mod3, norm2_g[i], w_branch_hy[i].astype(BF16),
                                  w_branch_na[i].astype(BF16), w_out[i].astype(BF16), w_router[i].T, col_g, tm=TOKEN_TILE)
        rank, rank_t, gate_t = _route(logits_t, cap, grp=math.gcd(bsz, 8))
        y = _experts(rank, h2, *experts_w, cap, grp=math.gcd(bsz, 2))
        x = _combine(rank_t, gate_t, y, x1, mod3, final_g, cap, tm=TOKEN_TILE)
    return x
```

```python
import functools
import math

import jax
import jax.numpy as jnp
import numpy as np
from jax import lax
from jax.experimental import pallas as pl
from jax.experimental.pallas import tpu as pltpu

F32 = jnp.float32
BF16 = jnp.bfloat16

EPS = 1e-6
GRID_W = 64
HY_ORDER = 2
SHORT_CONV = 3
FILT_BANDS = 8
DECAY_TARGET = 1e-2
FAST_DECAY_PCT = 0.3
SLOW_DECAY_PCT = 1.5
HY_BLOCK = 512
NA_HEADS = 8
NA_HEAD_DIM = 64
NA_WIN_H = 8
NA_WIN_W = 16
NA_ROWS_PER_STEP = 4
NA_SLAB_ROWS = NA_ROWS_PER_STEP + NA_WIN_H
N_EXPERTS = 16
EC_CAPACITY = 2

MASK_VALUE = -1e30
LOG2_E = 1.4426950408889634
LANES = 128
BF16_SUBLANES = 16
VMEM_LIMIT = 56 * 1024 * 1024
TOKEN_TILE = 1024
PROJ_COL_CHUNK = 1024
MOD_COL_TILE = 1536
HY_CHANNEL_BLOCK = 256


def _cparams(*sem):
    return pltpu.CompilerParams(dimension_semantics=sem, vmem_limit_bytes=VMEM_LIMIT)


def _const_spec(shape):
    nd = len(shape)
    return pl.BlockSpec(shape, lambda *_: (0,) * nd, pipeline_mode=pl.Buffered(1))


def _dot(a, b):
    return jnp.dot(a, b, preferred_element_type=F32)


def _dot_nt(a, b):
    return lax.dot_general(a, b, (((1,), (1,)), ((), ())), preferred_element_type=F32)


def _mod_kernel(c_ref, w_ref, b_ref, o_ref):
    c = c_ref[...]
    s = c * jax.nn.sigmoid(c)
    o_ref[...] = _dot(s, w_ref[...]) + b_ref[...]


def _modulation(cc, w_mod, b_mod):
    rows, d = cc.shape
    n = w_mod.shape[1]
    tn = MOD_COL_TILE
    return pl.pallas_call(
        _mod_kernel,
        grid=(n // tn,),
        in_specs=[
            pl.BlockSpec((rows, d), lambda j: (0, 0)),
            pl.BlockSpec((d, tn), lambda j: (0, j)),
            pl.BlockSpec((1, tn), lambda j: (0, j)),
        ],
        out_specs=pl.BlockSpec((rows, tn), lambda j: (0, j)),
        out_shape=jax.ShapeDtypeStruct((rows, n), F32),
        compiler_params=_cparams("arbitrary"),
        name="modulation",
    )(cc, w_mod, b_mod.reshape(1, n))


def _rms_mod(x, g, shift, scale):
    ms = jnp.mean(x * x, axis=-1, keepdims=True)
    return (x * lax.rsqrt(ms + EPS) * g) * (1.0 + scale) + shift


def _in_proj_kernel(x_ref, mod_ref, g_ref, w_ref, b_ref, o_ref, *, n_chunk):
    h = _rms_mod(x_ref[0], g_ref[...], mod_ref[0, 0:1, :], mod_ref[0, 1:2, :]).astype(BF16)
    n = w_ref.shape[1]
    for j in range(0, n, n_chunk):
        z = _dot(h, w_ref[:, j:j + n_chunk]) + b_ref[:, j:j + n_chunk]
        o_ref[0, :, j:j + n_chunk] = z.astype(o_ref.dtype)


def _in_proj(x, mod3, mod_row, g, w, b, tm):
    bsz, t, d = x.shape
    n = w.shape[1]
    return pl.pallas_call(
        functools.partial(_in_proj_kernel, n_chunk=min(n, PROJ_COL_CHUNK)),
        grid=(bsz, t // tm),
        in_specs=[
            pl.BlockSpec((1, tm, d), lambda i, j: (i, j, 0)),
            pl.BlockSpec((1, 6, d), lambda i, j: (mod_row(i), 0, 0)),
            _const_spec((1, d)),
            _const_spec((d, n)),
            _const_spec((1, n)),
        ],
        out_specs=pl.BlockSpec((1, tm, n), lambda i, j: (i, j, 0)),
        out_shape=jax.ShapeDtypeStruct((bsz, t, n), BF16),
        compiler_params=_cparams("parallel", "parallel"),
        name="in_proj",
    )(x, mod3, g.reshape(1, d), w, b.reshape(1, n))


def _dft_matrix(blk):
    n = 2 * blk
    i = jnp.arange(blk, dtype=jnp.int32)
    ft = (i[:, None] * i[None, :]) % n
    ang = ft.astype(F32) * (2.0 * math.pi / n)
    return jnp.concatenate([jnp.cos(ang), -jnp.sin(ang)], axis=1).astype(BF16)


def _alt_sign(shape):
    row = lax.broadcasted_iota(jnp.int32, shape, 0)
    return (1 - 2 * (row & 1)).astype(F32)


def _filter_kernel(feats_ref, w1_ref, b1_ref, w2_ref, b2_ref, freq_ref, w3p_ref, w3f_ref, decay_ref, skip_ref,
                   cs_ref, spec_ref, nyq_ref, h_ref, k2_ref, ar_ref, ai_ref, *, blk):
    seq2, cb = k2_ref.shape
    seq = seq2 // 2
    nblk = seq2 // blk
    n = 2 * blk

    @pl.when((pl.program_id(0) == 0) & (pl.program_id(1) == 0))
    def _():
        freq = freq_ref[...]
        h1 = jnp.sin(freq * (_dot(feats_ref[...], w1_ref[...]) + b1_ref[...]))
        h_ref[...] = jnp.sin(freq * (_dot(h1, w2_ref[...]) + b2_ref[...]))

    h = h_ref[...]
    row = lax.broadcasted_iota(jnp.int32, (seq2, cb), 0)
    k2 = jnp.where(row < seq, _dot(h, w3f_ref[...]), _dot(h, w3p_ref[...])) * decay_ref[...]
    k2_ref[...] = jnp.where(row == 0, 0.0, k2)
    sign = _alt_sign((blk, cb))
    for d in range(nblk):
        a = k2_ref[d * blk:(d + 1) * blk, :]
        ab = a.astype(BF16)
        ar_ref[d] = _dot(cs_ref[:, 0:blk], ab)
        ai_ref[d] = _dot(cs_ref[:, blk:n], ab)
    lag0 = nblk // 2
    ar_ref[lag0] = ar_ref[lag0] + skip_ref[0]
    k2_ref[seq:seq + 1, :] = k2_ref[seq:seq + 1, :] + skip_ref[0]
    frow = lax.broadcasted_iota(jnp.int32, (blk, cb), 0)
    wgt = jnp.where(frow == 0, 1.0 / n, 2.0 / n)
    for d in range(1, nblk):
        a0 = k2_ref[(d - 1) * blk:(d - 1) * blk + 1, :]
        spec_ref[0, d - 1, 0] = ((ar_ref[d] + sign * (ar_ref[d - 1] - a0)) * wgt).astype(spec_ref.dtype)
        spec_ref[0, d - 1, 1] = ((ai_ref[d] + sign * ai_ref[d - 1]) * wgt).astype(spec_ref.dtype)
        cur = jnp.sum(sign * k2_ref[d * blk:(d + 1) * blk, :], axis=0, keepdims=True)
        prev = jnp.sum(sign * k2_ref[(d - 1) * blk:d * blk, :], axis=0, keepdims=True)
        nyq_ref[0, d - 1] = (cur + prev - a0) * (1.0 / n)


def _filter_spectra(seq, width, w1, b1, w2, b2, w3, freq, skip, cs, cb, blk):
    pos = jnp.abs(jnp.arange(2 * seq, dtype=F32) - seq)
    t = pos / max(seq - 1, 1)
    omega = 2.0 * math.pi * pos / seq
    bands = jnp.linspace(1e-4, FILT_BANDS - 1, FILT_BANDS, dtype=F32)
    ang = omega[:, None] * bands[None, :]
    feats = jnp.concatenate([t[:, None], jnp.cos(ang), -jnp.sin(ang)], axis=-1)
    emb, hid = w1.shape
    emb_pad = -(-emb // 8) * 8
    feats = jnp.pad(feats, ((0, 0), (0, emb_pad - emb)))
    w1 = jnp.pad(w1, ((0, emb_pad - emb), (0, 0)))
    max_decay = math.log(DECAY_TARGET) / FAST_DECAY_PCT
    min_decay = math.log(DECAY_TARGET) / SLOW_DECAY_PCT
    deltas = jnp.linspace(min_decay, max_decay, width, dtype=F32)
    decay = jnp.exp(-t[:, None] * jnp.abs(deltas)[None, :])
    ncb = width // cb
    nblk = 2 * seq // blk
    return pl.pallas_call(
        functools.partial(_filter_kernel, blk=blk),
        grid=(HY_ORDER, ncb),
        in_specs=[
            _const_spec((2 * seq, emb_pad)),
            _const_spec((emb_pad, hid)),
            _const_spec((1, hid)),
            _const_spec((hid, hid)),
            _const_spec((1, hid)),
            _const_spec((1, hid)),
            pl.BlockSpec((hid, cb), lambda o, c: (0, o * ncb + c)),
            pl.BlockSpec((hid, cb), lambda o, c: (0, (HY_ORDER + o) * ncb + c)),
            pl.BlockSpec((2 * seq, cb), lambda o, c: (0, c)),
            pl.BlockSpec((1, 1, cb), lambda o, c: (o, 0, c)),
            _const_spec((blk, 2 * blk)),
        ],
        out_specs=[
            pl.BlockSpec((1, nblk - 1, 2, blk, cb), lambda o, c: (o, 0, 0, 0, c)),
            pl.BlockSpec((1, nblk - 1, 1, cb), lambda o, c: (o, 0, 0, c)),
        ],
        out_shape=[
            jax.ShapeDtypeStruct((HY_ORDER, nblk - 1, 2, blk, width), BF16),
            jax.ShapeDtypeStruct((HY_ORDER, nblk - 1, 1, width), F32),
        ],
        scratch_shapes=[
            pltpu.VMEM((2 * seq, hid), F32),
            pltpu.VMEM((2 * seq, cb), F32),
            pltpu.VMEM((nblk, blk, cb), F32),
            pltpu.VMEM((nblk, blk, cb), F32),
        ],
        compiler_params=_cparams("arbitrary", "arbitrary"),
        name="hyena_filters",
    )(feats, w1, b1.reshape(1, hid), w2, b2.reshape(1, hid), freq.reshape(1, hid), w3, w3, decay,
      skip.reshape(HY_ORDER, 1, width), cs)


def _hyena_kernel(z_ref, sw_ref, sb_ref, spec_ref, nyq_ref, cs_ref, o_ref,
                  u_ref, g_ref, ub_ref, re_ref, im_ref, yri_ref, *, blk, cb):
    seq, width = o_ref.shape[1], o_ref.shape[2]
    nb = seq // blk
    sign = _alt_sign((blk, cb))

    def short_conv(dst_ref, g, cols):
        zc = slice(g * width + cols.start, g * width + cols.stop)
        z = z_ref[0, :, zc].astype(F32)
        w0, w1, w2 = sw_ref[g, 0:1, cols], sw_ref[g, 1:2, cols], sw_ref[g, 2:3, cols]
        dst_ref[...] = sb_ref[g, :, cols] + pltpu.roll(z, 1, 0) * w0 + z * w1 + pltpu.roll(z, seq - 1, 0) * w2
        dst_ref[0:1, :] = dst_ref[0:1, :] - z_ref[0, seq - 1:seq, zc].astype(F32) * w0
        dst_ref[seq - 1:seq, :] = dst_ref[seq - 1:seq, :] - z_ref[0, 0:1, zc].astype(F32) * w2

    def long_conv(o, g, cols):
        ub_ref[...] = u_ref[...].astype(BF16)
        nyq_in = []
        for j in range(nb):
            rows = slice(j * blk, (j + 1) * blk)
            re_ref[j] = _dot(cs_ref[:, 0:blk], ub_ref[rows, :]).astype(BF16)
            im_ref[j] = _dot(cs_ref[:, blk:2 * blk], ub_ref[rows, :]).astype(BF16)
            nyq_in.append(jnp.sum(sign * u_ref[rows, :], axis=0, keepdims=True))
        short_conv(g_ref, g, cols)
        for i in range(nb):
            yr = yi = nyq = None
            for j in range(nb):
                d = i - j + nb - 1
                gr, gi = spec_ref[o, d, 0, :, cols], spec_ref[o, d, 1, :, cols]
                re, im = re_ref[j], im_ref[j]
                tr = re * gr - im * gi
                ti = re * gi + im * gr
                tn = nyq_in[j] * nyq_ref[o, d, :, cols]
                yr, yi, nyq = (tr, ti, tn) if yr is None else (yr + tr, yi + ti, nyq + tn)
            yri_ref[0:blk, :] = yr
            yri_ref[blk:2 * blk, :] = yi
            rows = slice(i * blk, (i + 1) * blk)
            y = _dot(cs_ref[...], yri_ref[...]) + sign * nyq
            g_ref[rows, :] = g_ref[rows, :] * y
        u_ref[...] = g_ref[...]

    for c in range(0, width, cb):
        cols = slice(c, c + cb)
        short_conv(u_ref, 0, cols)
        long_conv(0, 1, cols)
        long_conv(1, 2, cols)
        o_ref[0, :, cols] = u_ref[...].astype(o_ref.dtype)


def _hyena(zx, short_w, short_b, spec, nyq, cs, width, cb, blk):
    bsz, seq, _ = zx.shape
    nb = seq // blk
    sw = short_w.reshape(SHORT_CONV, 3, width).transpose(1, 0, 2)
    sb = short_b.reshape(3, 1, width)
    return pl.pallas_call(
        functools.partial(_hyena_kernel, blk=blk, cb=cb),
        grid=(bsz,),
        in_specs=[
            pl.BlockSpec((1, seq, 3 * width), lambda i: (i, 0, 0)),
            _const_spec(sw.shape),
            _const_spec(sb.shape),
            _const_spec(spec.shape),
            _const_spec(nyq.shape),
            _const_spec((blk, 2 * blk)),
        ],
        out_specs=pl.BlockSpec((1, seq, width), lambda i: (i, 0, 0)),
        out_shape=jax.ShapeDtypeStruct((bsz, seq, width), BF16),
        scratch_shapes=[
            pltpu.VMEM((seq, cb), F32),
            pltpu.VMEM((seq, cb), F32),
            pltpu.VMEM((seq, cb), BF16),
            pltpu.VMEM((nb, blk, cb), BF16),
            pltpu.VMEM((nb, blk, cb), BF16),
            pltpu.VMEM((2 * blk, cb), BF16),
        ],
        compiler_params=_cparams("parallel"),
        name="hyena",
    )(zx, sw, sb, spec, nyq, cs)


def _na_col_bias_kernel(rpb_ref, sel_ref, mask_ref, rowmask_ref, o_ref):
    acc = jnp.zeros(o_ref.shape, F32) + mask_ref[...] + rowmask_ref[...]
    for ci in range(sel_ref.shape[0]):
        acc = acc + (rpb_ref[:, ci:ci + 1] * LOG2_E) * sel_ref[ci:ci + 1, :]
    o_ref[...] = acc


def _na_bias_tables(rpb, rows):
    rq, sl, half = NA_ROWS_PER_STEP, NA_SLAB_ROWS, NA_WIN_H // 2
    n_ri, n_ci = 2 * NA_WIN_H - 1, 2 * NA_WIN_W - 1
    qc = np.arange(GRID_W)
    kc = np.arange(GRID_W)
    ws = np.clip(qc - NA_WIN_W // 2, 0, GRID_W - NA_WIN_W)
    in_win = (kc[None, :] >= ws[:, None]) & (kc[None, :] < ws[:, None] + NA_WIN_W)
    ci = np.clip(kc[None, :] - qc[:, None] + NA_WIN_W - 1, 0, n_ci - 1)
    sel = ((ci[None] == np.arange(n_ci)[:, None, None]) & in_win[None]).astype(np.float32)
    sel = sel.reshape(n_ci, GRID_W * GRID_W)
    mask = np.where(in_win, 0.0, MASK_VALUE).astype(np.float32).reshape(1, GRID_W * GRID_W)
    rowmask = np.tile(np.where(np.arange(n_ri + 1) == n_ri, MASK_VALUE, 0.0).astype(np.float32), NA_HEADS)
    rpb_x = jnp.pad(rpb, ((0, 0), (0, 1), (0, 0))).reshape(NA_HEADS * (n_ri + 1), n_ci)
    col = pl.pallas_call(
        _na_col_bias_kernel,
        out_shape=jax.ShapeDtypeStruct((NA_HEADS * (n_ri + 1), GRID_W * GRID_W), F32),
        name="na_col_bias",
    )(rpb_x, jnp.asarray(sel), jnp.asarray(mask), jnp.asarray(rowmask.reshape(-1, 1)))
    col = col.reshape(NA_HEADS, n_ri + 1, GRID_W, GRID_W)

    n_steps = rows // rq
    steps = (0, 1, n_steps - 1)
    ri = np.full((len(steps), rq, sl), n_ri, np.int32)
    for c, step in enumerate(steps):
        start = int(np.clip(step * rq - half, 0, rows - sl))
        for i in range(rq):
            r = step * rq + i
            rs = int(np.clip(r - half, 0, rows - NA_WIN_H))
            for j in range(NA_WIN_H):
                ri[c, i, rs - start + j] = rs + j - r + NA_WIN_H - 1
    return pl.pallas_call(
        functools.partial(_na_bias_assemble_kernel, ri=ri),
        out_shape=jax.ShapeDtypeStruct((len(steps), NA_HEADS, rq * GRID_W, sl * GRID_W), F32),
        compiler_params=pltpu.CompilerParams(vmem_limit_bytes=VMEM_LIMIT),
        name="na_bias_tables",
    )(col)


def _na_bias_assemble_kernel(col_ref, o_ref, *, ri):
    n_cfg, rq, sl = ri.shape
    per_tile = LANES // GRID_W
    for c in range(n_cfg):
        for i in range(rq):
            for j in range(0, sl, per_tile):
                tile = jnp.concatenate([col_ref[:, int(ri[c, i, j + jj])] for jj in range(per_tile)], axis=-1)
                o_ref[c, :, i * GRID_W:(i + 1) * GRID_W, j * GRID_W:(j + per_tile) * GRID_W] = tile


def _na_kernel(q_ref, k_ref, v_ref, kvc_ref, bias_ref, *rest, rows):
    n_cast = (len(rest) - 1) // 2
    o_ref = rest[n_cast]
    for src, dst in zip(rest[:n_cast], rest[n_cast + 1:]):
        dst[...] = src[...].astype(dst.dtype)
    rq, sl = NA_ROWS_PER_STEP, NA_SLAB_ROWS
    width = NA_HEADS * NA_HEAD_DIM
    step = pl.program_id(1)
    start = jnp.clip(step * rq - NA_WIN_H // 2, 0, rows - sl)
    start = pl.multiple_of(start * GRID_W, GRID_W)
    nh = LANES // NA_HEAD_DIM
    nq = rq * GRID_W
    lane = lax.broadcasted_iota(jnp.int32, (nq, LANES), 1)
    owns = [(lane >= s * NA_HEAD_DIM) & (lane < (s + 1) * NA_HEAD_DIM) for s in range(nh)]

    for hp in range(width // LANES):
        cols = slice(hp * LANES, (hp + 1) * LANES)
        q2 = q_ref[0, :, cols]
        k2 = k_ref[0, pl.ds(start, sl * GRID_W), cols]
        v2 = v_ref[0, pl.ds(start, sl * GRID_W), cols]
        kc2 = kvc_ref[0, :, cols]
        vc2 = kvc_ref[0, :, width + hp * LANES:width + (hp + 1) * LANES]
        qm = jnp.concatenate([jnp.where(own, q2, jnp.zeros_like(q2)) for own in owns], axis=0)
        s_loc = _dot_nt(qm, k2) + bias_ref[0, hp * nh:(hp + 1) * nh].reshape(nh * nq, sl * GRID_W)
        s_ctx = _dot_nt(qm, kc2)
        m = jnp.maximum(jnp.max(s_loc, axis=-1, keepdims=True), jnp.max(s_ctx, axis=-1, keepdims=True))
        p_loc = jnp.exp2(s_loc - m)
        p_ctx = jnp.exp2(s_ctx - m)
        den = jnp.sum(p_loc, axis=-1, keepdims=True) + jnp.sum(p_ctx, axis=-1, keepdims=True)
        o = (_dot(p_loc.astype(BF16), v2) + _dot(p_ctx.astype(BF16), vc2)) / den
        o_ref[0, :, cols] = sum(jnp.where(owns[s], o[s * nq:(s + 1) * nq], 0.0) for s in range(nh)).astype(o_ref.dtype)


def _na(zx, kvc, bias, col_q, cast_2d):
    bsz, seq, _ = zx.shape
    rows = seq // GRID_W
    rq = NA_ROWS_PER_STEP
    assert rows % rq == 0 and rows >= NA_SLAB_ROWS and rows // rq >= 3
    n_steps = rows // rq
    width = NA_HEADS * NA_HEAD_DIM
    qb = col_q // width
    total = bsz * n_steps
    for w in cast_2d:
        assert w.shape[0] % (total * BF16_SUBLANES) == 0, (w.shape, total)

    def cfg(s):
        return jnp.minimum(s, 1) + jnp.maximum(s - (n_steps - 2), 0)

    def cast_spec(w):
        return pl.BlockSpec((w.shape[0] // total, w.shape[1]), lambda i, s: (i * n_steps + s, 0))

    outs = pl.pallas_call(
        functools.partial(_na_kernel, rows=rows),
        grid=(bsz, n_steps),
        in_specs=[
            pl.BlockSpec((1, rq * GRID_W, width), lambda i, s: (i, s, qb)),
            pl.BlockSpec((1, seq, width), lambda i, s: (i, 0, qb + 1)),
            pl.BlockSpec((1, seq, width), lambda i, s: (i, 0, qb + 2)),
            pl.BlockSpec((1,) + kvc.shape[1:], lambda i, s: (i, 0, 0)),
            pl.BlockSpec((1,) + bias.shape[1:], lambda i, s: (cfg(s), 0, 0, 0)),
        ] + [cast_spec(w) for w in cast_2d],
        out_specs=[pl.BlockSpec((1, rq * GRID_W, width), lambda i, s: (i, s, 0))] + [cast_spec(w) for w in cast_2d],
        out_shape=[jax.ShapeDtypeStruct((bsz, seq, width), BF16)]
        + [jax.ShapeDtypeStruct(w.shape, BF16) for w in cast_2d],
        compiler_params=_cparams("parallel", "arbitrary"),
        name="na",
    )(zx, zx, zx, kvc, bias, *cast_2d)
    return outs[0], outs[1:]


def _cast_in_na(weights, total_steps):
    max_block_bytes = 1 << 20
    for w in weights:
        rows = w.shape[0] * w.shape[1]
        if rows % (total_steps * BF16_SUBLANES) or rows // total_steps * w.shape[2] * 4 > max_block_bytes:
            return False
    return True


def _merge_kernel(hy_ref, na_ref, ghy_ref, gna_ref, x_ref, mod_ref, g2_ref, wbh_ref, wbn_ref, wo_ref, wr_ref,
                  x1_ref, h2_ref, lg_ref):
    a = _dot(hy_ref[0], wbh_ref[...])
    b = _dot(na_ref[0], wbn_ref[...])
    m = jax.nn.sigmoid(ghy_ref[0].astype(F32)) * a + jax.nn.sigmoid(gna_ref[0].astype(F32)) * b
    mix = _dot(m.astype(BF16), wo_ref[...])
    x1 = x_ref[0] + mod_ref[0, 2:3, :] * mix
    x1_ref[0] = x1
    h2 = _rms_mod(x1, g2_ref[...], mod_ref[0, 3:4, :], mod_ref[0, 4:5, :])
    h2_ref[0] = h2.astype(h2_ref.dtype)
    lg_ref[0] = _dot_nt(wr_ref[...], h2)


def _merge(hy, na, zx, x, mod3, g2, wbh, wbn, wo, wr_t, col_g, tm):
    bsz, seq, d = x.shape
    gb = col_g // d
    ne = wr_t.shape[0]
    return pl.pallas_call(
        _merge_kernel,
        grid=(bsz, seq // tm),
        in_specs=[
            pl.BlockSpec((1, tm, hy.shape[2]), lambda i, j: (i, j, 0)),
            pl.BlockSpec((1, tm, na.shape[2]), lambda i, j: (i, j, 0)),
            pl.BlockSpec((1, tm, d), lambda i, j: (i, j, gb)),
            pl.BlockSpec((1, tm, d), lambda i, j: (i, j, gb + 1)),
            pl.BlockSpec((1, tm, d), lambda i, j: (i, j, 0)),
            pl.BlockSpec((1, 6, d), lambda i, j: (i, 0, 0)),
            _const_spec((1, d)),
            _const_spec(wbh.shape),
            _const_spec(wbn.shape),
            _const_spec(wo.shape),
            _const_spec(wr_t.shape),
        ],
        out_specs=[
            pl.BlockSpec((1, tm, d), lambda i, j: (i, j, 0)),
            pl.BlockSpec((1, tm, d), lambda i, j: (i, j, 0)),
            pl.BlockSpec((1, ne, tm), lambda i, j: (i, 0, j)),
        ],
        out_shape=[
            jax.ShapeDtypeStruct((bsz, seq, d), F32),
            jax.ShapeDtypeStruct((bsz, seq, d), BF16),
            jax.ShapeDtypeStruct((bsz, ne, seq), F32),
        ],
        compiler_params=_cparams("parallel", "parallel"),
        name="merge",
    )(hy, na, zx, zx, x, mod3, g2.reshape(1, d), wbh, wbn, wo, wr_t)


def _route_kernel(lg_ref, tri_ref, rank_ref, rank_t_ref, gate_t_ref, *, cap):
    grp, ne, t = lg_ref.shape
    lg = lg_ref[...]
    e = jnp.exp(lg - jnp.max(lg, axis=1, keepdims=True))
    aff = (e / jnp.sum(e, axis=1, keepdims=True)).reshape(grp * ne, t)

    def bit_step(i, bits):
        cand = bits | (jnp.int32(1) << (30 - i))
        keep = jnp.sum((aff >= pltpu.bitcast(cand, F32)).astype(jnp.int32), axis=1, keepdims=True) >= cap
        return jnp.where(keep, cand, bits)

    thr = pltpu.bitcast(lax.fori_loop(0, 31, bit_step, jnp.zeros((grp * ne, 1), jnp.int32)), F32)
    above = aff > thr
    tie = aff == thr
    need = cap - jnp.sum(above.astype(jnp.int32), axis=1, keepdims=True)
    tri = tri_ref[...]
    tie_before = _dot(tie.astype(BF16), tri)
    sel = above | (tie & (tie_before < need.astype(F32)))
    sel_before = _dot(sel.astype(BF16), tri)
    rank = jnp.where(sel, sel_before, -1.0)
    gate = jnp.where(sel, aff, 0.0)
    rank_ref[...] = rank.astype(jnp.int32).reshape(grp, ne, t)
    pad_r = jnp.full((LANES - ne, t), -1.0, F32)
    pad_g = jnp.zeros((LANES - ne, t), F32)
    for g in range(grp):
        rows = slice(g * ne, (g + 1) * ne)
        rank_t_ref[g] = jnp.concatenate([rank[rows], pad_r], axis=0).T.astype(jnp.int32)
        gate_t_ref[g] = jnp.concatenate([gate[rows], pad_g], axis=0).T


def _route(logits_t, cap, grp):
    bsz, ne, t = logits_t.shape
    i = jnp.arange(t, dtype=jnp.int32)
    tri = (i[:, None] < i[None, :]).astype(BF16)
    return pl.pallas_call(
        functools.partial(_route_kernel, cap=cap),
        grid=(bsz // grp,),
        in_specs=[
            pl.BlockSpec((grp, ne, t), lambda b: (b, 0, 0)),
            _const_spec((t, t)),
        ],
        out_specs=[
            pl.BlockSpec((grp, ne, t), lambda b: (b, 0, 0)),
            pl.BlockSpec((grp, t, LANES), lambda b: (b, 0, 0)),
            pl.BlockSpec((grp, t, LANES), lambda b: (b, 0, 0)),
        ],
        out_shape=[
            jax.ShapeDtypeStruct((bsz, ne, t), jnp.int32),
            jax.ShapeDtypeStruct((bsz, t, LANES), jnp.int32),
            jax.ShapeDtypeStruct((bsz, t, LANES), F32),
        ],
        compiler_params=_cparams("parallel"),
        name="route",
    )(logits_t, tri)


def _expert_kernel(rank_ref, h_ref, wg_ref, wu_ref, wd_ref, y_ref, *, cap):
    e = pl.program_id(0)
    grp, t, _ = h_ref.shape
    slot = lax.broadcasted_iota(jnp.int32, (cap, t), 0)
    xe = []
    for b in range(grp):
        rank = rank_ref[b, pl.ds(e, 1), :]
        onehot = jnp.where(rank == slot, 1.0, 0.0).astype(BF16)
        xe.append(_dot(onehot, h_ref[b]).astype(BF16))
    xe = jnp.concatenate(xe, axis=0)
    g = _dot(xe, wg_ref[0])
    u = _dot(xe, wu_ref[0])
    act = (g * jax.nn.sigmoid(g) * u).astype(BF16)
    y = _dot(act, wd_ref[0])
    for b in range(grp):
        y_ref[b, 0] = y[b * cap:(b + 1) * cap].astype(y_ref.dtype)


def _experts(rank, h2, wg, wu, wd, cap, grp):
    bsz, t, d = h2.shape
    ne, _, f = wg.shape

    def wspec(shape, buffers=1):
        return pl.BlockSpec((1,) + shape, lambda e, b: (e, 0, 0), pipeline_mode=pl.Buffered(buffers))

    return pl.pallas_call(
        functools.partial(_expert_kernel, cap=cap),
        grid=(ne, bsz // grp),
        in_specs=[
            pl.BlockSpec((grp, ne, t), lambda e, b: (b, 0, 0)),
            pl.BlockSpec((grp, t, d), lambda e, b: (b, 0, 0)),
            wspec((d, f)), wspec((d, f)), wspec((f, d), buffers=2),
        ],
        out_specs=pl.BlockSpec((grp, 1, cap, d), lambda e, b: (b, e, 0, 0)),
        out_shape=jax.ShapeDtypeStruct((bsz, ne, cap, d), BF16),
        compiler_params=_cparams("arbitrary", "arbitrary"),
        name="experts",
    )(rank, h2, wg, wu, wd)


def _combine_kernel(rank_t_ref, gate_t_ref, y_ref, x1_ref, mod_ref, gf_ref, o_ref, *, cap):
    ne = y_ref.shape[1]
    tm = x1_ref.shape[1]
    slot = lax.broadcasted_iota(jnp.int32, (tm, cap), 1)
    acc = jnp.zeros(x1_ref.shape[1:], F32)
    for e in range(ne):
        r = rank_t_ref[0, :, e:e + 1]
        g = gate_t_ref[0, :, e:e + 1]
        scat = jnp.where(r == slot, g, 0.0).astype(BF16)
        acc = acc + _dot(scat, y_ref[0, e])
    x2 = x1_ref[0] + mod_ref[0, 5:6, :] * acc
    ms = jnp.mean(x2 * x2, axis=-1, keepdims=True)
    o_ref[0] = x2 * lax.rsqrt(ms + EPS) * gf_ref[...]


def _combine(rank_t, gate_t, y, x1, mod3, final_g, cap, tm):
    bsz, t, d = x1.shape
    ne = y.shape[1]
    return pl.pallas_call(
        functools.partial(_combine_kernel, cap=cap),
        grid=(bsz, t // tm),
        in_specs=[
            pl.BlockSpec((1, tm, LANES), lambda i, j: (i, j, 0)),
            pl.BlockSpec((1, tm, LANES), lambda i, j: (i, j, 0)),
            pl.BlockSpec((1, ne, cap, d), lambda i, j: (i, 0, 0, 0)),
            pl.BlockSpec((1, tm, d), lambda i, j: (i, j, 0)),
            pl.BlockSpec((1, 6, d), lambda i, j: (i, 0, 0)),
            _const_spec((1, d)),
        ],
        out_specs=pl.BlockSpec((1, tm, d), lambda i, j: (i, j, 0)),
        out_shape=jax.ShapeDtypeStruct((bsz, t, d), F32),
        compiler_params=_cparams("parallel", "arbitrary"),
        name="combine",
    )(rank_t, gate_t, y, x1, mod3, final_g.reshape(1, d))


def kernel(x, c, ctx, c_ctx, w_mod, b_mod, norm1_g, norm2_g, w_in, b_in, hy_short_w, hy_short_b, hy_skip, filt_w1, filt_b1, filt_w2, filt_b2, filt_w3, filt_freq, na_rpb, w_branch_hy, w_branch_na, w_out, w_router, w_gate, w_up, w_down, final_g):
    depth = w_mod.shape[0]
    bsz, seq, d = x.shape
    hy_width = w_branch_hy.shape[1]
    na_width = w_branch_na.shape[1]
    col_q = 3 * hy_width
    col_k = col_q + na_width
    col_g = col_q + 3 * na_width
    cap = EC_CAPACITY * seq // N_EXPERTS
    rows = seq // GRID_W
    mod_rows = -(-(bsz + 1) // 8) * 8

    hy_blk = HY_BLOCK
    cs = _dft_matrix(hy_blk)
    for i in range(depth):
        assert i == depth - 1, "only the final layer's data flow (context feeds keys/values only) is implemented"
        cc = jnp.zeros((mod_rows, d), F32).at[:bsz].set(c).at[bsz].set(c_ctx)
        mod3 = _modulation(cc, w_mod[i], b_mod[i]).reshape(mod_rows, 6, d)

        qscale = jnp.ones((w_in.shape[2],), F32).at[col_q:col_k].set(NA_HEAD_DIM ** -0.5 * LOG2_E)
        w_in_s = (w_in[i] * qscale).astype(BF16)
        b_in_s = b_in[i] * qscale
        zx = _in_proj(x, mod3, lambda b: b, norm1_g[i], w_in_s, b_in_s, tm=TOKEN_TILE)
        kvc = _in_proj(ctx.reshape(1, -1, d), mod3, lambda b: bsz, norm1_g[i], w_in_s[:, col_k:col_g],
                       b_in_s[col_k:col_g], tm=math.gcd(bsz * ctx.shape[1], TOKEN_TILE)).reshape(bsz, ctx.shape[1], -1)

        spec, nyq = _filter_spectra(seq, hy_width, filt_w1[i], filt_b1[i], filt_w2[i], filt_b2[i], filt_w3[i],
                                    filt_freq[i], hy_skip[i], cs, cb=HY_CHANNEL_BLOCK, blk=hy_blk)
        hy = _hyena(zx, hy_short_w[i], hy_short_b[i], spec, nyq, cs, hy_width, cb=HY_CHANNEL_BLOCK, blk=hy_blk)

        experts_w = (w_gate[i], w_up[i], w_down[i])
        if _cast_in_na(experts_w, bsz * (rows // NA_ROWS_PER_STEP)):
            na, experts_w = _na(zx, kvc, _na_bias_tables(na_rpb[i], rows), col_q,
                                [w.reshape(-1, w.shape[2]) for w in experts_w])
            experts_w = [w2.reshape(w.shape) for w2, w in zip(experts_w, (w_gate[i], w_up[i], w_down[i]))]
        else:
            na, _ = _na(zx, kvc, _na_bias_tables(na_rpb[i], rows), col_q, [])
            experts_w = [w.astype(BF16) for w in experts_w]

        x1, h2, logits_t = _merge(hy, na, zx, x, mod3, norm2_g[i], w_branch_hy[i].astype(BF16),
                                  w_branch_na[i].astype(BF16), w_out[i].astype(BF16), w_router[i].T, col_g, tm=TOKEN_TILE)
        rank, rank_t, gate_t = _route(logits_t, cap, grp=math.gcd(bsz, 8))
        y = _experts(rank, h2, *experts_w, cap, grp=math.gcd(bsz, 2))
        x = _combine(rank_t, gate_t, y, x1, mod3, final_g, cap, tm=TOKEN_TILE)
    return x
```

```python
import functools
import math

import jax
import jax.numpy as jnp
import numpy as np
from jax import lax
from jax.experimental import pallas as pl
from jax.experimental.pallas import tpu as pltpu

F32 = jnp.float32
BF16 = jnp.bfloat16

EPS = 1e-6
GRID_W = 64
HY_ORDER = 2
SHORT_CONV = 3
FILT_BANDS = 8
DECAY_TARGET = 1e-2
FAST_DECAY_PCT = 0.3
SLOW_DECAY_PCT = 1.5
HY_BLOCK = 512
NA_HEADS = 8
NA_HEAD_DIM = 64
NA_WIN_H = 8
NA_WIN_W = 16
NA_ROWS_PER_STEP = 4
NA_SLAB_ROWS = NA_ROWS_PER_STEP + NA_WIN_H
N_EXPERTS = 16
EC_CAPACITY = 2

MASK_VALUE = -1e30
LOG2_E = 1.4426950408889634
LANES = 128
BF16_SUBLANES = 16
VMEM_LIMIT = 56 * 1024 * 1024
EXPERT_VMEM_LIMIT = 60 * 1024 * 1024
TOKEN_TILE = 1024
PROJ_COL_CHUNK = 1024
MOD_COL_TILE = 1536
HY_CHANNEL_BLOCK = 256


def _cparams(*sem):
    return pltpu.CompilerParams(dimension_semantics=sem, vmem_limit_bytes=VMEM_LIMIT)


def _const_spec(shape):
    nd = len(shape)
    return pl.BlockSpec(shape, lambda *_: (0,) * nd, pipeline_mode=pl.Buffered(1))


def _dot(a, b):
    return jnp.dot(a, b, preferred_element_type=F32)


def _dot_nt(a, b):
    return lax.dot_general(a, b, (((1,), (1,)), ((), ())), preferred_element_type=F32)


def _mod_kernel(c_ref, w_ref, b_ref, o_ref):
    c = c_ref[...]
    s = c * jax.nn.sigmoid(c)
    o_ref[...] = _dot(s, w_ref[...]) + b_ref[...]


def _modulation(cc, w_mod, b_mod):
    rows, d = cc.shape
    n = w_mod.shape[1]
    tn = MOD_COL_TILE
    return pl.pallas_call(
        _mod_kernel,
        grid=(n // tn,),
        in_specs=[
            pl.BlockSpec((rows, d), lambda j: (0, 0)),
            pl.BlockSpec((d, tn), lambda j: (0, j)),
            pl.BlockSpec((1, tn), lambda j: (0, j)),
        ],
        out_specs=pl.BlockSpec((rows, tn), lambda j: (0, j)),
        out_shape=jax.ShapeDtypeStruct((rows, n), F32),
        compiler_params=_cparams("arbitrary"),
        name="modulation",
    )(cc, w_mod, b_mod.reshape(1, n))


def _rms_mod(x, g, shift, scale):
    ms = jnp.mean(x * x, axis=-1, keepdims=True)
    return (x * lax.rsqrt(ms + EPS) * g) * (1.0 + scale) + shift


def _in_proj_kernel(x_ref, mod_ref, g_ref, w_ref, b_ref, o_ref, *, n_chunk):
    h = _rms_mod(x_ref[0], g_ref[...], mod_ref[0, 0:1, :], mod_ref[0, 1:2, :]).astype(BF16)
    n = w_ref.shape[1]
    for j in range(0, n, n_chunk):
        z = _dot(h, w_ref[:, j:j + n_chunk]) + b_ref[:, j:j + n_chunk]
        o_ref[0, :, j:j + n_chunk] = z.astype(o_ref.dtype)


def _in_proj(x, mod3, mod_row, g, w, b, tm):
    bsz, t, d = x.shape
    n = w.shape[1]
    return pl.pallas_call(
        functools.partial(_in_proj_kernel, n_chunk=min(n, PROJ_COL_CHUNK)),
        grid=(bsz, t // tm),
        in_specs=[
            pl.BlockSpec((1, tm, d), lambda i, j: (i, j, 0)),
            pl.BlockSpec((1, 6, d), lambda i, j: (mod_row(i), 0, 0)),
            _const_spec((1, d)),
            _const_spec((d, n)),
            _const_spec((1, n)),
        ],
        out_specs=pl.BlockSpec((1, tm, n), lambda i, j: (i, j, 0)),
        out_shape=jax.ShapeDtypeStruct((bsz, t, n), BF16),
        compiler_params=_cparams("parallel", "parallel"),
        name="in_proj",
    )(x, mod3, g.reshape(1, d), w, b.reshape(1, n))


def _dft_matrix(blk):
    n = 2 * blk
    i = jnp.arange(blk, dtype=jnp.int32)
    ft = (i[:, None] * i[None, :]) % n
    ang = ft.astype(F32) * (2.0 * math.pi / n)
    return jnp.concatenate([jnp.cos(ang), -jnp.sin(ang)], axis=1).astype(BF16)


def _alt_sign(shape):
    row = lax.broadcasted_iota(jnp.int32, shape, 0)
    return (1 - 2 * (row & 1)).astype(F32)


def _filter_kernel(feats_ref, w1_ref, b1_ref, w2_ref, b2_ref, freq_ref, w3p_ref, w3f_ref, decay_ref, skip_ref,
                   cs_ref, spec_ref, nyq_ref, h_ref, k2_ref, ar_ref, ai_ref, *, blk):
    seq2, cb = k2_ref.shape
    seq = seq2 // 2
    nblk = seq2 // blk
    n = 2 * blk

    @pl.when((pl.program_id(0) == 0) & (pl.program_id(1) == 0))
    def _():
        freq = freq_ref[...]
        h1 = jnp.sin(freq * (_dot(feats_ref[...], w1_ref[...]) + b1_ref[...]))
        h_ref[...] = jnp.sin(freq * (_dot(h1, w2_ref[...]) + b2_ref[...]))

    h = h_ref[...]
    row = lax.broadcasted_iota(jnp.int32, (seq2, cb), 0)
    k2 = jnp.where(row < seq, _dot(h, w3f_ref[...]), _dot(h, w3p_ref[...])) * decay_ref[...]
    k2_ref[...] = jnp.where(row == 0, 0.0, k2)
    sign = _alt_sign((blk, cb))
    for d in range(nblk):
        a = k2_ref[d * blk:(d + 1) * blk, :]
        ab = a.astype(BF16)
        ar_ref[d] = _dot(cs_ref[:, 0:blk], ab)
        ai_ref[d] = _dot(cs_ref[:, blk:n], ab)
    lag0 = nblk // 2
    ar_ref[lag0] = ar_ref[lag0] + skip_ref[0]
    k2_ref[seq:seq + 1, :] = k2_ref[seq:seq + 1, :] + skip_ref[0]
    frow = lax.broadcasted_iota(jnp.int32, (blk, cb), 0)
    wgt = jnp.where(frow == 0, 1.0 / n, 2.0 / n)
    for d in range(1, nblk):
        a0 = k2_ref[(d - 1) * blk:(d - 1) * blk + 1, :]
        spec_ref[0, d - 1, 0] = ((ar_ref[d] + sign * (ar_ref[d - 1] - a0)) * wgt).astype(spec_ref.dtype)
        spec_ref[0, d - 1, 1] = ((ai_ref[d] + sign * ai_ref[d - 1]) * wgt).astype(spec_ref.dtype)
        cur = jnp.sum(sign * k2_ref[d * blk:(d + 1) * blk, :], axis=0, keepdims=True)
        prev = jnp.sum(sign * k2_ref[(d - 1) * blk:d * blk, :], axis=0, keepdims=True)
        nyq_ref[0, d - 1] = (cur + prev - a0) * (1.0 / n)


def _filter_spectra(seq, width, w1, b1, w2, b2, w3, freq, skip, cs, cb, blk):
    pos = jnp.abs(jnp.arange(2 * seq, dtype=F32) - seq)
    t = pos / max(seq - 1, 1)
    omega = 2.0 * math.pi * pos / seq
    bands = jnp.linspace(1e-4, FILT_BANDS - 1, FILT_BANDS, dtype=F32)
    ang = omega[:, None] * bands[None, :]
    feats = jnp.concatenate([t[:, None], jnp.cos(ang), -jnp.sin(ang)], axis=-1)
    emb, hid = w1.shape
    emb_pad = -(-emb // 8) * 8
    feats = jnp.pad(feats, ((0, 0), (0, emb_pad - emb)))
    w1 = jnp.pad(w1, ((0, emb_pad - emb), (0, 0)))
    max_decay = math.log(DECAY_TARGET) / FAST_DECAY_PCT
    min_decay = math.log(DECAY_TARGET) / SLOW_DECAY_PCT
    deltas = jnp.linspace(min_decay, max_decay, width, dtype=F32)
    decay = jnp.exp(-t[:, None] * jnp.abs(deltas)[None, :])
    ncb = width // cb
    nblk = 2 * seq // blk
    return pl.pallas_call(
        functools.partial(_filter_kernel, blk=blk),
        grid=(HY_ORDER, ncb),
        in_specs=[
            _const_spec((2 * seq, emb_pad)),
            _const_spec((emb_pad, hid)),
            _const_spec((1, hid)),
            _const_spec((hid, hid)),
            _const_spec((1, hid)),
            _const_spec((1, hid)),
            pl.BlockSpec((hid, cb), lambda o, c: (0, o * ncb + c)),
            pl.BlockSpec((hid, cb), lambda o, c: (0, (HY_ORDER + o) * ncb + c)),
            pl.BlockSpec((2 * seq, cb), lambda o, c: (0, c)),
            pl.BlockSpec((1, 1, cb), lambda o, c: (o, 0, c)),
            _const_spec((blk, 2 * blk)),
        ],
        out_specs=[
            pl.BlockSpec((1, nblk - 1, 2, blk, cb), lambda o, c: (o, 0, 0, 0, c)),
            pl.BlockSpec((1, nblk - 1, 1, cb), lambda o, c: (o, 0, 0, c)),
        ],
        out_shape=[
            jax.ShapeDtypeStruct((HY_ORDER, nblk - 1, 2, blk, width), BF16),
            jax.ShapeDtypeStruct((HY_ORDER, nblk - 1, 1, width), F32),
        ],
        scratch_shapes=[
            pltpu.VMEM((2 * seq, hid), F32),
            pltpu.VMEM((2 * seq, cb), F32),
            pltpu.VMEM((nblk, blk, cb), F32),
            pltpu.VMEM((nblk, blk, cb), F32),
        ],
        compiler_params=_cparams("arbitrary", "arbitrary"),
        name="hyena_filters",
    )(feats, w1, b1.reshape(1, hid), w2, b2.reshape(1, hid), freq.reshape(1, hid), w3, w3, decay,
      skip.reshape(HY_ORDER, 1, width), cs)


def _hyena_kernel(z_ref, sw_ref, sb_ref, spec_ref, nyq_ref, cs_ref, o_ref,
                  u_ref, g_ref, ub_ref, re_ref, im_ref, yri_ref, *, blk, cb):
    seq, width = o_ref.shape[1], o_ref.shape[2]
    nb = seq // blk
    sign = _alt_sign((blk, cb))

    def short_conv(dst_ref, g, cols):
        zc = slice(g * width + cols.start, g * width + cols.stop)
        z = z_ref[0, :, zc].astype(F32)
        w0, w1, w2 = sw_ref[g, 0:1, cols], sw_ref[g, 1:2, cols], sw_ref[g, 2:3, cols]
        dst_ref[...] = sb_ref[g, :, cols] + pltpu.roll(z, 1, 0) * w0 + z * w1 + pltpu.roll(z, seq - 1, 0) * w2
        dst_ref[0:1, :] = dst_ref[0:1, :] - z_ref[0, seq - 1:seq, zc].astype(F32) * w0
        dst_ref[seq - 1:seq, :] = dst_ref[seq - 1:seq, :] - z_ref[0, 0:1, zc].astype(F32) * w2

    def long_conv(o, g, cols):
        ub_ref[...] = u_ref[...].astype(BF16)
        nyq_in = []
        for j in range(nb):
            rows = slice(j * blk, (j + 1) * blk)
            re_ref[j] = _dot(cs_ref[:, 0:blk], ub_ref[rows, :]).astype(BF16)
            im_ref[j] = _dot(cs_ref[:, blk:2 * blk], ub_ref[rows, :]).astype(BF16)
            nyq_in.append(jnp.sum(sign * u_ref[rows, :], axis=0, keepdims=True))
        short_conv(g_ref, g, cols)
        for i in range(nb):
            yr = yi = nyq = None
            for j in range(nb):
                d = i - j + nb - 1
                gr, gi = spec_ref[o, d, 0, :, cols], spec_ref[o, d, 1, :, cols]
                re, im = re_ref[j], im_ref[j]
                tr = re * gr - im * gi
                ti = re * gi + im * gr
                tn = nyq_in[j] * nyq_ref[o, d, :, cols]
                yr, yi, nyq = (tr, ti, tn) if yr is None else (yr + tr, yi + ti, nyq + tn)
            yri_ref[0:blk, :] = yr
            yri_ref[blk:2 * blk, :] = yi
            rows = slice(i * blk, (i + 1) * blk)
            y = _dot(cs_ref[...], yri_ref[...]) + sign * nyq
            g_ref[rows, :] = g_ref[rows, :] * y
        u_ref[...] = g_ref[...]

    for c in range(0, width, cb):
        cols = slice(c, c + cb)
        short_conv(u_ref, 0, cols)
        long_conv(0, 1, cols)
        long_conv(1, 2, cols)
        o_ref[0, :, cols] = u_ref[...].astype(o_ref.dtype)


def _hyena(zx, short_w, short_b, spec, nyq, cs, width, cb, blk):
    bsz, seq, _ = zx.shape
    nb = seq // blk
    sw = short_w.reshape(SHORT_CONV, 3, width).transpose(1, 0, 2)
    sb = short_b.reshape(3, 1, width)
    return pl.pallas_call(
        functools.partial(_hyena_kernel, blk=blk, cb=cb),
        grid=(bsz,),
        in_specs=[
            pl.BlockSpec((1, seq, 3 * width), lambda i: (i, 0, 0)),
            _const_spec(sw.shape),
            _const_spec(sb.shape),
            _const_spec(spec.shape),
            _const_spec(nyq.shape),
            _const_spec((blk, 2 * blk)),
        ],
        out_specs=pl.BlockSpec((1, seq, width), lambda i: (i, 0, 0)),
        out_shape=jax.ShapeDtypeStruct((bsz, seq, width), BF16),
        scratch_shapes=[
            pltpu.VMEM((seq, cb), F32),
            pltpu.VMEM((seq, cb), F32),
            pltpu.VMEM((seq, cb), BF16),
            pltpu.VMEM((nb, blk, cb), BF16),
            pltpu.VMEM((nb, blk, cb), BF16),
            pltpu.VMEM((2 * blk, cb), BF16),
        ],
        compiler_params=_cparams("parallel"),
        name="hyena",
    )(zx, sw, sb, spec, nyq, cs)


def _na_col_bias_kernel(rpb_ref, sel_ref, mask_ref, rowmask_ref, o_ref):
    acc = jnp.zeros(o_ref.shape, F32) + mask_ref[...] + rowmask_ref[...]
    for ci in range(sel_ref.shape[0]):
        acc = acc + (rpb_ref[:, ci:ci + 1] * LOG2_E) * sel_ref[ci:ci + 1, :]
    o_ref[...] = acc


def _na_bias_tables(rpb, rows):
    rq, sl, half = NA_ROWS_PER_STEP, NA_SLAB_ROWS, NA_WIN_H // 2
    n_ri, n_ci = 2 * NA_WIN_H - 1, 2 * NA_WIN_W - 1
    qc = np.arange(GRID_W)
    kc = np.arange(GRID_W)
    ws = np.clip(qc - NA_WIN_W // 2, 0, GRID_W - NA_WIN_W)
    in_win = (kc[None, :] >= ws[:, None]) & (kc[None, :] < ws[:, None] + NA_WIN_W)
    ci = np.clip(kc[None, :] - qc[:, None] + NA_WIN_W - 1, 0, n_ci - 1)
    sel = ((ci[None] == np.arange(n_ci)[:, None, None]) & in_win[None]).astype(np.float32)
    sel = sel.reshape(n_ci, GRID_W * GRID_W)
    mask = np.where(in_win, 0.0, MASK_VALUE).astype(np.float32).reshape(1, GRID_W * GRID_W)
    rowmask = np.tile(np.where(np.arange(n_ri + 1) == n_ri, MASK_VALUE, 0.0).astype(np.float32), NA_HEADS)
    rpb_x = jnp.pad(rpb, ((0, 0), (0, 1), (0, 0))).reshape(NA_HEADS * (n_ri + 1), n_ci)
    col = pl.pallas_call(
        _na_col_bias_kernel,
        out_shape=jax.ShapeDtypeStruct((NA_HEADS * (n_ri + 1), GRID_W * GRID_W), F32),
        name="na_col_bias",
    )(rpb_x, jnp.asarray(sel), jnp.asarray(mask), jnp.asarray(rowmask.reshape(-1, 1)))
    col = col.reshape(NA_HEADS, n_ri + 1, GRID_W, GRID_W)

    n_steps = rows // rq
    steps = (0, 1, n_steps - 1)
    ri = np.full((len(steps), rq, sl), n_ri, np.int32)
    for c, step in enumerate(steps):
        start = int(np.clip(step * rq - half, 0, rows - sl))
        for i in range(rq):
            r = step * rq + i
            rs = int(np.clip(r - half, 0, rows - NA_WIN_H))
            for j in range(NA_WIN_H):
                ri[c, i, rs - start + j] = rs + j - r + NA_WIN_H - 1
    return pl.pallas_call(
        functools.partial(_na_bias_assemble_kernel, ri=ri),
        out_shape=jax.ShapeDtypeStruct((len(steps), NA_HEADS, rq * GRID_W, sl * GRID_W), F32),
        compiler_params=pltpu.CompilerParams(vmem_limit_bytes=VMEM_LIMIT),
        name="na_bias_tables",
    )(col)


def _na_bias_assemble_kernel(col_ref, o_ref, *, ri):
    n_cfg, rq, sl = ri.shape
    per_tile = LANES // GRID_W
    for c in range(n_cfg):
        for i in range(rq):
            for j in range(0, sl, per_tile):
                tile = jnp.concatenate([col_ref[:, int(ri[c, i, j + jj])] for jj in range(per_tile)], axis=-1)
                o_ref[c, :, i * GRID_W:(i + 1) * GRID_W, j * GRID_W:(j + per_tile) * GRID_W] = tile


def _na_kernel(q_ref, k_ref, v_ref, kvc_ref, bias_ref, *rest, rows):
    n_cast = (len(rest) - 1) // 2
    o_ref = rest[n_cast]
    for src, dst in zip(rest[:n_cast], rest[n_cast + 1:]):
        dst[...] = src[...].astype(dst.dtype)
    rq, sl = NA_ROWS_PER_STEP, NA_SLAB_ROWS
    width = NA_HEADS * NA_HEAD_DIM
    step = pl.program_id(1)
    start = jnp.clip(step * rq - NA_WIN_H // 2, 0, rows - sl)
    start = pl.multiple_of(start * GRID_W, GRID_W)
    nh = LANES // NA_HEAD_DIM
    nq = rq * GRID_W
    lane = lax.broadcasted_iota(jnp.int32, (nq, LANES), 1)
    owns = [(lane >= s * NA_HEAD_DIM) & (lane < (s + 1) * NA_HEAD_DIM) for s in range(nh)]

    for hp in range(width // LANES):
        cols = slice(hp * LANES, (hp + 1) * LANES)
        q2 = q_ref[0, :, cols]
        k2 = k_ref[0, pl.ds(start, sl * GRID_W), cols]
        v2 = v_ref[0, pl.ds(start, sl * GRID_W), cols]
        kc2 = kvc_ref[0, :, cols]
        vc2 = kvc_ref[0, :, width + hp * LANES:width + (hp + 1) * LANES]
        qm = jnp.concatenate([jnp.where(own, q2, jnp.zeros_like(q2)) for own in owns], axis=0)
        s_loc = _dot_nt(qm, k2) + bias_ref[0, hp * nh:(hp + 1) * nh].reshape(nh * nq, sl * GRID_W)
        s_ctx = _dot_nt(qm, kc2)
        m = jnp.maximum(jnp.max(s_loc, axis=-1, keepdims=True), jnp.max(s_ctx, axis=-1, keepdims=True))
        p_loc = jnp.exp2(s_loc - m)
        p_ctx = jnp.exp2(s_ctx - m)
        den = jnp.sum(p_loc, axis=-1, keepdims=True) + jnp.sum(p_ctx, axis=-1, keepdims=True)
        o = (_dot(p_loc.astype(BF16), v2) + _dot(p_ctx.astype(BF16), vc2)) / den
        o_ref[0, :, cols] = sum(jnp.where(owns[s], o[s * nq:(s + 1) * nq], 0.0) for s in range(nh)).astype(o_ref.dtype)


def _na(zx, kvc, bias, col_q, cast_2d):
    bsz, seq, _ = zx.shape
    rows = seq // GRID_W
    rq = NA_ROWS_PER_STEP
    assert rows % rq == 0 and rows >= NA_SLAB_ROWS and rows // rq >= 3
    n_steps = rows // rq
    width = NA_HEADS * NA_HEAD_DIM
    qb = col_q // width
    total = bsz * n_steps
    for w in cast_2d:
        assert w.shape[0] % (total * BF16_SUBLANES) == 0, (w.shape, total)

    def cfg(s):
        return jnp.minimum(s, 1) + jnp.maximum(s - (n_steps - 2), 0)

    def cast_spec(w):
        return pl.BlockSpec((w.shape[0] // total, w.shape[1]), lambda i, s: (i * n_steps + s, 0))

    outs = pl.pallas_call(
        functools.partial(_na_kernel, rows=rows),
        grid=(bsz, n_steps),
        in_specs=[
            pl.BlockSpec((1, rq * GRID_W, width), lambda i, s: (i, s, qb)),
            pl.BlockSpec((1, seq, width), lambda i, s: (i, 0, qb + 1)),
            pl.BlockSpec((1, seq, width), lambda i, s: (i, 0, qb + 2)),
            pl.BlockSpec((1,) + kvc.shape[1:], lambda i, s: (i, 0, 0)),
            pl.BlockSpec((1,) + bias.shape[1:], lambda i, s: (cfg(s), 0, 0, 0)),
        ] + [cast_spec(w) for w in cast_2d],
        out_specs=[pl.BlockSpec((1, rq * GRID_W, width), lambda i, s: (i, s, 0))] + [cast_spec(w) for w in cast_2d],
        out_shape=[jax.ShapeDtypeStruct((bsz, seq, width), BF16)]
        + [jax.ShapeDtypeStruct(w.shape, BF16) for w in cast_2d],
        compiler_params=_cparams("parallel", "arbitrary"),
        name="na",
    )(zx, zx, zx, kvc, bias, *cast_2d)
    return outs[0], outs[1:]


def _cast_in_na(weights, total_steps):
    max_block_bytes = 1 << 20
    for w in weights:
        rows = w.shape[0] * w.shape[1]
        if rows % (total_steps * BF16_SUBLANES) or rows // total_steps * w.shape[2] * 4 > max_block_bytes:
            return False
    return True


def _merge_kernel(hy_ref, na_ref, ghy_ref, gna_ref, x_ref, mod_ref, g2_ref, wbh_ref, wbn_ref, wo_ref, wr_ref,
                  x1_ref, h2_ref, lg_ref):
    a = _dot(hy_ref[0], wbh_ref[...])
    b = _dot(na_ref[0], wbn_ref[...])
    m = jax.nn.sigmoid(ghy_ref[0].astype(F32)) * a + jax.nn.sigmoid(gna_ref[0].astype(F32)) * b
    mix = _dot(m.astype(BF16), wo_ref[...])
    x1 = x_ref[0] + mod_ref[0, 2:3, :] * mix
    x1_ref[0] = x1
    h2 = _rms_mod(x1, g2_ref[...], mod_ref[0, 3:4, :], mod_ref[0, 4:5, :])
    h2_ref[0] = h2.astype(h2_ref.dtype)
    lg_ref[0] = _dot_nt(wr_ref[...], h2)


def _merge(hy, na, zx, x, mod3, g2, wbh, wbn, wo, wr_t, col_g, tm):
    bsz, seq, d = x.shape
    gb = col_g // d
    ne = wr_t.shape[0]
    return pl.pallas_call(
        _merge_kernel,
        grid=(bsz, seq // tm),
        in_specs=[
            pl.BlockSpec((1, tm, hy.shape[2]), lambda i, j: (i, j, 0)),
            pl.BlockSpec((1, tm, na.shape[2]), lambda i, j: (i, j, 0)),
            pl.BlockSpec((1, tm, d), lambda i, j: (i, j, gb)),
            pl.BlockSpec((1, tm, d), lambda i, j: (i, j, gb + 1)),
            pl.BlockSpec((1, tm, d), lambda i, j: (i, j, 0)),
            pl.BlockSpec((1, 6, d), lambda i, j: (i, 0, 0)),
            _const_spec((1, d)),
            _const_spec(wbh.shape),
            _const_spec(wbn.shape),
            _const_spec(wo.shape),
            _const_spec(wr_t.shape),
        ],
        out_specs=[
            pl.BlockSpec((1, tm, d), lambda i, j: (i, j, 0)),
            pl.BlockSpec((1, tm, d), lambda i, j: (i, j, 0)),
            pl.BlockSpec((1, ne, tm), lambda i, j: (i, 0, j)),
        ],
        out_shape=[
            jax.ShapeDtypeStruct((bsz, seq, d), F32),
            jax.ShapeDtypeStruct((bsz, seq, d), BF16),
            jax.ShapeDtypeStruct((bsz, ne, seq), F32),
        ],
        compiler_params=_cparams("parallel", "parallel"),
        name="merge",
    )(hy, na, zx, zx, x, mod3, g2.reshape(1, d), wbh, wbn, wo, wr_t)


def _route_kernel(lg_ref, tri_ref, rank_ref, rank_t_ref, gate_t_ref, *, cap):
    grp, ne, t = lg_ref.shape
    lg = lg_ref[...]
    e = jnp.exp(lg - jnp.max(lg, axis=1, keepdims=True))
    aff = (e / jnp.sum(e, axis=1, keepdims=True)).reshape(grp * ne, t)

    def bit_step(i, bits):
        cand = bits | (jnp.int32(1) << (30 - i))
        keep = jnp.sum((aff >= pltpu.bitcast(cand, F32)).astype(jnp.int32), axis=1, keepdims=True) >= cap
        return jnp.where(keep, cand, bits)

    thr = pltpu.bitcast(lax.fori_loop(0, 31, bit_step, jnp.zeros((grp * ne, 1), jnp.int32)), F32)
    above = aff > thr
    tie = aff == thr
    need = cap - jnp.sum(above.astype(jnp.int32), axis=1, keepdims=True)
    tri = tri_ref[...]
    tie_before = _dot(tie.astype(BF16), tri)
    sel = above | (tie & (tie_before < need.astype(F32)))
    sel_before = _dot(sel.astype(BF16), tri)
    rank = jnp.where(sel, sel_before, -1.0)
    gate = jnp.where(sel, aff, 0.0)
    rank_ref[...] = rank.astype(jnp.int32).reshape(grp, ne, t)
    pad_r = jnp.full((LANES - ne, t), -1.0, F32)
    pad_g = jnp.zeros((LANES - ne, t), F32)
    for g in range(grp):
        rows = slice(g * ne, (g + 1) * ne)
        rank_t_ref[g] = jnp.concatenate([rank[rows], pad_r], axis=0).T.astype(jnp.int32)
        gate_t_ref[g] = jnp.concatenate([gate[rows], pad_g], axis=0).T


def _route(logits_t, cap, grp):
    bsz, ne, t = logits_t.shape
    i = jnp.arange(t, dtype=jnp.int32)
    tri = (i[:, None] < i[None, :]).astype(BF16)
    return pl.pallas_call(
        functools.partial(_route_kernel, cap=cap),
        grid=(bsz // grp,),
        in_specs=[
            pl.BlockSpec((grp, ne, t), lambda b: (b, 0, 0)),
            _const_spec((t, t)),
        ],
        out_specs=[
            pl.BlockSpec((grp, ne, t), lambda b: (b, 0, 0)),
            pl.BlockSpec((grp, t, LANES), lambda b: (b, 0, 0)),
            pl.BlockSpec((grp, t, LANES), lambda b: (b, 0, 0)),
        ],
        out_shape=[
            jax.ShapeDtypeStruct((bsz, ne, t), jnp.int32),
            jax.ShapeDtypeStruct((bsz, t, LANES), jnp.int32),
            jax.ShapeDtypeStruct((bsz, t, LANES), F32),
        ],
        compiler_params=_cparams("parallel"),
        name="route",
    )(logits_t, tri)


def _expert_kernel(rank_ref, h_ref, wg_ref, wu_ref, wd_ref, y_ref, *, cap):
    e = pl.program_id(0)
    grp, t, _ = h_ref.shape
    slot = lax.broadcasted_iota(jnp.int32, (cap, t), 0)
    xe = []
    for b in range(grp):
        rank = rank_ref[b, pl.ds(e, 1), :]
        onehot = jnp.where(rank == slot, 1.0, 0.0).astype(BF16)
        xe.append(_dot(onehot, h_ref[b]).astype(BF16))
    xe = jnp.concatenate(xe, axis=0)
    g = _dot(xe, wg_ref[0])
    u = _dot(xe, wu_ref[0])
    act = (g * jax.nn.sigmoid(g) * u).astype(BF16)
    y = _dot(act, wd_ref[0])
    for b in range(grp):
        y_ref[b, 0] = y[b * cap:(b + 1) * cap].astype(y_ref.dtype)


def _experts(rank, h2, wg, wu, wd, cap, grp):
    bsz, t, d = h2.shape
    ne, _, f = wg.shape

    def wspec(shape):
        return pl.BlockSpec((1,) + shape, lambda e, b: (e, 0, 0))

    return pl.pallas_call(
        functools.partial(_expert_kernel, cap=cap),
        grid=(ne, bsz // grp),
        in_specs=[
            pl.BlockSpec((grp, ne, t), lambda e, b: (b, 0, 0)),
            pl.BlockSpec((grp, t, d), lambda e, b: (b, 0, 0)),
            wspec((d, f)), wspec((d, f)), wspec((f, d)),
        ],
        out_specs=pl.BlockSpec((grp, 1, cap, d), lambda e, b: (b, e, 0, 0)),
        out_shape=jax.ShapeDtypeStruct((bsz, ne, cap, d), BF16),
        compiler_params=pltpu.CompilerParams(dimension_semantics=("arbitrary", "arbitrary"),
                                             vmem_limit_bytes=EXPERT_VMEM_LIMIT),
        name="experts",
    )(rank, h2, wg, wu, wd)


def _combine_kernel(rank_t_ref, gate_t_ref, y_ref, x1_ref, mod_ref, gf_ref, o_ref, *, cap):
    ne = y_ref.shape[1]
    tm = x1_ref.shape[1]
    slot = lax.broadcasted_iota(jnp.int32, (tm, cap), 1)
    acc = jnp.zeros(x1_ref.shape[1:], F32)
    for e in range(ne):
        r = rank_t_ref[0, :, e:e + 1]
        g = gate_t_ref[0, :, e:e + 1]
        scat = jnp.where(r == slot, g, 0.0).astype(BF16)
        acc = acc + _dot(scat, y_ref[0, e])
    x2 = x1_ref[0] + mod_ref[0, 5:6, :] * acc
    ms = jnp.mean(x2 * x2, axis=-1, keepdims=True)
    o_ref[0] = x2 * lax.rsqrt(ms + EPS) * gf_ref[...]


def _combine(rank_t, gate_t, y, x1, mod3, final_g, cap, tm):
    bsz, t, d = x1.shape
    ne = y.shape[1]
    return pl.pallas_call(
        functools.partial(_combine_kernel, cap=cap),
        grid=(bsz, t // tm),
        in_specs=[
            pl.BlockSpec((1, tm, LANES), lambda i, j: (i, j, 0)),
            pl.BlockSpec((1, tm, LANES), lambda i, j: (i, j, 0)),
            pl.BlockSpec((1, ne, cap, d), lambda i, j: (i, 0, 0, 0)),
            pl.BlockSpec((1, tm, d), lambda i, j: (i, j, 0)),
            pl.BlockSpec((1, 6, d), lambda i, j: (i, 0, 0)),
            _const_spec((1, d)),
        ],
        out_specs=pl.BlockSpec((1, tm, d), lambda i, j: (i, j, 0)),
        out_shape=jax.ShapeDtypeStruct((bsz, t, d), F32),
        compiler_params=_cparams("parallel", "arbitrary"),
        name="combine",
    )(rank_t, gate_t, y, x1, mod3, final_g.reshape(1, d))


def kernel(x, c, ctx, c_ctx, w_mod, b_mod, norm1_g, norm2_g, w_in, b_in, hy_short_w, hy_short_b, hy_skip, filt_w1, filt_b1, filt_w2, filt_b2, filt_w3, filt_freq, na_rpb, w_branch_hy, w_branch_na, w_out, w_router, w_gate, w_up, w_down, final_g):
    depth = w_mod.shape[0]
    bsz, seq, d = x.shape
    hy_width = w_branch_hy.shape[1]
    na_width = w_branch_na.shape[1]
    col_q = 3 * hy_width
    col_k = col_q + na_width
    col_g = col_q + 3 * na_width
    cap = EC_CAPACITY * seq // N_EXPERTS
    rows = seq // GRID_W
    mod_rows = -(-(bsz + 1) // 8) * 8

    hy_blk = HY_BLOCK
    cs = _dft_matrix(hy_blk)
    for i in range(depth):
        assert i == depth - 1, "only the final layer's data flow (context feeds keys/values only) is implemented"
        cc = jnp.zeros((mod_rows, d), F32).at[:bsz].set(c).at[bsz].set(c_ctx)
        mod3 = _modulation(cc, w_mod[i], b_mod[i]).reshape(mod_rows, 6, d)

        qscale = jnp.ones((w_in.shape[2],), F32).at[col_q:col_k].set(NA_HEAD_DIM ** -0.5 * LOG2_E)
        w_in_s = (w_in[i] * qscale).astype(BF16)
        b_in_s = b_in[i] * qscale
        zx = _in_proj(x, mod3, lambda b: b, norm1_g[i], w_in_s, b_in_s, tm=TOKEN_TILE)
        kvc = _in_proj(ctx.reshape(1, -1, d), mod3, lambda b: bsz, norm1_g[i], w_in_s[:, col_k:col_g],
                       b_in_s[col_k:col_g], tm=math.gcd(bsz * ctx.shape[1], TOKEN_TILE)).reshape(bsz, ctx.shape[1], -1)

        spec, nyq = _filter_spectra(seq, hy_width, filt_w1[i], filt_b1[i], filt_w2[i], filt_b2[i], filt_w3[i],
                                    filt_freq[i], hy_skip[i], cs, cb=HY_CHANNEL_BLOCK, blk=hy_blk)
        hy = _hyena(zx, hy_short_w[i], hy_short_b[i], spec, nyq, cs, hy_width, cb=HY_CHANNEL_BLOCK, blk=hy_blk)

        experts_w = (w_gate[i], w_up[i], w_down[i])
        if _cast_in_na(experts_w, bsz * (rows // NA_ROWS_PER_STEP)):
            na, experts_w = _na(zx, kvc, _na_bias_tables(na_rpb[i], rows), col_q,
                                [w.reshape(-1, w.shape[2]) for w in experts_w])
            experts_w = [w2.reshape(w.shape) for w2, w in zip(experts_w, (w_gate[i], w_up[i], w_down[i]))]
        else:
            na, _ = _na(zx, kvc, _na_bias_tables(na_rpb[i], rows), col_q, [])
            experts_w = [w.astype(BF16) for w in experts_w]

        x1, h2, logits_t = _merge(hy, na, zx, x, mod3, norm2_g[i], w_branch_hy[i].astype(BF16),
                                  w_branch_na[i].astype(BF16), w_out[i].astype(BF16), w_router[i].T, col_g, tm=TOKEN_TILE)
        rank, rank_t, gate_t = _route(logits_t, cap, grp=math.gcd(bsz, 8))
        y = _experts(rank, h2, *experts_w, cap, grp=math.gcd(bsz, 2))
        x = _combine(rank_t, gate_t, y, x1, mod3, final_g, cap, tm=TOKEN_TILE)
    return x
```

```python
import functools
import math

import jax
import jax.numpy as jnp
import numpy as np
from jax import lax
from jax.experimental import pallas as pl
from jax.experimental.pallas import tpu as pltpu

F32 = jnp.float32
BF16 = jnp.bfloat16

EPS = 1e-6
GRID_W = 64
HY_ORDER = 2
SHORT_CONV = 3
FILT_BANDS = 8
DECAY_TARGET = 1e-2
FAST_DECAY_PCT = 0.3
SLOW_DECAY_PCT = 1.5
HY_BLOCK = 512
NA_HEADS = 8
NA_HEAD_DIM = 64
NA_WIN_H = 8
NA_WIN_W = 16
NA_ROWS_PER_STEP = 4
NA_SLAB_ROWS = NA_ROWS_PER_STEP + NA_WIN_H
N_EXPERTS = 16
EC_CAPACITY = 2

MASK_VALUE = -1e30
LOG2_E = 1.4426950408889634
LANES = 128
BF16_SUBLANES = 16
VMEM_LIMIT = 56 * 1024 * 1024
EXPERT_VMEM_LIMIT = 60 * 1024 * 1024
NA_VMEM_LIMIT = 60 * 1024 * 1024
TOKEN_TILE = 1024
PROJ_COL_CHUNK = 1024
MOD_COL_TILE = 1536
HY_CHANNEL_BLOCK = 256


def _cparams(*sem):
    return pltpu.CompilerParams(dimension_semantics=sem, vmem_limit_bytes=VMEM_LIMIT)


def _const_spec(shape):
    nd = len(shape)
    return pl.BlockSpec(shape, lambda *_: (0,) * nd, pipeline_mode=pl.Buffered(1))


def _dot(a, b):
    return jnp.dot(a, b, preferred_element_type=F32)


def _dot_nt(a, b):
    return lax.dot_general(a, b, (((1,), (1,)), ((), ())), preferred_element_type=F32)


def _mod_kernel(c_ref, w_ref, b_ref, o_ref):
    c = c_ref[...]
    s = c * jax.nn.sigmoid(c)
    o_ref[...] = _dot(s, w_ref[...]) + b_ref[...]


def _modulation(cc, w_mod, b_mod):
    rows, d = cc.shape
    n = w_mod.shape[1]
    tn = MOD_COL_TILE
    return pl.pallas_call(
        _mod_kernel,
        grid=(n // tn,),
        in_specs=[
            pl.BlockSpec((rows, d), lambda j: (0, 0)),
            pl.BlockSpec((d, tn), lambda j: (0, j)),
            pl.BlockSpec((1, tn), lambda j: (0, j)),
        ],
        out_specs=pl.BlockSpec((rows, tn), lambda j: (0, j)),
        out_shape=jax.ShapeDtypeStruct((rows, n), F32),
        compiler_params=_cparams("arbitrary"),
        name="modulation",
    )(cc, w_mod, b_mod.reshape(1, n))


def _rms_mod(x, g, shift, scale):
    ms = jnp.mean(x * x, axis=-1, keepdims=True)
    return (x * lax.rsqrt(ms + EPS) * g) * (1.0 + scale) + shift


def _in_proj_kernel(x_ref, mod_ref, g_ref, w_ref, b_ref, o_ref, *, n_chunk):
    h = _rms_mod(x_ref[0], g_ref[...], mod_ref[0, 0:1, :], mod_ref[0, 1:2, :]).astype(BF16)
    n = w_ref.shape[1]
    for j in range(0, n, n_chunk):
        z = _dot(h, w_ref[:, j:j + n_chunk]) + b_ref[:, j:j + n_chunk]
        o_ref[0, :, j:j + n_chunk] = z.astype(o_ref.dtype)


def _in_proj(x, mod3, mod_row, g, w, b, tm):
    bsz, t, d = x.shape
    n = w.shape[1]
    return pl.pallas_call(
        functools.partial(_in_proj_kernel, n_chunk=min(n, PROJ_COL_CHUNK)),
        grid=(bsz, t // tm),
        in_specs=[
            pl.BlockSpec((1, tm, d), lambda i, j: (i, j, 0)),
            pl.BlockSpec((1, 6, d), lambda i, j: (mod_row(i), 0, 0)),
            _const_spec((1, d)),
            _const_spec((d, n)),
            _const_spec((1, n)),
        ],
        out_specs=pl.BlockSpec((1, tm, n), lambda i, j: (i, j, 0)),
        out_shape=jax.ShapeDtypeStruct((bsz, t, n), BF16),
        compiler_params=_cparams("parallel", "parallel"),
        name="in_proj",
    )(x, mod3, g.reshape(1, d), w, b.reshape(1, n))


def _dft_matrix(blk):
    n = 2 * blk
    i = jnp.arange(blk, dtype=jnp.int32)
    ft = (i[:, None] * i[None, :]) % n
    ang = ft.astype(F32) * (2.0 * math.pi / n)
    return jnp.concatenate([jnp.cos(ang), -jnp.sin(ang)], axis=1).astype(BF16)


def _alt_sign(shape):
    row = lax.broadcasted_iota(jnp.int32, shape, 0)
    return (1 - 2 * (row & 1)).astype(F32)


def _filter_kernel(feats_ref, w1_ref, b1_ref, w2_ref, b2_ref, freq_ref, w3p_ref, w3f_ref, decay_ref, skip_ref,
                   cs_ref, spec_ref, nyq_ref, h_ref, k2_ref, ar_ref, ai_ref, *, blk):
    seq2, cb = k2_ref.shape
    seq = seq2 // 2
    nblk = seq2 // blk
    n = 2 * blk

    @pl.when((pl.program_id(0) == 0) & (pl.program_id(1) == 0))
    def _():
        freq = freq_ref[...]
        h1 = jnp.sin(freq * (_dot(feats_ref[...], w1_ref[...]) + b1_ref[...]))
        h_ref[...] = jnp.sin(freq * (_dot(h1, w2_ref[...]) + b2_ref[...]))

    h = h_ref[...]
    row = lax.broadcasted_iota(jnp.int32, (seq2, cb), 0)
    k2 = jnp.where(row < seq, _dot(h, w3f_ref[...]), _dot(h, w3p_ref[...])) * decay_ref[...]
    k2_ref[...] = jnp.where(row == 0, 0.0, k2)
    sign = _alt_sign((blk, cb))
    for d in range(nblk):
        a = k2_ref[d * blk:(d + 1) * blk, :]
        ab = a.astype(BF16)
        ar_ref[d] = _dot(cs_ref[:, 0:blk], ab)
        ai_ref[d] = _dot(cs_ref[:, blk:n], ab)
    lag0 = nblk // 2
    ar_ref[lag0] = ar_ref[lag0] + skip_ref[0]
    k2_ref[seq:seq + 1, :] = k2_ref[seq:seq + 1, :] + skip_ref[0]
    frow = lax.broadcasted_iota(jnp.int32, (blk, cb), 0)
    wgt = jnp.where(frow == 0, 1.0 / n, 2.0 / n)
    for d in range(1, nblk):
        a0 = k2_ref[(d - 1) * blk:(d - 1) * blk + 1, :]
        spec_ref[0, d - 1, 0] = ((ar_ref[d] + sign * (ar_ref[d - 1] - a0)) * wgt).astype(spec_ref.dtype)
        spec_ref[0, d - 1, 1] = ((ai_ref[d] + sign * ai_ref[d - 1]) * wgt).astype(spec_ref.dtype)
        cur = jnp.sum(sign * k2_ref[d * blk:(d + 1) * blk, :], axis=0, keepdims=True)
        prev = jnp.sum(sign * k2_ref[(d - 1) * blk:d * blk, :], axis=0, keepdims=True)
        nyq_ref[0, d - 1] = (cur + prev - a0) * (1.0 / n)


def _filter_spectra(seq, width, w1, b1, w2, b2, w3, freq, skip, cs, cb, blk):
    pos = jnp.abs(jnp.arange(2 * seq, dtype=F32) - seq)
    t = pos / max(seq - 1, 1)
    omega = 2.0 * math.pi * pos / seq
    bands = jnp.linspace(1e-4, FILT_BANDS - 1, FILT_BANDS, dtype=F32)
    ang = omega[:, None] * bands[None, :]
    feats = jnp.concatenate([t[:, None], jnp.cos(ang), -jnp.sin(ang)], axis=-1)
    emb, hid = w1.shape
    emb_pad = -(-emb // 8) * 8
    feats = jnp.pad(feats, ((0, 0), (0, emb_pad - emb)))
    w1 = jnp.pad(w1, ((0, emb_pad - emb), (0, 0)))
    max_decay = math.log(DECAY_TARGET) / FAST_DECAY_PCT
    min_decay = math.log(DECAY_TARGET) / SLOW_DECAY_PCT
    deltas = jnp.linspace(min_decay, max_decay, width, dtype=F32)
    decay = jnp.exp(-t[:, None] * jnp.abs(deltas)[None, :])
    ncb = width // cb
    nblk = 2 * seq // blk
    return pl.pallas_call(
        functools.partial(_filter_kernel, blk=blk),
        grid=(HY_ORDER, ncb),
        in_specs=[
            _const_spec((2 * seq, emb_pad)),
            _const_spec((emb_pad, hid)),
            _const_spec((1, hid)),
            _const_spec((hid, hid)),
            _const_spec((1, hid)),
            _const_spec((1, hid)),
            pl.BlockSpec((hid, cb), lambda o, c: (0, o * ncb + c)),
            pl.BlockSpec((hid, cb), lambda o, c: (0, (HY_ORDER + o) * ncb + c)),
            pl.BlockSpec((2 * seq, cb), lambda o, c: (0, c)),
            pl.BlockSpec((1, 1, cb), lambda o, c: (o, 0, c)),
            _const_spec((blk, 2 * blk)),
        ],
        out_specs=[
            pl.BlockSpec((1, nblk - 1, 2, blk, cb), lambda o, c: (o, 0, 0, 0, c)),
            pl.BlockSpec((1, nblk - 1, 1, cb), lambda o, c: (o, 0, 0, c)),
        ],
        out_shape=[
            jax.ShapeDtypeStruct((HY_ORDER, nblk - 1, 2, blk, width), BF16),
            jax.ShapeDtypeStruct((HY_ORDER, nblk - 1, 1, width), F32),
        ],
        scratch_shapes=[
            pltpu.VMEM((2 * seq, hid), F32),
            pltpu.VMEM((2 * seq, cb), F32),
            pltpu.VMEM((nblk, blk, cb), F32),
            pltpu.VMEM((nblk, blk, cb), F32),
        ],
        compiler_params=_cparams("arbitrary", "arbitrary"),
        name="hyena_filters",
    )(feats, w1, b1.reshape(1, hid), w2, b2.reshape(1, hid), freq.reshape(1, hid), w3, w3, decay,
      skip.reshape(HY_ORDER, 1, width), cs)


def _hyena_kernel(z_ref, sw_ref, sb_ref, spec_ref, nyq_ref, cs_ref, o_ref,
                  u_ref, g_ref, ub_ref, re_ref, im_ref, yri_ref, *, blk, cb):
    seq, width = o_ref.shape[1], o_ref.shape[2]
    nb = seq // blk
    sign = _alt_sign((blk, cb))

    def short_conv(dst_ref, g, cols):
        zc = slice(g * width + cols.start, g * width + cols.stop)
        z = z_ref[0, :, zc].astype(F32)
        w0, w1, w2 = sw_ref[g, 0:1, cols], sw_ref[g, 1:2, cols], sw_ref[g, 2:3, cols]
        dst_ref[...] = sb_ref[g, :, cols] + pltpu.roll(z, 1, 0) * w0 + z * w1 + pltpu.roll(z, seq - 1, 0) * w2
        dst_ref[0:1, :] = dst_ref[0:1, :] - z_ref[0, seq - 1:seq, zc].astype(F32) * w0
        dst_ref[seq - 1:seq, :] = dst_ref[seq - 1:seq, :] - z_ref[0, 0:1, zc].astype(F32) * w2

    def long_conv(o, g, cols):
        ub_ref[...] = u_ref[...].astype(BF16)
        nyq_in = []
        for j in range(nb):
            rows = slice(j * blk, (j + 1) * blk)
            re_ref[j] = _dot(cs_ref[:, 0:blk], ub_ref[rows, :]).astype(BF16)
            im_ref[j] = _dot(cs_ref[:, blk:2 * blk], ub_ref[rows, :]).astype(BF16)
            nyq_in.append(jnp.sum(sign * u_ref[rows, :], axis=0, keepdims=True))
        short_conv(g_ref, g, cols)
        for i in range(nb):
            yr = yi = nyq = None
            for j in range(nb):
                d = i - j + nb - 1
                gr, gi = spec_ref[o, d, 0, :, cols], spec_ref[o, d, 1, :, cols]
                re, im = re_ref[j], im_ref[j]
                tr = re * gr - im * gi
                ti = re * gi + im * gr
                tn = nyq_in[j] * nyq_ref[o, d, :, cols]
                yr, yi, nyq = (tr, ti, tn) if yr is None else (yr + tr, yi + ti, nyq + tn)
            yri_ref[0:blk, :] = yr
            yri_ref[blk:2 * blk, :] = yi
            rows = slice(i * blk, (i + 1) * blk)
            y = _dot(cs_ref[...], yri_ref[...]) + sign * nyq
            g_ref[rows, :] = g_ref[rows, :] * y
        u_ref[...] = g_ref[...]

    for c in range(0, width, cb):
        cols = slice(c, c + cb)
        short_conv(u_ref, 0, cols)
        long_conv(0, 1, cols)
        long_conv(1, 2, cols)
        o_ref[0, :, cols] = u_ref[...].astype(o_ref.dtype)


def _hyena(zx, short_w, short_b, spec, nyq, cs, width, cb, blk):
    bsz, seq, _ = zx.shape
    nb = seq // blk
    sw = short_w.reshape(SHORT_CONV, 3, width).transpose(1, 0, 2)
    sb = short_b.reshape(3, 1, width)
    return pl.pallas_call(
        functools.partial(_hyena_kernel, blk=blk, cb=cb),
        grid=(bsz,),
        in_specs=[
            pl.BlockSpec((1, seq, 3 * width), lambda i: (i, 0, 0)),
            _const_spec(sw.shape),
            _const_spec(sb.shape),
            _const_spec(spec.shape),
            _const_spec(nyq.shape),
            _const_spec((blk, 2 * blk)),
        ],
        out_specs=pl.BlockSpec((1, seq, width), lambda i: (i, 0, 0)),
        out_shape=jax.ShapeDtypeStruct((bsz, seq, width), BF16),
        scratch_shapes=[
            pltpu.VMEM((seq, cb), F32),
            pltpu.VMEM((seq, cb), F32),
            pltpu.VMEM((seq, cb), BF16),
            pltpu.VMEM((nb, blk, cb), BF16),
            pltpu.VMEM((nb, blk, cb), BF16),
            pltpu.VMEM((2 * blk, cb), BF16),
        ],
        compiler_params=_cparams("parallel"),
        name="hyena",
    )(zx, sw, sb, spec, nyq, cs)


def _na_col_bias_kernel(rpb_ref, sel_ref, mask_ref, rowmask_ref, o_ref):
    acc = jnp.zeros(o_ref.shape, F32) + mask_ref[...] + rowmask_ref[...]
    for ci in range(sel_ref.shape[0]):
        acc = acc + (rpb_ref[:, ci:ci + 1] * LOG2_E) * sel_ref[ci:ci + 1, :]
    o_ref[...] = acc


def _na_bias_tables(rpb, rows):
    rq, sl, half = NA_ROWS_PER_STEP, NA_SLAB_ROWS, NA_WIN_H // 2
    n_ri, n_ci = 2 * NA_WIN_H - 1, 2 * NA_WIN_W - 1
    qc = np.arange(GRID_W)
    kc = np.arange(GRID_W)
    ws = np.clip(qc - NA_WIN_W // 2, 0, GRID_W - NA_WIN_W)
    in_win = (kc[None, :] >= ws[:, None]) & (kc[None, :] < ws[:, None] + NA_WIN_W)
    ci = np.clip(kc[None, :] - qc[:, None] + NA_WIN_W - 1, 0, n_ci - 1)
    sel = ((ci[None] == np.arange(n_ci)[:, None, None]) & in_win[None]).astype(np.float32)
    sel = sel.reshape(n_ci, GRID_W * GRID_W)
    mask = np.where(in_win, 0.0, MASK_VALUE).astype(np.float32).reshape(1, GRID_W * GRID_W)
    rowmask = np.tile(np.where(np.arange(n_ri + 1) == n_ri, MASK_VALUE, 0.0).astype(np.float32), NA_HEADS)
    rpb_x = jnp.pad(rpb, ((0, 0), (0, 1), (0, 0))).reshape(NA_HEADS * (n_ri + 1), n_ci)
    col = pl.pallas_call(
        _na_col_bias_kernel,
        out_shape=jax.ShapeDtypeStruct((NA_HEADS * (n_ri + 1), GRID_W * GRID_W), F32),
        name="na_col_bias",
    )(rpb_x, jnp.asarray(sel), jnp.asarray(mask), jnp.asarray(rowmask.reshape(-1, 1)))
    col = col.reshape(NA_HEADS, n_ri + 1, GRID_W, GRID_W)

    n_steps = rows // rq
    steps = (0, 1, n_steps - 1)
    ri = np.full((len(steps), rq, sl), n_ri, np.int32)
    for c, step in enumerate(steps):
        start = int(np.clip(step * rq - half, 0, rows - sl))
        for i in range(rq):
            r = step * rq + i
            rs = int(np.clip(r - half, 0, rows - NA_WIN_H))
            for j in range(NA_WIN_H):
                ri[c, i, rs - start + j] = rs + j - r + NA_WIN_H - 1
    return pl.pallas_call(
        functools.partial(_na_bias_assemble_kernel, ri=ri),
        out_shape=jax.ShapeDtypeStruct((len(steps), NA_HEADS, rq * GRID_W, sl * GRID_W), F32),
        compiler_params=pltpu.CompilerParams(vmem_limit_bytes=VMEM_LIMIT),
        name="na_bias_tables",
    )(col)


def _na_bias_assemble_kernel(col_ref, o_ref, *, ri):
    n_cfg, rq, sl = ri.shape
    per_tile = LANES // GRID_W
    for c in range(n_cfg):
        for i in range(rq):
            for j in range(0, sl, per_tile):
                tile = jnp.concatenate([col_ref[:, int(ri[c, i, j + jj])] for jj in range(per_tile)], axis=-1)
                o_ref[c, :, i * GRID_W:(i + 1) * GRID_W, j * GRID_W:(j + per_tile) * GRID_W] = tile


def _na_kernel(q_ref, k_ref, v_ref, kvc_ref, bias_ref, *rest, rows):
    n_cast = (len(rest) - 1) // 2
    o_ref = rest[n_cast]
    for src, dst in zip(rest[:n_cast], rest[n_cast + 1:]):
        dst[...] = src[...].astype(dst.dtype)
    rq, sl = NA_ROWS_PER_STEP, NA_SLAB_ROWS
    width = NA_HEADS * NA_HEAD_DIM
    step = pl.program_id(1)
    start = jnp.clip(step * rq - NA_WIN_H // 2, 0, rows - sl)
    start = pl.multiple_of(start * GRID_W, GRID_W)
    nh = LANES // NA_HEAD_DIM
    nq = rq * GRID_W
    lane = lax.broadcasted_iota(jnp.int32, (nq, LANES), 1)
    owns = [(lane >= s * NA_HEAD_DIM) & (lane < (s + 1) * NA_HEAD_DIM) for s in range(nh)]

    for smp, hp in [(a, b) for a in range(q_ref.shape[0]) for b in range(width // LANES)]:
        cols = slice(hp * LANES, (hp + 1) * LANES)
        q2 = q_ref[smp, :, cols]
        k2 = k_ref[smp, pl.ds(start, sl * GRID_W), cols]
        v2 = v_ref[smp, pl.ds(start, sl * GRID_W), cols]
        kc2 = kvc_ref[smp, :, cols]
        vc2 = kvc_ref[smp, :, width + hp * LANES:width + (hp + 1) * LANES]
        qm = jnp.concatenate([jnp.where(own, q2, jnp.zeros_like(q2)) for own in owns], axis=0)
        s_loc = _dot_nt(qm, k2) + bias_ref[0, hp * nh:(hp + 1) * nh].reshape(nh * nq, sl * GRID_W)
        s_ctx = _dot_nt(qm, kc2)
        m = jnp.maximum(jnp.max(s_loc, axis=-1, keepdims=True), jnp.max(s_ctx, axis=-1, keepdims=True))
        p_loc = jnp.exp2(s_loc - m)
        p_ctx = jnp.exp2(s_ctx - m)
        den = jnp.sum(p_loc, axis=-1, keepdims=True) + jnp.sum(p_ctx, axis=-1, keepdims=True)
        o = (_dot(p_loc.astype(BF16), v2) + _dot(p_ctx.astype(BF16), vc2)) / den
        o_ref[smp, :, cols] = sum(jnp.where(owns[s], o[s * nq:(s + 1) * nq], 0.0) for s in range(nh)).astype(o_ref.dtype)


def _na(zx, kvc, bias, col_q, cast_2d, grp):
    bsz, seq, _ = zx.shape
    rows = seq // GRID_W
    rq = NA_ROWS_PER_STEP
    assert rows % rq == 0 and rows >= NA_SLAB_ROWS and rows // rq >= 3
    n_steps = rows // rq
    width = NA_HEADS * NA_HEAD_DIM
    qb = col_q // width
    total = bsz // grp * n_steps
    for w in cast_2d:
        assert w.shape[0] % (total * BF16_SUBLANES) == 0, (w.shape, total)

    def cfg(s):
        return jnp.minimum(s, 1) + jnp.maximum(s - (n_steps - 2), 0)

    def cast_spec(w):
        return pl.BlockSpec((w.shape[0] // total, w.shape[1]), lambda i, s: (i * n_steps + s, 0))

    outs = pl.pallas_call(
        functools.partial(_na_kernel, rows=rows),
        grid=(bsz // grp, n_steps),
        in_specs=[
            pl.BlockSpec((grp, rq * GRID_W, width), lambda i, s: (i, s, qb)),
            pl.BlockSpec((grp, seq, width), lambda i, s: (i, 0, qb + 1)),
            pl.BlockSpec((grp, seq, width), lambda i, s: (i, 0, qb + 2)),
            pl.BlockSpec((grp,) + kvc.shape[1:], lambda i, s: (i, 0, 0)),
            pl.BlockSpec((1,) + bias.shape[1:], lambda i, s: (cfg(s), 0, 0, 0)),
        ] + [cast_spec(w) for w in cast_2d],
        out_specs=[pl.BlockSpec((grp, rq * GRID_W, width), lambda i, s: (i, s, 0))] + [cast_spec(w) for w in cast_2d],
        out_shape=[jax.ShapeDtypeStruct((bsz, seq, width), BF16)]
        + [jax.ShapeDtypeStruct(w.shape, BF16) for w in cast_2d],
        compiler_params=pltpu.CompilerParams(dimension_semantics=("parallel", "arbitrary"),
                                             vmem_limit_bytes=NA_VMEM_LIMIT),
        name="na",
    )(zx, zx, zx, kvc, bias, *cast_2d)
    return outs[0], outs[1:]


def _cast_in_na(weights, total_steps):
    max_block_bytes = 2 << 20
    for w in weights:
        rows = w.shape[0] * w.shape[1]
        if rows % (total_steps * BF16_SUBLANES) or rows // total_steps * w.shape[2] * 4 > max_block_bytes:
            return False
    return True


def _merge_kernel(hy_ref, na_ref, ghy_ref, gna_ref, x_ref, mod_ref, g2_ref, wbh_ref, wbn_ref, wo_ref, wr_ref,
                  x1_ref, h2_ref, lg_ref):
    a = _dot(hy_ref[0], wbh_ref[...])
    b = _dot(na_ref[0], wbn_ref[...])
    m = jax.nn.sigmoid(ghy_ref[0].astype(F32)) * a + jax.nn.sigmoid(gna_ref[0].astype(F32)) * b
    mix = _dot(m.astype(BF16), wo_ref[...])
    x1 = x_ref[0] + mod_ref[0, 2:3, :] * mix
    x1_ref[0] = x1
    h2 = _rms_mod(x1, g2_ref[...], mod_ref[0, 3:4, :], mod_ref[0, 4:5, :])
    h2_ref[0] = h2.astype(h2_ref.dtype)
    lg_ref[0] = _dot_nt(wr_ref[...], h2)


def _merge(hy, na, zx, x, mod3, g2, wbh, wbn, wo, wr_t, col_g, tm):
    bsz, seq, d = x.shape
    gb = col_g // d
    ne = wr_t.shape[0]
    return pl.pallas_call(
        _merge_kernel,
        grid=(bsz, seq // tm),
        in_specs=[
            pl.BlockSpec((1, tm, hy.shape[2]), lambda i, j: (i, j, 0)),
            pl.BlockSpec((1, tm, na.shape[2]), lambda i, j: (i, j, 0)),
            pl.BlockSpec((1, tm, d), lambda i, j: (i, j, gb)),
            pl.BlockSpec((1, tm, d), lambda i, j: (i, j, gb + 1)),
            pl.BlockSpec((1, tm, d), lambda i, j: (i, j, 0)),
            pl.BlockSpec((1, 6, d), lambda i, j: (i, 0, 0)),
            _const_spec((1, d)),
            _const_spec(wbh.shape),
            _const_spec(wbn.shape),
            _const_spec(wo.shape),
            _const_spec(wr_t.shape),
        ],
        out_specs=[
            pl.BlockSpec((1, tm, d), lambda i, j: (i, j, 0)),
            pl.BlockSpec((1, tm, d), lambda i, j: (i, j, 0)),
            pl.BlockSpec((1, ne, tm), lambda i, j: (i, 0, j)),
        ],
        out_shape=[
            jax.ShapeDtypeStruct((bsz, seq, d), F32),
            jax.ShapeDtypeStruct((bsz, seq, d), BF16),
            jax.ShapeDtypeStruct((bsz, ne, seq), F32),
        ],
        compiler_params=_cparams("parallel", "parallel"),
        name="merge",
    )(hy, na, zx, zx, x, mod3, g2.reshape(1, d), wbh, wbn, wo, wr_t)


def _route_kernel(lg_ref, tri_ref, rank_ref, rank_t_ref, gate_t_ref, *, cap):
    grp, ne, t = lg_ref.shape
    lg = lg_ref[...]
    e = jnp.exp(lg - jnp.max(lg, axis=1, keepdims=True))
    aff = (e / jnp.sum(e, axis=1, keepdims=True)).reshape(grp * ne, t)

    def bit_step(i, bits):
        cand = bits | (jnp.int32(1) << (30 - i))
        keep = jnp.sum((aff >= pltpu.bitcast(cand, F32)).astype(jnp.int32), axis=1, keepdims=True) >= cap
        return jnp.where(keep, cand, bits)

    thr = pltpu.bitcast(lax.fori_loop(0, 31, bit_step, jnp.zeros((grp * ne, 1), jnp.int32)), F32)
    above = aff > thr
    tie = aff == thr
    need = cap - jnp.sum(above.astype(jnp.int32), axis=1, keepdims=True)
    tri = tri_ref[...]
    tie_before = _dot(tie.astype(BF16), tri)
    sel = above | (tie & (tie_before < need.astype(F32)))
    sel_before = _dot(sel.astype(BF16), tri)
    rank = jnp.where(sel, sel_before, -1.0)
    gate = jnp.where(sel, aff, 0.0)
    rank_ref[...] = rank.astype(jnp.int32).reshape(grp, ne, t)
    pad_r = jnp.full((LANES - ne, t), -1.0, F32)
    pad_g = jnp.zeros((LANES - ne, t), F32)
    for g in range(grp):
        rows = slice(g * ne, (g + 1) * ne)
        rank_t_ref[g] = jnp.concatenate([rank[rows], pad_r], axis=0).T.astype(jnp.int32)
        gate_t_ref[g] = jnp.concatenate([gate[rows], pad_g], axis=0).T


def _route(logits_t, cap, grp):
    bsz, ne, t = logits_t.shape
    i = jnp.arange(t, dtype=jnp.int32)
    tri = (i[:, None] < i[None, :]).astype(BF16)
    return pl.pallas_call(
        functools.partial(_route_kernel, cap=cap),
        grid=(bsz // grp,),
        in_specs=[
            pl.BlockSpec((grp, ne, t), lambda b: (b, 0, 0)),
            _const_spec((t, t)),
        ],
        out_specs=[
            pl.BlockSpec((grp, ne, t), lambda b: (b, 0, 0)),
            pl.BlockSpec((grp, t, LANES), lambda b: (b, 0, 0)),
            pl.BlockSpec((grp, t, LANES), lambda b: (b, 0, 0)),
        ],
        out_shape=[
            jax.ShapeDtypeStruct((bsz, ne, t), jnp.int32),
            jax.ShapeDtypeStruct((bsz, t, LANES), jnp.int32),
            jax.ShapeDtypeStruct((bsz, t, LANES), F32),
        ],
        compiler_params=_cparams("parallel"),
        name="route",
    )(logits_t, tri)


def _expert_kernel(rank_ref, h_ref, wg_ref, wu_ref, wd_ref, y_ref, *, cap):
    e = pl.program_id(0)
    grp, t, _ = h_ref.shape
    slot = lax.broadcasted_iota(jnp.int32, (cap, t), 0)
    xe = []
    for b in range(grp):
        rank = rank_ref[b, pl.ds(e, 1), :]
        onehot = jnp.where(rank == slot, 1.0, 0.0).astype(BF16)
        xe.append(_dot(onehot, h_ref[b]).astype(BF16))
    xe = jnp.concatenate(xe, axis=0)
    g = _dot(xe, wg_ref[0])
    u = _dot(xe, wu_ref[0])
    act = (g * jax.nn.sigmoid(g) * u).astype(BF16)
    y = _dot(act, wd_ref[0])
    for b in range(grp):
        y_ref[b, 0] = y[b * cap:(b + 1) * cap].astype(y_ref.dtype)


def _experts(rank, h2, wg, wu, wd, cap, grp):
    bsz, t, d = h2.shape
    ne, _, f = wg.shape

    def wspec(shape):
        return pl.BlockSpec((1,) + shape, lambda e, b: (e, 0, 0))

    return pl.pallas_call(
        functools.partial(_expert_kernel, cap=cap),
        grid=(ne, bsz // grp),
        in_specs=[
            pl.BlockSpec((grp, ne, t), lambda e, b: (b, 0, 0)),
            pl.BlockSpec((grp, t, d), lambda e, b: (b, 0, 0)),
            wspec((d, f)), wspec((d, f)), wspec((f, d)),
        ],
        out_specs=pl.BlockSpec((grp, 1, cap, d), lambda e, b: (b, e, 0, 0)),
        out_shape=jax.ShapeDtypeStruct((bsz, ne, cap, d), BF16),
        compiler_params=pltpu.CompilerParams(dimension_semantics=("arbitrary", "arbitrary"),
                                             vmem_limit_bytes=EXPERT_VMEM_LIMIT),
        name="experts",
    )(rank, h2, wg, wu, wd)


def _combine_kernel(rank_t_ref, gate_t_ref, y_ref, x1_ref, mod_ref, gf_ref, o_ref, *, cap):
    ne = y_ref.shape[1]
    tm = x1_ref.shape[1]
    slot = lax.broadcasted_iota(jnp.int32, (tm, cap), 1)
    acc = jnp.zeros(x1_ref.shape[1:], F32)
    for e in range(ne):
        r = rank_t_ref[0, :, e:e + 1]
        g = gate_t_ref[0, :, e:e + 1]
        scat = jnp.where(r == slot, g, 0.0).astype(BF16)
        acc = acc + _dot(scat, y_ref[0, e])
    x2 = x1_ref[0] + mod_ref[0, 5:6, :] * acc
    ms = jnp.mean(x2 * x2, axis=-1, keepdims=True)
    o_ref[0] = x2 * lax.rsqrt(ms + EPS) * gf_ref[...]


def _combine(rank_t, gate_t, y, x1, mod3, final_g, cap, tm):
    bsz, t, d = x1.shape
    ne = y.shape[1]
    return pl.pallas_call(
        functools.partial(_combine_kernel, cap=cap),
        grid=(bsz, t // tm),
        in_specs=[
            pl.BlockSpec((1, tm, LANES), lambda i, j: (i, j, 0)),
            pl.BlockSpec((1, tm, LANES), lambda i, j: (i, j, 0)),
            pl.BlockSpec((1, ne, cap, d), lambda i, j: (i, 0, 0, 0)),
            pl.BlockSpec((1, tm, d), lambda i, j: (i, j, 0)),
            pl.BlockSpec((1, 6, d), lambda i, j: (i, 0, 0)),
            _const_spec((1, d)),
        ],
        out_specs=pl.BlockSpec((1, tm, d), lambda i, j: (i, j, 0)),
        out_shape=jax.ShapeDtypeStruct((bsz, t, d), F32),
        compiler_params=_cparams("parallel", "arbitrary"),
        name="combine",
    )(rank_t, gate_t, y, x1, mod3, final_g.reshape(1, d))


def kernel(x, c, ctx, c_ctx, w_mod, b_mod, norm1_g, norm2_g, w_in, b_in, hy_short_w, hy_short_b, hy_skip, filt_w1, filt_b1, filt_w2, filt_b2, filt_w3, filt_freq, na_rpb, w_branch_hy, w_branch_na, w_out, w_router, w_gate, w_up, w_down, final_g):
    depth = w_mod.shape[0]
    bsz, seq, d = x.shape
    hy_width = w_branch_hy.shape[1]
    na_width = w_branch_na.shape[1]
    col_q = 3 * hy_width
    col_k = col_q + na_width
    col_g = col_q + 3 * na_width
    cap = EC_CAPACITY * seq // N_EXPERTS
    rows = seq // GRID_W
    mod_rows = -(-(bsz + 1) // 8) * 8

    hy_blk = HY_BLOCK
    cs = _dft_matrix(hy_blk)
    for i in range(depth):
        assert i == depth - 1, "only the final layer's data flow (context feeds keys/values only) is implemented"
        cc = jnp.zeros((mod_rows, d), F32).at[:bsz].set(c).at[bsz].set(c_ctx)
        mod3 = _modulation(cc, w_mod[i], b_mod[i]).reshape(mod_rows, 6, d)

        qscale = jnp.ones((w_in.shape[2],), F32).at[col_q:col_k].set(NA_HEAD_DIM ** -0.5 * LOG2_E)
        w_in_s = (w_in[i] * qscale).astype(BF16)
        b_in_s = b_in[i] * qscale
        zx = _in_proj(x, mod3, lambda b: b, norm1_g[i], w_in_s, b_in_s, tm=TOKEN_TILE)
        kvc = _in_proj(ctx.reshape(1, -1, d), mod3, lambda b: bsz, norm1_g[i], w_in_s[:, col_k:col_g],
                       b_in_s[col_k:col_g], tm=math.gcd(bsz * ctx.shape[1], TOKEN_TILE)).reshape(bsz, ctx.shape[1], -1)

        spec, nyq = _filter_spectra(seq, hy_width, filt_w1[i], filt_b1[i], filt_w2[i], filt_b2[i], filt_w3[i],
                                    filt_freq[i], hy_skip[i], cs, cb=HY_CHANNEL_BLOCK, blk=hy_blk)
        hy = _hyena(zx, hy_short_w[i], hy_short_b[i], spec, nyq, cs, hy_width, cb=HY_CHANNEL_BLOCK, blk=hy_blk)

        experts_w = (w_gate[i], w_up[i], w_down[i])
        na_grp = math.gcd(bsz, 2)
        if _cast_in_na(experts_w, bsz // na_grp * (rows // NA_ROWS_PER_STEP)):
            na, experts_w = _na(zx, kvc, _na_bias_tables(na_rpb[i], rows), col_q,
                                [w.reshape(-1, w.shape[2]) for w in experts_w], na_grp)
            experts_w = [w2.reshape(w.shape) for w2, w in zip(experts_w, (w_gate[i], w_up[i], w_down[i]))]
        else:
            na, _ = _na(zx, kvc, _na_bias_tables(na_rpb[i], rows), col_q, [], na_grp)
            experts_w = [w.astype(BF16) for w in experts_w]

        x1, h2, logits_t = _merge(hy, na, zx, x, mod3, norm2_g[i], w_branch_hy[i].astype(BF16),
                                  w_branch_na[i].astype(BF16), w_out[i].astype(BF16), w_router[i].T, col_g, tm=TOKEN_TILE)
        rank, rank_t, gate_t = _route(logits_t, cap, grp=math.gcd(bsz, 8))
        y = _experts(rank, h2, *experts_w, cap, grp=math.gcd(bsz, 2))
        x = _combine(rank_t, gate_t, y, x1, mod3, final_g, cap, tm=TOKEN_TILE)
    return x
```

```python
import functools
import math

import jax
import jax.numpy as jnp
import numpy as np
from jax import lax
from jax.experimental import pallas as pl
from jax.experimental.pallas import tpu as pltpu

F32 = jnp.float32
BF16 = jnp.bfloat16

EPS = 1e-6
GRID_W = 64
HY_ORDER = 2
SHORT_CONV = 3
FILT_BANDS = 8
DECAY_TARGET = 1e-2
FAST_DECAY_PCT = 0.3
SLOW_DECAY_PCT = 1.5
HY_BLOCK = 512
NA_HEADS = 8
NA_HEAD_DIM = 64
NA_WIN_H = 8
NA_WIN_W = 16
NA_ROWS_PER_STEP = 4
NA_SLAB_ROWS = NA_ROWS_PER_STEP + NA_WIN_H
N_EXPERTS = 16
EC_CAPACITY = 2

MASK_VALUE = -1e30
LOG2_E = 1.4426950408889634
LANES = 128
BF16_SUBLANES = 16
VMEM_LIMIT = 56 * 1024 * 1024
EXPERT_VMEM_LIMIT = 60 * 1024 * 1024
NA_VMEM_LIMIT = 60 * 1024 * 1024
TOKEN_TILE = 1024
PROJ_COL_CHUNK = 1024
MOD_COL_TILE = 1536
HY_CHANNEL_BLOCK = 256


def _cparams(*sem):
    return pltpu.CompilerParams(dimension_semantics=sem, vmem_limit_bytes=VMEM_LIMIT)


def _const_spec(shape):
    nd = len(shape)
    return pl.BlockSpec(shape, lambda *_: (0,) * nd, pipeline_mode=pl.Buffered(1))


def _dot(a, b):
    return jnp.dot(a, b, preferred_element_type=F32)


def _dot_nt(a, b):
    return lax.dot_general(a, b, (((1,), (1,)), ((), ())), preferred_element_type=F32)


def _mod_kernel(c_ref, w_ref, b_ref, o_ref):
    c = c_ref[...]
    s = c * jax.nn.sigmoid(c)
    o_ref[...] = _dot(s, w_ref[...]) + b_ref[...]


def _modulation(cc, w_mod, b_mod):
    rows, d = cc.shape
    n = w_mod.shape[1]
    tn = MOD_COL_TILE
    return pl.pallas_call(
        _mod_kernel,
        grid=(n // tn,),
        in_specs=[
            pl.BlockSpec((rows, d), lambda j: (0, 0)),
            pl.BlockSpec((d, tn), lambda j: (0, j)),
            pl.BlockSpec((1, tn), lambda j: (0, j)),
        ],
        out_specs=pl.BlockSpec((rows, tn), lambda j: (0, j)),
        out_shape=jax.ShapeDtypeStruct((rows, n), F32),
        compiler_params=_cparams("arbitrary"),
        name="modulation",
    )(cc, w_mod, b_mod.reshape(1, n))


def _rms_mod(x, g, shift, scale):
    ms = jnp.mean(x * x, axis=-1, keepdims=True)
    return (x * lax.rsqrt(ms + EPS) * g) * (1.0 + scale) + shift


def _in_proj_kernel(x_ref, mod_ref, g_ref, w_ref, b_ref, o_ref, *, n_chunk):
    h = _rms_mod(x_ref[0], g_ref[...], mod_ref[0, 0:1, :], mod_ref[0, 1:2, :]).astype(BF16)
    n = w_ref.shape[1]
    for j in range(0, n, n_chunk):
        z = _dot(h, w_ref[:, j:j + n_chunk]) + b_ref[:, j:j + n_chunk]
        o_ref[0, :, j:j + n_chunk] = z.astype(o_ref.dtype)


def _in_proj(x, mod3, mod_row, g, w, b, tm):
    bsz, t, d = x.shape
    n = w.shape[1]
    return pl.pallas_call(
        functools.partial(_in_proj_kernel, n_chunk=min(n, PROJ_COL_CHUNK)),
        grid=(bsz, t // tm),
        in_specs=[
            pl.BlockSpec((1, tm, d), lambda i, j: (i, j, 0)),
            pl.BlockSpec((1, 6, d), lambda i, j: (mod_row(i), 0, 0)),
            _const_spec((1, d)),
            _const_spec((d, n)),
            _const_spec((1, n)),
        ],
        out_specs=pl.BlockSpec((1, tm, n), lambda i, j: (i, j, 0)),
        out_shape=jax.ShapeDtypeStruct((bsz, t, n), BF16),
        compiler_params=_cparams("parallel", "parallel"),
        name="in_proj",
    )(x, mod3, g.reshape(1, d), w, b.reshape(1, n))


def _dft_matrix(blk):
    n = 2 * blk
    i = jnp.arange(blk, dtype=jnp.int32)
    ft = (i[:, None] * i[None, :]) % n
    ang = ft.astype(F32) * (2.0 * math.pi / n)
    return jnp.concatenate([jnp.cos(ang), -jnp.sin(ang)], axis=1).astype(BF16)


def _alt_sign(shape):
    row = lax.broadcasted_iota(jnp.int32, shape, 0)
    return (1 - 2 * (row & 1)).astype(F32)


def _filter_kernel(feats_ref, w1_ref, b1_ref, w2_ref, b2_ref, freq_ref, w3p_ref, w3f_ref, decay_ref, skip_ref,
                   cs_ref, spec_ref, nyq_ref, h_ref, k2_ref, ar_ref, ai_ref, *, blk):
    seq2, cb = k2_ref.shape
    seq = seq2 // 2
    nblk = seq2 // blk
    n = 2 * blk

    @pl.when((pl.program_id(0) == 0) & (pl.program_id(1) == 0))
    def _():
        freq = freq_ref[...]
        h1 = jnp.sin(freq * (_dot(feats_ref[...], w1_ref[...]) + b1_ref[...]))
        h_ref[...] = jnp.sin(freq * (_dot(h1, w2_ref[...]) + b2_ref[...]))

    h = h_ref[...]
    row = lax.broadcasted_iota(jnp.int32, (seq2, cb), 0)
    k2 = jnp.where(row < seq, _dot(h, w3f_ref[...]), _dot(h, w3p_ref[...])) * decay_ref[...]
    k2_ref[...] = jnp.where(row == 0, 0.0, k2)
    sign = _alt_sign((blk, cb))
    for d in range(nblk):
        a = k2_ref[d * blk:(d + 1) * blk, :]
        ab = a.astype(BF16)
        ar_ref[d] = _dot(cs_ref[:, 0:blk], ab)
        ai_ref[d] = _dot(cs_ref[:, blk:n], ab)
    lag0 = nblk // 2
    ar_ref[lag0] = ar_ref[lag0] + skip_ref[0]
    k2_ref[seq:seq + 1, :] = k2_ref[seq:seq + 1, :] + skip_ref[0]
    frow = lax.broadcasted_iota(jnp.int32, (blk, cb), 0)
    wgt = jnp.where(frow == 0, 1.0 / n, 2.0 / n)
    for d in range(1, nblk):
        a0 = k2_ref[(d - 1) * blk:(d - 1) * blk + 1, :]
        spec_ref[0, d - 1, 0] = ((ar_ref[d] + sign * (ar_ref[d - 1] - a0)) * wgt).astype(spec_ref.dtype)
        spec_ref[0, d - 1, 1] = ((ai_ref[d] + sign * ai_ref[d - 1]) * wgt).astype(spec_ref.dtype)
        cur = jnp.sum(sign * k2_ref[d * blk:(d + 1) * blk, :], axis=0, keepdims=True)
        prev = jnp.sum(sign * k2_ref[(d - 1) * blk:d * blk, :], axis=0, keepdims=True)
        nyq_ref[0, d - 1] = (cur + prev - a0) * (1.0 / n)


def _filter_spectra(seq, width, w1, b1, w2, b2, w3, freq, skip, cs, cb, blk):
    pos = jnp.abs(jnp.arange(2 * seq, dtype=F32) - seq)
    t = pos / max(seq - 1, 1)
    omega = 2.0 * math.pi * pos / seq
    bands = jnp.linspace(1e-4, FILT_BANDS - 1, FILT_BANDS, dtype=F32)
    ang = omega[:, None] * bands[None, :]
    feats = jnp.concatenate([t[:, None], jnp.cos(ang), -jnp.sin(ang)], axis=-1)
    emb, hid = w1.shape
    emb_pad = -(-emb // 8) * 8
    feats = jnp.pad(feats, ((0, 0), (0, emb_pad - emb)))
    w1 = jnp.pad(w1, ((0, emb_pad - emb), (0, 0)))
    max_decay = math.log(DECAY_TARGET) / FAST_DECAY_PCT
    min_decay = math.log(DECAY_TARGET) / SLOW_DECAY_PCT
    deltas = jnp.linspace(min_decay, max_decay, width, dtype=F32)
    decay = jnp.exp(-t[:, None] * jnp.abs(deltas)[None, :])
    ncb = width // cb
    nblk = 2 * seq // blk
    return pl.pallas_call(
        functools.partial(_filter_kernel, blk=blk),
        grid=(HY_ORDER, ncb),
        in_specs=[
            _const_spec((2 * seq, emb_pad)),
            _const_spec((emb_pad, hid)),
            _const_spec((1, hid)),
            _const_spec((hid, hid)),
            _const_spec((1, hid)),
            _const_spec((1, hid)),
            pl.BlockSpec((hid, cb), lambda o, c: (0, o * ncb + c)),
            pl.BlockSpec((hid, cb), lambda o, c: (0, (HY_ORDER + o) * ncb + c)),
            pl.BlockSpec((2 * seq, cb), lambda o, c: (0, c)),
            pl.BlockSpec((1, 1, cb), lambda o, c: (o, 0, c)),
            _const_spec((blk, 2 * blk)),
        ],
        out_specs=[
            pl.BlockSpec((1, nblk - 1, 2, blk, cb), lambda o, c: (o, 0, 0, 0, c)),
            pl.BlockSpec((1, nblk - 1, 1, cb), lambda o, c: (o, 0, 0, c)),
        ],
        out_shape=[
            jax.ShapeDtypeStruct((HY_ORDER, nblk - 1, 2, blk, width), BF16),
            jax.ShapeDtypeStruct((HY_ORDER, nblk - 1, 1, width), F32),
        ],
        scratch_shapes=[
            pltpu.VMEM((2 * seq, hid), F32),
            pltpu.VMEM((2 * seq, cb), F32),
            pltpu.VMEM((nblk, blk, cb), F32),
            pltpu.VMEM((nblk, blk, cb), F32),
        ],
        compiler_params=_cparams("arbitrary", "arbitrary"),
        name="hyena_filters",
    )(feats, w1, b1.reshape(1, hid), w2, b2.reshape(1, hid), freq.reshape(1, hid), w3, w3, decay,
      skip.reshape(HY_ORDER, 1, width), cs)


def _hyena_kernel(z_ref, sw_ref, sb_ref, spec_ref, nyq_ref, cs_ref, o_ref,
                  u_ref, g_ref, ub_ref, re_ref, im_ref, yri_ref, *, blk, cb):
    seq, width = o_ref.shape[1], o_ref.shape[2]
    nb = seq // blk
    sign = _alt_sign((blk, cb))

    def short_conv(dst_ref, g, cols):
        zc = slice(g * width + cols.start, g * width + cols.stop)
        z = z_ref[0, :, zc].astype(F32)
        w0, w1, w2 = sw_ref[g, 0:1, cols], sw_ref[g, 1:2, cols], sw_ref[g, 2:3, cols]
        dst_ref[...] = sb_ref[g, :, cols] + pltpu.roll(z, 1, 0) * w0 + z * w1 + pltpu.roll(z, seq - 1, 0) * w2
        dst_ref[0:1, :] = dst_ref[0:1, :] - z_ref[0, seq - 1:seq, zc].astype(F32) * w0
        dst_ref[seq - 1:seq, :] = dst_ref[seq - 1:seq, :] - z_ref[0, 0:1, zc].astype(F32) * w2

    def long_conv(o, g, cols):
        ub_ref[...] = u_ref[...].astype(BF16)
        nyq_in = []
        for j in range(nb):
            rows = slice(j * blk, (j + 1) * blk)
            re_ref[j] = _dot(cs_ref[:, 0:blk], ub_ref[rows, :]).astype(BF16)
            im_ref[j] = _dot(cs_ref[:, blk:2 * blk], ub_ref[rows, :]).astype(BF16)
            nyq_in.append(jnp.sum(sign * u_ref[rows, :], axis=0, keepdims=True))
        short_conv(g_ref, g, cols)
        for i in range(nb):
            yr = yi = nyq = None
            for j in range(nb):
                d = i - j + nb - 1
                gr, gi = spec_ref[o, d, 0, :, cols], spec_ref[o, d, 1, :, cols]
                re, im = re_ref[j], im_ref[j]
                tr = re * gr - im * gi
                ti = re * gi + im * gr
                tn = nyq_in[j] * nyq_ref[o, d, :, cols]
                yr, yi, nyq = (tr, ti, tn) if yr is None else (yr + tr, yi + ti, nyq + tn)
            yri_ref[0:blk, :] = yr
            yri_ref[blk:2 * blk, :] = yi
            rows = slice(i * blk, (i + 1) * blk)
            y = _dot(cs_ref[...], yri_ref[...]) + sign * nyq
            g_ref[rows, :] = g_ref[rows, :] * y
        u_ref[...] = g_ref[...]

    for c in range(0, width, cb):
        cols = slice(c, c + cb)
        short_conv(u_ref, 0, cols)
        long_conv(0, 1, cols)
        long_conv(1, 2, cols)
        o_ref[0, :, cols] = u_ref[...].astype(o_ref.dtype)


def _hyena(zx, short_w, short_b, spec, nyq, cs, width, cb, blk):
    bsz, seq, _ = zx.shape
    nb = seq // blk
    sw = short_w.reshape(SHORT_CONV, 3, width).transpose(1, 0, 2)
    sb = short_b.reshape(3, 1, width)
    return pl.pallas_call(
        functools.partial(_hyena_kernel, blk=blk, cb=cb),
        grid=(bsz,),
        in_specs=[
            pl.BlockSpec((1, seq, 3 * width), lambda i: (i, 0, 0)),
            _const_spec(sw.shape),
            _const_spec(sb.shape),
            _const_spec(spec.shape),
            _const_spec(nyq.shape),
            _const_spec((blk, 2 * blk)),
        ],
        out_specs=pl.BlockSpec((1, seq, width), lambda i: (i, 0, 0)),
        out_shape=jax.ShapeDtypeStruct((bsz, seq, width), BF16),
        scratch_shapes=[
            pltpu.VMEM((seq, cb), F32),
            pltpu.VMEM((seq, cb), F32),
            pltpu.VMEM((seq, cb), BF16),
            pltpu.VMEM((nb, blk, cb), BF16),
            pltpu.VMEM((nb, blk, cb), BF16),
            pltpu.VMEM((2 * blk, cb), BF16),
        ],
        compiler_params=_cparams("parallel"),
        name="hyena",
    )(zx, sw, sb, spec, nyq, cs)


def _na_col_bias_kernel(rpb_ref, sel_ref, mask_ref, rowmask_ref, o_ref):
    acc = jnp.zeros(o_ref.shape, F32) + mask_ref[...] + rowmask_ref[...]
    for ci in range(sel_ref.shape[0]):
        acc = acc + (rpb_ref[:, ci:ci + 1] * LOG2_E) * sel_ref[ci:ci + 1, :]
    o_ref[...] = acc


def _na_bias_tables(rpb, rows):
    rq, sl, half = NA_ROWS_PER_STEP, NA_SLAB_ROWS, NA_WIN_H // 2
    n_ri, n_ci = 2 * NA_WIN_H - 1, 2 * NA_WIN_W - 1
    qc = np.arange(GRID_W)
    kc = np.arange(GRID_W)
    ws = np.clip(qc - NA_WIN_W // 2, 0, GRID_W - NA_WIN_W)
    in_win = (kc[None, :] >= ws[:, None]) & (kc[None, :] < ws[:, None] + NA_WIN_W)
    ci = np.clip(kc[None, :] - qc[:, None] + NA_WIN_W - 1, 0, n_ci - 1)
    sel = ((ci[None] == np.arange(n_ci)[:, None, None]) & in_win[None]).astype(np.float32)
    sel = sel.reshape(n_ci, GRID_W * GRID_W)
    mask = np.where(in_win, 0.0, MASK_VALUE).astype(np.float32).reshape(1, GRID_W * GRID_W)
    rowmask = np.tile(np.where(np.arange(n_ri + 1) == n_ri, MASK_VALUE, 0.0).astype(np.float32), NA_HEADS)
    rpb_x = jnp.pad(rpb, ((0, 0), (0, 1), (0, 0))).reshape(NA_HEADS * (n_ri + 1), n_ci)
    col = pl.pallas_call(
        _na_col_bias_kernel,
        out_shape=jax.ShapeDtypeStruct((NA_HEADS * (n_ri + 1), GRID_W * GRID_W), F32),
        name="na_col_bias",
    )(rpb_x, jnp.asarray(sel), jnp.asarray(mask), jnp.asarray(rowmask.reshape(-1, 1)))
    col = col.reshape(NA_HEADS, n_ri + 1, GRID_W, GRID_W)

    n_steps = rows // rq
    steps = (0, 1, n_steps - 1)
    ri = np.full((len(steps), rq, sl), n_ri, np.int32)
    for c, step in enumerate(steps):
        start = int(np.clip(step * rq - half, 0, rows - sl))
        for i in range(rq):
            r = step * rq + i
            rs = int(np.clip(r - half, 0, rows - NA_WIN_H))
            for j in range(NA_WIN_H):
                ri[c, i, rs - start + j] = rs + j - r + NA_WIN_H - 1
    return pl.pallas_call(
        functools.partial(_na_bias_assemble_kernel, ri=ri),
        out_shape=jax.ShapeDtypeStruct((len(steps), NA_HEADS, rq * GRID_W, sl * GRID_W), F32),
        compiler_params=pltpu.CompilerParams(vmem_limit_bytes=VMEM_LIMIT),
        name="na_bias_tables",
    )(col)


def _na_bias_assemble_kernel(col_ref, o_ref, *, ri):
    n_cfg, rq, sl = ri.shape
    per_tile = LANES // GRID_W
    for c in range(n_cfg):
        for i in range(rq):
            for j in range(0, sl, per_tile):
                tile = jnp.concatenate([col_ref[:, int(ri[c, i, j + jj])] for jj in range(per_tile)], axis=-1)
                o_ref[c, :, i * GRID_W:(i + 1) * GRID_W, j * GRID_W:(j + per_tile) * GRID_W] = tile


def _na_kernel(q_ref, k_ref, v_ref, kvc_ref, bias_ref, *rest, rows):
    n_cast = (len(rest) - 1) // 2
    o_ref = rest[n_cast]
    for src, dst in zip(rest[:n_cast], rest[n_cast + 1:]):
        dst[...] = src[...].astype(dst.dtype)
    rq, sl = NA_ROWS_PER_STEP, NA_SLAB_ROWS
    width = NA_HEADS * NA_HEAD_DIM
    step = pl.program_id(1)
    start = jnp.clip(step * rq - NA_WIN_H // 2, 0, rows - sl)
    start = pl.multiple_of(start * GRID_W, GRID_W)
    nh = LANES // NA_HEAD_DIM
    nq = rq * GRID_W
    lane = lax.broadcasted_iota(jnp.int32, (nq, LANES), 1)
    owns = [(lane >= s * NA_HEAD_DIM) & (lane < (s + 1) * NA_HEAD_DIM) for s in range(nh)]

    for smp, hp in [(a, b) for a in range(q_ref.shape[0]) for b in range(width // LANES)]:
        cols = slice(hp * LANES, (hp + 1) * LANES)
        q2 = q_ref[smp, :, cols]
        k2 = k_ref[smp, pl.ds(start, sl * GRID_W), cols]
        v2 = v_ref[smp, pl.ds(start, sl * GRID_W), cols]
        kc2 = kvc_ref[smp, :, cols]
        vc2 = kvc_ref[smp, :, width + hp * LANES:width + (hp + 1) * LANES]
        qm = jnp.concatenate([jnp.where(own, q2, jnp.zeros_like(q2)) for own in owns], axis=0)
        s_loc = _dot_nt(qm, k2) + bias_ref[0, hp * nh:(hp + 1) * nh].reshape(nh * nq, sl * GRID_W)
        s_ctx = _dot_nt(qm, kc2)
        m = jnp.maximum(jnp.max(s_loc, axis=-1, keepdims=True), jnp.max(s_ctx, axis=-1, keepdims=True))
        p_loc = jnp.exp2(s_loc - m)
        p_ctx = jnp.exp2(s_ctx - m)
        den = jnp.sum(p_loc, axis=-1, keepdims=True) + jnp.sum(p_ctx, axis=-1, keepdims=True)
        o = (_dot(p_loc.astype(BF16), v2) + _dot(p_ctx.astype(BF16), vc2)) / den
        o_ref[smp, :, cols] = sum(jnp.where(owns[s], o[s * nq:(s + 1) * nq], 0.0) for s in range(nh)).astype(o_ref.dtype)


def _na(zx, kvc, bias, col_q, cast_2d, grp):
    bsz, seq, _ = zx.shape
    rows = seq // GRID_W
    rq = NA_ROWS_PER_STEP
    assert rows % rq == 0 and rows >= NA_SLAB_ROWS and rows // rq >= 3
    n_steps = rows // rq
    width = NA_HEADS * NA_HEAD_DIM
    qb = col_q // width
    total = bsz // grp * n_steps
    for w in cast_2d:
        assert w.shape[0] % (total * BF16_SUBLANES) == 0, (w.shape, total)

    def cfg(s):
        return jnp.minimum(s, 1) + jnp.maximum(s - (n_steps - 2), 0)

    def cast_spec(w):
        return pl.BlockSpec((w.shape[0] // total, w.shape[1]), lambda i, s: (i * n_steps + s, 0))

    outs = pl.pallas_call(
        functools.partial(_na_kernel, rows=rows),
        grid=(bsz // grp, n_steps),
        in_specs=[
            pl.BlockSpec((grp, rq * GRID_W, width), lambda i, s: (i, s, qb)),
            pl.BlockSpec((grp, seq, width), lambda i, s: (i, 0, qb + 1)),
            pl.BlockSpec((grp, seq, width), lambda i, s: (i, 0, qb + 2)),
            pl.BlockSpec((grp,) + kvc.shape[1:], lambda i, s: (i, 0, 0)),
            pl.BlockSpec((1,) + bias.shape[1:], lambda i, s: (cfg(s), 0, 0, 0)),
        ] + [cast_spec(w) for w in cast_2d],
        out_specs=[pl.BlockSpec((grp, rq * GRID_W, width), lambda i, s: (i, s, 0))] + [cast_spec(w) for w in cast_2d],
        out_shape=[jax.ShapeDtypeStruct((bsz, seq, width), BF16)]
        + [jax.ShapeDtypeStruct(w.shape, BF16) for w in cast_2d],
        compiler_params=pltpu.CompilerParams(dimension_semantics=("parallel", "arbitrary"),
                                             vmem_limit_bytes=NA_VMEM_LIMIT),
        name="na",
    )(zx, zx, zx, kvc, bias, *cast_2d)
    return outs[0], outs[1:]


def _cast_in_na(weights, total_steps):
    max_block_bytes = 2 << 20
    for w in weights:
        rows = w.shape[0] * w.shape[1]
        if rows % (total_steps * BF16_SUBLANES) or rows // total_steps * w.shape[2] * 4 > max_block_bytes:
            return False
    return True


def _merge_kernel(hy_ref, na_ref, ghy_ref, gna_ref, x_ref, mod_ref, g2_ref, wbh_ref, wbn_ref, wo_ref, wr_ref,
                  x1_ref, h2_ref, lg_ref):
    a = _dot(hy_ref[0], wbh_ref[...])
    b = _dot(na_ref[0], wbn_ref[...])
    m = jax.nn.sigmoid(ghy_ref[0].astype(F32)) * a + jax.nn.sigmoid(gna_ref[0].astype(F32)) * b
    mix = _dot(m.astype(BF16), wo_ref[...])
    x1 = x_ref[0] + mod_ref[0, 2:3, :] * mix
    x1_ref[0] = x1
    h2 = _rms_mod(x1, g2_ref[...], mod_ref[0, 3:4, :], mod_ref[0, 4:5, :])
    h2_ref[0] = h2.astype(h2_ref.dtype)
    lg_ref[0] = _dot_nt(wr_ref[...], h2)


def _merge(hy, na, zx, x, mod3, g2, wbh, wbn, wo, wr_t, col_g, tm):
    bsz, seq, d = x.shape
    gb = col_g // d
    ne = wr_t.shape[0]
    return pl.pallas_call(
        _merge_kernel,
        grid=(bsz, seq // tm),
        in_specs=[
            pl.BlockSpec((1, tm, hy.shape[2]), lambda i, j: (i, j, 0)),
            pl.BlockSpec((1, tm, na.shape[2]), lambda i, j: (i, j, 0)),
            pl.BlockSpec((1, tm, d), lambda i, j: (i, j, gb)),
            pl.BlockSpec((1, tm, d), lambda i, j: (i, j, gb + 1)),
            pl.BlockSpec((1, tm, d), lambda i, j: (i, j, 0)),
            pl.BlockSpec((1, 6, d), lambda i, j: (i, 0, 0)),
            _const_spec((1, d)),
            _const_spec(wbh.shape),
            _const_spec(wbn.shape),
            _const_spec(wo.shape),
            _const_spec(wr_t.shape),
        ],
        out_specs=[
            pl.BlockSpec((1, tm, d), lambda i, j: (i, j, 0)),
            pl.BlockSpec((1, tm, d), lambda i, j: (i, j, 0)),
            pl.BlockSpec((1, ne, tm), lambda i, j: (i, 0, j)),
        ],
        out_shape=[
            jax.ShapeDtypeStruct((bsz, seq, d), F32),
            jax.ShapeDtypeStruct((bsz, seq, d), BF16),
            jax.ShapeDtypeStruct((bsz, ne, seq), F32),
        ],
        compiler_params=_cparams("parallel", "parallel"),
        name="merge",
    )(hy, na, zx, zx, x, mod3, g2.reshape(1, d), wbh, wbn, wo, wr_t)


def _route_kernel(lg_ref, tri_ref, rank_ref, rank_t_ref, gate_t_ref, *, cap):
    grp, ne, t = lg_ref.shape
    lg = lg_ref[...]
    e = jnp.exp(lg - jnp.max(lg, axis=1, keepdims=True))
    aff = (e / jnp.sum(e, axis=1, keepdims=True)).reshape(grp * ne, t)

    def bit_step(i, bits):
        cand = bits | (jnp.int32(1) << (30 - i))
        keep = jnp.sum((aff >= pltpu.bitcast(cand, F32)).astype(jnp.int32), axis=1, keepdims=True) >= cap
        return jnp.where(keep, cand, bits)

    thr = pltpu.bitcast(lax.fori_loop(0, 31, bit_step, jnp.zeros((grp * ne, 1), jnp.int32)), F32)
    above = aff > thr
    tie = aff == thr
    need = cap - jnp.sum(above.astype(jnp.int32), axis=1, keepdims=True)
    tri = tri_ref[...]
    tie_before = _dot(tie.astype(BF16), tri)
    sel = above | (tie & (tie_before < need.astype(F32)))
    sel_before = _dot(sel.astype(BF16), tri)
    rank = jnp.where(sel, sel_before, -1.0)
    gate = jnp.where(sel, aff, 0.0)
    rank_ref[...] = rank.astype(jnp.int32).reshape(grp, ne, t)
    pad_r = jnp.full((LANES - ne, t), -1.0, F32)
    pad_g = jnp.zeros((LANES - ne, t), F32)
    for g in range(grp):
        rows = slice(g * ne, (g + 1) * ne)
        rank_t_ref[g] = jnp.concatenate([rank[rows], pad_r], axis=0).T.astype(jnp.int32)
        gate_t_ref[g] = jnp.concatenate([gate[rows], pad_g], axis=0).T


def _route(logits_t, cap, grp):
    bsz, ne, t = logits_t.shape
    i = jnp.arange(t, dtype=jnp.int32)
    tri = (i[:, None] < i[None, :]).astype(BF16)
    return pl.pallas_call(
        functools.partial(_route_kernel, cap=cap),
        grid=(bsz // grp,),
        in_specs=[
            pl.BlockSpec((grp, ne, t), lambda b: (b, 0, 0)),
            _const_spec((t, t)),
        ],
        out_specs=[
            pl.BlockSpec((grp, ne, t), lambda b: (b, 0, 0)),
            pl.BlockSpec((grp, t, LANES), lambda b: (b, 0, 0)),
            pl.BlockSpec((grp, t, LANES), lambda b: (b, 0, 0)),
        ],
        out_shape=[
            jax.ShapeDtypeStruct((bsz, ne, t), jnp.int32),
            jax.ShapeDtypeStruct((bsz, t, LANES), jnp.int32),
            jax.ShapeDtypeStruct((bsz, t, LANES), F32),
        ],
        compiler_params=_cparams("parallel"),
        name="route",
    )(logits_t, tri)


def _expert_kernel(rank_ref, h_ref, wg_ref, wu_ref, wd_ref, y_ref, *, cap):
    e = pl.program_id(0)
    grp, t, _ = h_ref.shape
    slot = lax.broadcasted_iota(jnp.int32, (cap, t), 0)
    xe = []
    for b in range(grp):
        rank = rank_ref[b, pl.ds(e, 1), :]
        onehot = jnp.where(rank == slot, 1.0, 0.0).astype(BF16)
        xe.append(_dot(onehot, h_ref[b]).astype(BF16))
    xe = jnp.concatenate(xe, axis=0)
    g = _dot(xe, wg_ref[0])
    u = _dot(xe, wu_ref[0])
    act = (g * jax.nn.sigmoid(g) * u).astype(BF16)
    y = _dot(act, wd_ref[0])
    for b in range(grp):
        y_ref[b, 0] = y[b * cap:(b + 1) * cap].astype(y_ref.dtype)


def _experts(rank, h2, wg, wu, wd, cap, grp):
    bsz, t, d = h2.shape
    ne, _, f = wg.shape

    def wspec(shape):
        return pl.BlockSpec((1,) + shape, lambda e, b: (e, 0, 0))

    return pl.pallas_call(
        functools.partial(_expert_kernel, cap=cap),
        grid=(ne, bsz // grp),
        in_specs=[
            pl.BlockSpec((grp, ne, t), lambda e, b: (b, 0, 0)),
            pl.BlockSpec((grp, t, d), lambda e, b: (b, 0, 0)),
            wspec((d, f)), wspec((d, f)), wspec((f, d)),
        ],
        out_specs=pl.BlockSpec((grp, 1, cap, d), lambda e, b: (b, e, 0, 0)),
        out_shape=jax.ShapeDtypeStruct((bsz, ne, cap, d), BF16),
        compiler_params=pltpu.CompilerParams(dimension_semantics=("arbitrary", "arbitrary"),
                                             vmem_limit_bytes=EXPERT_VMEM_LIMIT),
        name="experts",
    )(rank, h2, wg, wu, wd)


def _combine_kernel(rank_t_ref, gate_t_ref, y_ref, x1_ref, mod_ref, gf_ref, o_ref, *, cap):
    ne = y_ref.shape[1]
    tm = x1_ref.shape[1]
    slot = lax.broadcasted_iota(jnp.int32, (tm, cap), 1)
    scat = []
    for e in range(ne):
        r = rank_t_ref[0, :, e:e + 1]
        g = gate_t_ref[0, :, e:e + 1]
        scat.append(jnp.where(r == slot, g, 0.0).astype(BF16))
    acc = _dot(jnp.concatenate(scat, axis=1), y_ref[0].reshape(ne * cap, -1))
    x2 = x1_ref[0] + mod_ref[0, 5:6, :] * acc
    ms = jnp.mean(x2 * x2, axis=-1, keepdims=True)
    o_ref[0] = x2 * lax.rsqrt(ms + EPS) * gf_ref[...]


def _combine(rank_t, gate_t, y, x1, mod3, final_g, cap, tm):
    bsz, t, d = x1.shape
    ne = y.shape[1]
    return pl.pallas_call(
        functools.partial(_combine_kernel, cap=cap),
        grid=(bsz, t // tm),
        in_specs=[
            pl.BlockSpec((1, tm, LANES), lambda i, j: (i, j, 0)),
            pl.BlockSpec((1, tm, LANES), lambda i, j: (i, j, 0)),
            pl.BlockSpec((1, ne, cap, d), lambda i, j: (i, 0, 0, 0)),
            pl.BlockSpec((1, tm, d), lambda i, j: (i, j, 0)),
            pl.BlockSpec((1, 6, d), lambda i, j: (i, 0, 0)),
            _const_spec((1, d)),
        ],
        out_specs=pl.BlockSpec((1, tm, d), lambda i, j: (i, j, 0)),
        out_shape=jax.ShapeDtypeStruct((bsz, t, d), F32),
        compiler_params=_cparams("parallel", "arbitrary"),
        name="combine",
    )(rank_t, gate_t, y, x1, mod3, final_g.reshape(1, d))


def kernel(x, c, ctx, c_ctx, w_mod, b_mod, norm1_g, norm2_g, w_in, b_in, hy_short_w, hy_short_b, hy_skip, filt_w1, filt_b1, filt_w2, filt_b2, filt_w3, filt_freq, na_rpb, w_branch_hy, w_branch_na, w_out, w_router, w_gate, w_up, w_down, final_g):
    depth = w_mod.shape[0]
    bsz, seq, d = x.shape
    hy_width = w_branch_hy.shape[1]
    na_width = w_branch_na.shape[1]
    col_q = 3 * hy_width
    col_k = col_q + na_width
    col_g = col_q + 3 * na_width
    cap = EC_CAPACITY * seq // N_EXPERTS
    rows = seq // GRID_W
    mod_rows = -(-(bsz + 1) // 8) * 8

    hy_blk = HY_BLOCK
    cs = _dft_matrix(hy_blk)
    for i in range(depth):
        assert i == depth - 1, "only the final layer's data flow (context feeds keys/values only) is implemented"
        cc = jnp.zeros((mod_rows, d), F32).at[:bsz].set(c).at[bsz].set(c_ctx)
        mod3 = _modulation(cc, w_mod[i], b_mod[i]).reshape(mod_rows, 6, d)

        qscale = jnp.ones((w_in.shape[2],), F32).at[col_q:col_k].set(NA_HEAD_DIM ** -0.5 * LOG2_E)
        w_in_s = (w_in[i] * qscale).astype(BF16)
        b_in_s = b_in[i] * qscale
        zx = _in_proj(x, mod3, lambda b: b, norm1_g[i], w_in_s, b_in_s, tm=TOKEN_TILE)
        kvc = _in_proj(ctx.reshape(1, -1, d), mod3, lambda b: bsz, norm1_g[i], w_in_s[:, col_k:col_g],
                       b_in_s[col_k:col_g], tm=math.gcd(bsz * ctx.shape[1], TOKEN_TILE)).reshape(bsz, ctx.shape[1], -1)

        spec, nyq = _filter_spectra(seq, hy_width, filt_w1[i], filt_b1[i], filt_w2[i], filt_b2[i], filt_w3[i],
                                    filt_freq[i], hy_skip[i], cs, cb=HY_CHANNEL_BLOCK, blk=hy_blk)
        hy = _hyena(zx, hy_short_w[i], hy_short_b[i], spec, nyq, cs, hy_width, cb=HY_CHANNEL_BLOCK, blk=hy_blk)

        experts_w = (w_gate[i], w_up[i], w_down[i])
        na_grp = math.gcd(bsz, 2)
        if _cast_in_na(experts_w, bsz // na_grp * (rows // NA_ROWS_PER_STEP)):
            na, experts_w = _na(zx, kvc, _na_bias_tables(na_rpb[i], rows), col_q,
                                [w.reshape(-1, w.shape[2]) for w in experts_w], na_grp)
            experts_w = [w2.reshape(w.shape) for w2, w in zip(experts_w, (w_gate[i], w_up[i], w_down[i]))]
        else:
            na, _ = _na(zx, kvc, _na_bias_tables(na_rpb[i], rows), col_q, [], na_grp)
            experts_w = [w.astype(BF16) for w in experts_w]

        x1, h2, logits_t = _merge(hy, na, zx, x, mod3, norm2_g[i], w_branch_hy[i].astype(BF16),
                                  w_branch_na[i].astype(BF16), w_out[i].astype(BF16), w_router[i].T, col_g, tm=TOKEN_TILE)
        rank, rank_t, gate_t = _route(logits_t, cap, grp=math.gcd(bsz, 8))
        y = _experts(rank, h2, *experts_w, cap, grp=math.gcd(bsz, 2))
        x = _combine(rank_t, gate_t, y, x1, mod3, final_g, cap, tm=TOKEN_TILE)
    return x
```

```python
import functools
import math

import jax
import jax.numpy as jnp
import numpy as np
from jax import lax
from jax.experimental import pallas as pl
from jax.experimental.pallas import tpu as pltpu

F32 = jnp.float32
BF16 = jnp.bfloat16

EPS = 1e-6
GRID_W = 64
HY_ORDER = 2
SHORT_CONV = 3
FILT_BANDS = 8
DECAY_TARGET = 1e-2
FAST_DECAY_PCT = 0.3
SLOW_DECAY_PCT = 1.5
HY_BLOCK = 512
NA_HEADS = 8
NA_HEAD_DIM = 64
NA_WIN_H = 8
NA_WIN_W = 16
NA_ROWS_PER_STEP = 4
NA_SLAB_ROWS = NA_ROWS_PER_STEP + NA_WIN_H
N_EXPERTS = 16
EC_CAPACITY = 2

MASK_VALUE = -1e30
LOG2_E = 1.4426950408889634
LANES = 128
BF16_SUBLANES = 16
VMEM_LIMIT = 56 * 1024 * 1024
EXPERT_VMEM_LIMIT = 60 * 1024 * 1024
NA_VMEM_LIMIT = 60 * 1024 * 1024
TOKEN_TILE = 1024
PROJ_COL_CHUNK = 1024
MOD_COL_TILE = 1536
HY_CHANNEL_BLOCK = 256


def _cparams(*sem):
    return pltpu.CompilerParams(dimension_semantics=sem, vmem_limit_bytes=VMEM_LIMIT)


def _const_spec(shape):
    nd = len(shape)
    return pl.BlockSpec(shape, lambda *_: (0,) * nd, pipeline_mode=pl.Buffered(1))


def _dot(a, b):
    return jnp.dot(a, b, preferred_element_type=F32)


def _dot_nt(a, b):
    return lax.dot_general(a, b, (((1,), (1,)), ((), ())), preferred_element_type=F32)


def _mod_kernel(c_ref, w_ref, b_ref, o_ref):
    c = c_ref[...]
    s = c * jax.nn.sigmoid(c)
    o_ref[...] = _dot(s, w_ref[...]) + b_ref[...]


def _modulation(cc, w_mod, b_mod):
    rows, d = cc.shape
    n = w_mod.shape[1]
    tn = MOD_COL_TILE
    return pl.pallas_call(
        _mod_kernel,
        grid=(n // tn,),
        in_specs=[
            pl.BlockSpec((rows, d), lambda j: (0, 0)),
            pl.BlockSpec((d, tn), lambda j: (0, j)),
            pl.BlockSpec((1, tn), lambda j: (0, j)),
        ],
        out_specs=pl.BlockSpec((rows, tn), lambda j: (0, j)),
        out_shape=jax.ShapeDtypeStruct((rows, n), F32),
        compiler_params=_cparams("arbitrary"),
        name="modulation",
    )(cc, w_mod, b_mod.reshape(1, n))


def _rms_mod(x, g, shift, scale):
    ms = jnp.mean(x * x, axis=-1, keepdims=True)
    return (x * lax.rsqrt(ms + EPS) * g) * (1.0 + scale) + shift


def _in_proj_kernel(x_ref, mod_ref, g_ref, w_ref, b_ref, *o_refs, n_chunk, split):
    h = _rms_mod(x_ref[0], g_ref[...], mod_ref[0, 0:1, :], mod_ref[0, 1:2, :]).astype(BF16)
    n = w_ref.shape[1]
    starts = list(range(0, split, n_chunk)) + list(range(split, n, n_chunk))
    for j in starts:
        width = min(n_chunk, (split if j < split else n) - j)
        z = _dot(h, w_ref[:, j:j + width]) + b_ref[:, j:j + width]
        if j < split:
            o_refs[0][0, :, j:j + width] = z.astype(o_refs[0].dtype)
        else:
            o_refs[-1][0, :, j - split:j - split + width] = z.astype(o_refs[-1].dtype)


def _in_proj(x, mod3, mod_row, g, w, b, tm, split=0):
    bsz, t, d = x.shape
    n = w.shape[1]
    widths = [split, n - split] if split else [n]
    outs = pl.pallas_call(
        functools.partial(_in_proj_kernel, n_chunk=min(n, PROJ_COL_CHUNK), split=split),
        grid=(bsz, t // tm),
        in_specs=[
            pl.BlockSpec((1, tm, d), lambda i, j: (i, j, 0)),
            pl.BlockSpec((1, 6, d), lambda i, j: (mod_row(i), 0, 0)),
            _const_spec((1, d)),
            _const_spec((d, n)),
            _const_spec((1, n)),
        ],
        out_specs=[pl.BlockSpec((1, tm, wd), lambda i, j: (i, j, 0)) for wd in widths],
        out_shape=[jax.ShapeDtypeStruct((bsz, t, wd), BF16) for wd in widths],
        compiler_params=_cparams("parallel", "parallel"),
        name="in_proj",
    )(x, mod3, g.reshape(1, d), w, b.reshape(1, n))
    return outs if split else outs[0]


def _dft_matrix(blk):
    n = 2 * blk
    i = jnp.arange(blk, dtype=jnp.int32)
    ft = (i[:, None] * i[None, :]) % n
    ang = ft.astype(F32) * (2.0 * math.pi / n)
    return jnp.concatenate([jnp.cos(ang), -jnp.sin(ang)], axis=1).astype(BF16)


def _alt_sign(shape):
    row = lax.broadcasted_iota(jnp.int32, shape, 0)
    return (1 - 2 * (row & 1)).astype(F32)


def _filter_kernel(feats_ref, w1_ref, b1_ref, w2_ref, b2_ref, freq_ref, w3p_ref, w3f_ref, decay_ref, skip_ref,
                   cs_ref, spec_ref, nyq_ref, h_ref, k2_ref, ar_ref, ai_ref, *, blk):
    seq2, cb = k2_ref.shape
    seq = seq2 // 2
    nblk = seq2 // blk
    n = 2 * blk

    @pl.when((pl.program_id(0) == 0) & (pl.program_id(1) == 0))
    def _():
        freq = freq_ref[...]
        h1 = jnp.sin(freq * (_dot(feats_ref[...], w1_ref[...]) + b1_ref[...]))
        h_ref[...] = jnp.sin(freq * (_dot(h1, w2_ref[...]) + b2_ref[...]))

    h = h_ref[...]
    row = lax.broadcasted_iota(jnp.int32, (seq2, cb), 0)
    k2 = jnp.where(row < seq, _dot(h, w3f_ref[...]), _dot(h, w3p_ref[...])) * decay_ref[...]
    k2_ref[...] = jnp.where(row == 0, 0.0, k2)
    sign = _alt_sign((blk, cb))
    for d in range(nblk):
        a = k2_ref[d * blk:(d + 1) * blk, :]
        ab = a.astype(BF16)
        ar_ref[d] = _dot(cs_ref[:, 0:blk], ab)
        ai_ref[d] = _dot(cs_ref[:, blk:n], ab)
    lag0 = nblk // 2
    ar_ref[lag0] = ar_ref[lag0] + skip_ref[0]
    k2_ref[seq:seq + 1, :] = k2_ref[seq:seq + 1, :] + skip_ref[0]
    frow = lax.broadcasted_iota(jnp.int32, (blk, cb), 0)
    wgt = jnp.where(frow == 0, 1.0 / n, 2.0 / n)
    for d in range(1, nblk):
        a0 = k2_ref[(d - 1) * blk:(d - 1) * blk + 1, :]
        spec_ref[0, d - 1, 0] = ((ar_ref[d] + sign * (ar_ref[d - 1] - a0)) * wgt).astype(spec_ref.dtype)
        spec_ref[0, d - 1, 1] = ((ai_ref[d] + sign * ai_ref[d - 1]) * wgt).astype(spec_ref.dtype)
        cur = jnp.sum(sign * k2_ref[d * blk:(d + 1) * blk, :], axis=0, keepdims=True)
        prev = jnp.sum(sign * k2_ref[(d - 1) * blk:d * blk, :], axis=0, keepdims=True)
        nyq_ref[0, d - 1] = (cur + prev - a0) * (1.0 / n)


def _filter_spectra(seq, width, w1, b1, w2, b2, w3, freq, skip, cs, cb, blk):
    pos = jnp.abs(jnp.arange(2 * seq, dtype=F32) - seq)
    t = pos / max(seq - 1, 1)
    omega = 2.0 * math.pi * pos / seq
    bands = jnp.linspace(1e-4, FILT_BANDS - 1, FILT_BANDS, dtype=F32)
    ang = omega[:, None] * bands[None, :]
    feats = jnp.concatenate([t[:, None], jnp.cos(ang), -jnp.sin(ang)], axis=-1)
    emb, hid = w1.shape
    emb_pad = -(-emb // 8) * 8
    feats = jnp.pad(feats, ((0, 0), (0, emb_pad - emb)))
    w1 = jnp.pad(w1, ((0, emb_pad - emb), (0, 0)))
    max_decay = math.log(DECAY_TARGET) / FAST_DECAY_PCT
    min_decay = math.log(DECAY_TARGET) / SLOW_DECAY_PCT
    deltas = jnp.linspace(min_decay, max_decay, width, dtype=F32)
    decay = jnp.exp(-t[:, None] * jnp.abs(deltas)[None, :])
    ncb = width // cb
    nblk = 2 * seq // blk
    return pl.pallas_call(
        functools.partial(_filter_kernel, blk=blk),
        grid=(HY_ORDER, ncb),
        in_specs=[
            _const_spec((2 * seq, emb_pad)),
            _const_spec((emb_pad, hid)),
            _const_spec((1, hid)),
            _const_spec((hid, hid)),
            _const_spec((1, hid)),
            _const_spec((1, hid)),
            pl.BlockSpec((hid, cb), lambda o, c: (0, o * ncb + c)),
            pl.BlockSpec((hid, cb), lambda o, c: (0, (HY_ORDER + o) * ncb + c)),
            pl.BlockSpec((2 * seq, cb), lambda o, c: (0, c)),
            pl.BlockSpec((1, 1, cb), lambda o, c: (o, 0, c)),
            _const_spec((blk, 2 * blk)),
        ],
        out_specs=[
            pl.BlockSpec((1, nblk - 1, 2, blk, cb), lambda o, c: (o, 0, 0, 0, c)),
            pl.BlockSpec((1, nblk - 1, 1, cb), lambda o, c: (o, 0, 0, c)),
        ],
        out_shape=[
            jax.ShapeDtypeStruct((HY_ORDER, nblk - 1, 2, blk, width), BF16),
            jax.ShapeDtypeStruct((HY_ORDER, nblk - 1, 1, width), F32),
        ],
        scratch_shapes=[
            pltpu.VMEM((2 * seq, hid), F32),
            pltpu.VMEM((2 * seq, cb), F32),
            pltpu.VMEM((nblk, blk, cb), F32),
            pltpu.VMEM((nblk, blk, cb), F32),
        ],
        compiler_params=_cparams("arbitrary", "arbitrary"),
        name="hyena_filters",
    )(feats, w1, b1.reshape(1, hid), w2, b2.reshape(1, hid), freq.reshape(1, hid), w3, w3, decay,
      skip.reshape(HY_ORDER, 1, width), cs)


def _hyena_kernel(z_ref, sw_ref, sb_ref, spec_ref, nyq_ref, cs_ref, o_ref,
                  u_ref, g_ref, ub_ref, re_ref, im_ref, yri_ref, *, blk, cb):
    seq, width = o_ref.shape[1], o_ref.shape[2]
    nb = seq // blk
    sign = _alt_sign((blk, cb))

    def short_conv(dst_ref, g, cols):
        zc = slice(g * width + cols.start, g * width + cols.stop)
        z = z_ref[0, :, zc].astype(F32)
        w0, w1, w2 = sw_ref[g, 0:1, cols], sw_ref[g, 1:2, cols], sw_ref[g, 2:3, cols]
        dst_ref[...] = sb_ref[g, :, cols] + pltpu.roll(z, 1, 0) * w0 + z * w1 + pltpu.roll(z, seq - 1, 0) * w2
        dst_ref[0:1, :] = dst_ref[0:1, :] - z_ref[0, seq - 1:seq, zc].astype(F32) * w0
        dst_ref[seq - 1:seq, :] = dst_ref[seq - 1:seq, :] - z_ref[0, 0:1, zc].astype(F32) * w2

    def long_conv(o, g, cols):
        ub_ref[...] = u_ref[...].astype(BF16)
        nyq_in = []
        for j in range(nb):
            rows = slice(j * blk, (j + 1) * blk)
            re_ref[j] = _dot(cs_ref[:, 0:blk], ub_ref[rows, :]).astype(BF16)
            im_ref[j] = _dot(cs_ref[:, blk:2 * blk], ub_ref[rows, :]).astype(BF16)
            nyq_in.append(jnp.sum(sign * u_ref[rows, :], axis=0, keepdims=True))
        short_conv(g_ref, g, cols)
        for i in range(nb):
            yr = yi = nyq = None
            for j in range(nb):
                d = i - j + nb - 1
                gr, gi = spec_ref[o, d, 0, :, cols], spec_ref[o, d, 1, :, cols]
                re, im = re_ref[j], im_ref[j]
                tr = re * gr - im * gi
                ti = re * gi + im * gr
                tn = nyq_in[j] * nyq_ref[o, d, :, cols]
                yr, yi, nyq = (tr, ti, tn) if yr is None else (yr + tr, yi + ti, nyq + tn)
            yri_ref[0:blk, :] = yr
            yri_ref[blk:2 * blk, :] = yi
            rows = slice(i * blk, (i + 1) * blk)
            y = _dot(cs_ref[...], yri_ref[...]) + sign * nyq
            g_ref[rows, :] = g_ref[rows, :] * y
        u_ref[...] = g_ref[...]

    for c in range(0, width, cb):
        cols = slice(c, c + cb)
        short_conv(u_ref, 0, cols)
        long_conv(0, 1, cols)
        long_conv(1, 2, cols)
        o_ref[0, :, cols] = u_ref[...].astype(o_ref.dtype)


def _hyena(zx, short_w, short_b, spec, nyq, cs, width, cb, blk):
    bsz, seq, _ = zx.shape
    nb = seq // blk
    sw = short_w.reshape(SHORT_CONV, 3, width).transpose(1, 0, 2)
    sb = short_b.reshape(3, 1, width)
    return pl.pallas_call(
        functools.partial(_hyena_kernel, blk=blk, cb=cb),
        grid=(bsz,),
        in_specs=[
            pl.BlockSpec((1, seq, 3 * width), lambda i: (i, 0, 0)),
            _const_spec(sw.shape),
            _const_spec(sb.shape),
            _const_spec(spec.shape),
            _const_spec(nyq.shape),
            _const_spec((blk, 2 * blk)),
        ],
        out_specs=pl.BlockSpec((1, seq, width), lambda i: (i, 0, 0)),
        out_shape=jax.ShapeDtypeStruct((bsz, seq, width), BF16),
        scratch_shapes=[
            pltpu.VMEM((seq, cb), F32),
            pltpu.VMEM((seq, cb), F32),
            pltpu.VMEM((seq, cb), BF16),
            pltpu.VMEM((nb, blk, cb), BF16),
            pltpu.VMEM((nb, blk, cb), BF16),
            pltpu.VMEM((2 * blk, cb), BF16),
        ],
        compiler_params=_cparams("parallel"),
        name="hyena",
    )(zx, sw, sb, spec, nyq, cs)


def _na_col_bias_kernel(rpb_ref, sel_ref, mask_ref, rowmask_ref, o_ref):
    acc = jnp.zeros(o_ref.shape, F32) + mask_ref[...] + rowmask_ref[...]
    for ci in range(sel_ref.shape[0]):
        acc = acc + (rpb_ref[:, ci:ci + 1] * LOG2_E) * sel_ref[ci:ci + 1, :]
    o_ref[...] = acc


def _na_bias_tables(rpb, rows):
    rq, sl, half = NA_ROWS_PER_STEP, NA_SLAB_ROWS, NA_WIN_H // 2
    n_ri, n_ci = 2 * NA_WIN_H - 1, 2 * NA_WIN_W - 1
    qc = np.arange(GRID_W)
    kc = np.arange(GRID_W)
    ws = np.clip(qc - NA_WIN_W // 2, 0, GRID_W - NA_WIN_W)
    in_win = (kc[None, :] >= ws[:, None]) & (kc[None, :] < ws[:, None] + NA_WIN_W)
    ci = np.clip(kc[None, :] - qc[:, None] + NA_WIN_W - 1, 0, n_ci - 1)
    sel = ((ci[None] == np.arange(n_ci)[:, None, None]) & in_win[None]).astype(np.float32)
    sel = sel.reshape(n_ci, GRID_W * GRID_W)
    mask = np.where(in_win, 0.0, MASK_VALUE).astype(np.float32).reshape(1, GRID_W * GRID_W)
    rowmask = np.tile(np.where(np.arange(n_ri + 1) == n_ri, MASK_VALUE, 0.0).astype(np.float32), NA_HEADS)
    rpb_x = jnp.pad(rpb, ((0, 0), (0, 1), (0, 0))).reshape(NA_HEADS * (n_ri + 1), n_ci)
    col = pl.pallas_call(
        _na_col_bias_kernel,
        out_shape=jax.ShapeDtypeStruct((NA_HEADS * (n_ri + 1), GRID_W * GRID_W), F32),
        name="na_col_bias",
    )(rpb_x, jnp.asarray(sel), jnp.asarray(mask), jnp.asarray(rowmask.reshape(-1, 1)))
    col = col.reshape(NA_HEADS, n_ri + 1, GRID_W, GRID_W)

    n_steps = rows // rq
    steps = (0, 1, n_steps - 1)
    ri = np.full((len(steps), rq, sl), n_ri, np.int32)
    for c, step in enumerate(steps):
        start = int(np.clip(step * rq - half, 0, rows - sl))
        for i in range(rq):
            r = step * rq + i
            rs = int(np.clip(r - half, 0, rows - NA_WIN_H))
            for j in range(NA_WIN_H):
                ri[c, i, rs - start + j] = rs + j - r + NA_WIN_H - 1
    return pl.pallas_call(
        functools.partial(_na_bias_assemble_kernel, ri=ri),
        out_shape=jax.ShapeDtypeStruct((len(steps), NA_HEADS, rq * GRID_W, sl * GRID_W), F32),
        compiler_params=pltpu.CompilerParams(vmem_limit_bytes=VMEM_LIMIT),
        name="na_bias_tables",
    )(col)


def _na_bias_assemble_kernel(col_ref, o_ref, *, ri):
    n_cfg, rq, sl = ri.shape
    per_tile = LANES // GRID_W
    for c in range(n_cfg):
        for i in range(rq):
            for j in range(0, sl, per_tile):
                tile = jnp.concatenate([col_ref[:, int(ri[c, i, j + jj])] for jj in range(per_tile)], axis=-1)
                o_ref[c, :, i * GRID_W:(i + 1) * GRID_W, j * GRID_W:(j + per_tile) * GRID_W] = tile


def _na_kernel(q_ref, k_ref, v_ref, kvc_ref, bias_ref, *rest, rows):
    n_cast = (len(rest) - 1) // 2
    o_ref = rest[n_cast]
    for src, dst in zip(rest[:n_cast], rest[n_cast + 1:]):
        dst[...] = src[...].astype(dst.dtype)
    rq, sl = NA_ROWS_PER_STEP, NA_SLAB_ROWS
    width = NA_HEADS * NA_HEAD_DIM
    step = pl.program_id(1)
    start = jnp.clip(step * rq - NA_WIN_H // 2, 0, rows - sl)
    start = pl.multiple_of(start * GRID_W, GRID_W)
    nh = LANES // NA_HEAD_DIM
    nq = rq * GRID_W
    lane = lax.broadcasted_iota(jnp.int32, (nq, LANES), 1)
    owns = [(lane >= s * NA_HEAD_DIM) & (lane < (s + 1) * NA_HEAD_DIM) for s in range(nh)]

    for smp, hp in [(a, b) for a in range(q_ref.shape[0]) for b in range(width // LANES)]:
        cols = slice(hp * LANES, (hp + 1) * LANES)
        q2 = q_ref[smp, :, cols]
        k2 = k_ref[smp, pl.ds(start, sl * GRID_W), cols]
        v2 = v_ref[smp, pl.ds(start, sl * GRID_W), cols]
        kc2 = kvc_ref[smp, :, cols]
        vc2 = kvc_ref[smp, :, width + hp * LANES:width + (hp + 1) * LANES]
        qm = jnp.concatenate([jnp.where(own, q2, jnp.zeros_like(q2)) for own in owns], axis=0)
        s_loc = _dot_nt(qm, k2) + bias_ref[0, hp * nh:(hp + 1) * nh].reshape(nh * nq, sl * GRID_W)
        s_ctx = _dot_nt(qm, kc2)
        m = jnp.maximum(jnp.max(s_loc, axis=-1, keepdims=True), jnp.max(s_ctx, axis=-1, keepdims=True))
        p_loc = jnp.exp2(s_loc - m)
        p_ctx = jnp.exp2(s_ctx - m)
        den = jnp.sum(p_loc, axis=-1, keepdims=True) + jnp.sum(p_ctx, axis=-1, keepdims=True)
        o = (_dot(p_loc.astype(BF16), v2) + _dot(p_ctx.astype(BF16), vc2)) / den
        o_ref[smp, :, cols] = sum(jnp.where(owns[s], o[s * nq:(s + 1) * nq], 0.0) for s in range(nh)).astype(o_ref.dtype)


def _na(zx, kvc, bias, col_q, cast_2d, grp):
    bsz, seq, _ = zx.shape
    rows = seq // GRID_W
    rq = NA_ROWS_PER_STEP
    assert rows % rq == 0 and rows >= NA_SLAB_ROWS and rows // rq >= 3
    n_steps = rows // rq
    width = NA_HEADS * NA_HEAD_DIM
    qb = col_q // width
    total = bsz // grp * n_steps
    for w in cast_2d:
        assert w.shape[0] % (total * BF16_SUBLANES) == 0, (w.shape, total)

    def cfg(s):
        return jnp.minimum(s, 1) + jnp.maximum(s - (n_steps - 2), 0)

    def cast_spec(w):
        return pl.BlockSpec((w.shape[0] // total, w.shape[1]), lambda i, s: (i * n_steps + s, 0))

    outs = pl.pallas_call(
        functools.partial(_na_kernel, rows=rows),
        grid=(bsz // grp, n_steps),
        in_specs=[
            pl.BlockSpec((grp, rq * GRID_W, width), lambda i, s: (i, s, qb)),
            pl.BlockSpec((grp, seq, width), lambda i, s: (i, 0, qb + 1)),
            pl.BlockSpec((grp, seq, width), lambda i, s: (i, 0, qb + 2)),
            pl.BlockSpec((grp,) + kvc.shape[1:], lambda i, s: (i, 0, 0)),
            pl.BlockSpec((1,) + bias.shape[1:], lambda i, s: (cfg(s), 0, 0, 0)),
        ] + [cast_spec(w) for w in cast_2d],
        out_specs=[pl.BlockSpec((grp, rq * GRID_W, width), lambda i, s: (i, s, 0))] + [cast_spec(w) for w in cast_2d],
        out_shape=[jax.ShapeDtypeStruct((bsz, seq, width), BF16)]
        + [jax.ShapeDtypeStruct(w.shape, BF16) for w in cast_2d],
        compiler_params=pltpu.CompilerParams(dimension_semantics=("parallel", "arbitrary"),
                                             vmem_limit_bytes=NA_VMEM_LIMIT),
        name="na",
    )(zx, zx, zx, kvc, bias, *cast_2d)
    return outs[0], outs[1:]


def _cast_in_na(weights, total_steps):
    max_block_bytes = 2 << 20
    for w in weights:
        rows = w.shape[0] * w.shape[1]
        if rows % (total_steps * BF16_SUBLANES) or rows // total_steps * w.shape[2] * 4 > max_block_bytes:
            return False
    return True


def _merge_kernel(hy_ref, na_ref, ghy_ref, gna_ref, x_ref, mod_ref, g2_ref, wbh_ref, wbn_ref, wo_ref, wr_ref,
                  x1_ref, h2_ref, lg_ref):
    a = _dot(hy_ref[0], wbh_ref[...])
    b = _dot(na_ref[0], wbn_ref[...])
    m = jax.nn.sigmoid(ghy_ref[0].astype(F32)) * a + jax.nn.sigmoid(gna_ref[0].astype(F32)) * b
    mix = _dot(m.astype(BF16), wo_ref[...])
    x1 = x_ref[0] + mod_ref[0, 2:3, :] * mix
    x1_ref[0] = x1
    h2 = _rms_mod(x1, g2_ref[...], mod_ref[0, 3:4, :], mod_ref[0, 4:5, :])
    h2_ref[0] = h2.astype(h2_ref.dtype)
    lg_ref[0] = _dot_nt(wr_ref[...], h2)


def _merge(hy, na, zx, x, mod3, g2, wbh, wbn, wo, wr_t, col_g, tm):
    bsz, seq, d = x.shape
    gb = col_g // d
    ne = wr_t.shape[0]
    return pl.pallas_call(
        _merge_kernel,
        grid=(bsz, seq // tm),
        in_specs=[
            pl.BlockSpec((1, tm, hy.shape[2]), lambda i, j: (i, j, 0)),
            pl.BlockSpec((1, tm, na.shape[2]), lambda i, j: (i, j, 0)),
            pl.BlockSpec((1, tm, d), lambda i, j: (i, j, gb)),
            pl.BlockSpec((1, tm, d), lambda i, j: (i, j, gb + 1)),
            pl.BlockSpec((1, tm, d), lambda i, j: (i, j, 0)),
            pl.BlockSpec((1, 6, d), lambda i, j: (i, 0, 0)),
            _const_spec((1, d)),
            _const_spec(wbh.shape),
            _const_spec(wbn.shape),
            _const_spec(wo.shape),
            _const_spec(wr_t.shape),
        ],
        out_specs=[
            pl.BlockSpec((1, tm, d), lambda i, j: (i, j, 0)),
            pl.BlockSpec((1, tm, d), lambda i, j: (i, j, 0)),
            pl.BlockSpec((1, ne, tm), lambda i, j: (i, 0, j)),
        ],
        out_shape=[
            jax.ShapeDtypeStruct((bsz, seq, d), F32),
            jax.ShapeDtypeStruct((bsz, seq, d), BF16),
            jax.ShapeDtypeStruct((bsz, ne, seq), F32),
        ],
        compiler_params=_cparams("parallel", "parallel"),
        name="merge",
    )(hy, na, zx, zx, x, mod3, g2.reshape(1, d), wbh, wbn, wo, wr_t)


def _route_kernel(lg_ref, tri_ref, rank_ref, rank_t_ref, gate_t_ref, *, cap):
    grp, ne, t = lg_ref.shape
    lg = lg_ref[...]
    e = jnp.exp(lg - jnp.max(lg, axis=1, keepdims=True))
    aff = (e / jnp.sum(e, axis=1, keepdims=True)).reshape(grp * ne, t)

    def bit_step(i, bits):
        cand = bits | (jnp.int32(1) << (30 - i))
        keep = jnp.sum((aff >= pltpu.bitcast(cand, F32)).astype(jnp.int32), axis=1, keepdims=True) >= cap
        return jnp.where(keep, cand, bits)

    thr = pltpu.bitcast(lax.fori_loop(0, 31, bit_step, jnp.zeros((grp * ne, 1), jnp.int32)), F32)
    above = aff > thr
    tie = aff == thr
    need = cap - jnp.sum(above.astype(jnp.int32), axis=1, keepdims=True)
    tri = tri_ref[...]
    tie_before = _dot(tie.astype(BF16), tri)
    sel = above | (tie & (tie_before < need.astype(F32)))
    sel_before = _dot(sel.astype(BF16), tri)
    rank = jnp.where(sel, sel_before, -1.0)
    gate = jnp.where(sel, aff, 0.0)
    rank_ref[...] = rank.astype(jnp.int32).reshape(grp, ne, t)
    pad_r = jnp.full((LANES - ne, t), -1.0, F32)
    pad_g = jnp.zeros((LANES - ne, t), F32)
    for g in range(grp):
        rows = slice(g * ne, (g + 1) * ne)
        rank_t_ref[g] = jnp.concatenate([rank[rows], pad_r], axis=0).T.astype(jnp.int32)
        gate_t_ref[g] = jnp.concatenate([gate[rows], pad_g], axis=0).T


def _route(logits_t, cap, grp):
    bsz, ne, t = logits_t.shape
    i = jnp.arange(t, dtype=jnp.int32)
    tri = (i[:, None] < i[None, :]).astype(BF16)
    return pl.pallas_call(
        functools.partial(_route_kernel, cap=cap),
        grid=(bsz // grp,),
        in_specs=[
            pl.BlockSpec((grp, ne, t), lambda b: (b, 0, 0)),
            _const_spec((t, t)),
        ],
        out_specs=[
            pl.BlockSpec((grp, ne, t), lambda b: (b, 0, 0)),
            pl.BlockSpec((grp, t, LANES), lambda b: (b, 0, 0)),
            pl.BlockSpec((grp, t, LANES), lambda b: (b, 0, 0)),
        ],
        out_shape=[
            jax.ShapeDtypeStruct((bsz, ne, t), jnp.int32),
            jax.ShapeDtypeStruct((bsz, t, LANES), jnp.int32),
            jax.ShapeDtypeStruct((bsz, t, LANES), F32),
        ],
        compiler_params=_cparams("parallel"),
        name="route",
    )(logits_t, tri)


def _expert_kernel(rank_ref, h_ref, wg_ref, wu_ref, wd_ref, y_ref, *, cap):
    e = pl.program_id(0)
    grp, t, _ = h_ref.shape
    slot = lax.broadcasted_iota(jnp.int32, (cap, t), 0)
    xe = []
    for b in range(grp):
        rank = rank_ref[b, pl.ds(e, 1), :]
        onehot = jnp.where(rank == slot, 1.0, 0.0).astype(BF16)
        xe.append(_dot(onehot, h_ref[b]).astype(BF16))
    xe = jnp.concatenate(xe, axis=0)
    g = _dot(xe, wg_ref[0])
    u = _dot(xe, wu_ref[0])
    act = (g * jax.nn.sigmoid(g) * u).astype(BF16)
    y = _dot(act, wd_ref[0])
    for b in range(grp):
        y_ref[b, 0] = y[b * cap:(b + 1) * cap].astype(y_ref.dtype)


def _experts(rank, h2, wg, wu, wd, cap, grp):
    bsz, t, d = h2.shape
    ne, _, f = wg.shape

    def wspec(shape):
        return pl.BlockSpec((1,) + shape, lambda e, b: (e, 0, 0))

    return pl.pallas_call(
        functools.partial(_expert_kernel, cap=cap),
        grid=(ne, bsz // grp),
        in_specs=[
            pl.BlockSpec((grp, ne, t), lambda e, b: (b, 0, 0)),
            pl.BlockSpec((grp, t, d), lambda e, b: (b, 0, 0)),
            wspec((d, f)), wspec((d, f)), wspec((f, d)),
        ],
        out_specs=pl.BlockSpec((grp, 1, cap, d), lambda e, b: (b, e, 0, 0)),
        out_shape=jax.ShapeDtypeStruct((bsz, ne, cap, d), BF16),
        compiler_params=pltpu.CompilerParams(dimension_semantics=("arbitrary", "arbitrary"),
                                             vmem_limit_bytes=EXPERT_VMEM_LIMIT),
        name="experts",
    )(rank, h2, wg, wu, wd)


def _combine_kernel(rank_t_ref, gate_t_ref, y_ref, x1_ref, mod_ref, gf_ref, o_ref, *, cap):
    ne = y_ref.shape[1]
    tm = x1_ref.shape[1]
    slot = lax.broadcasted_iota(jnp.int32, (tm, cap), 1)
    scat = []
    for e in range(ne):
        r = rank_t_ref[0, :, e:e + 1]
        g = gate_t_ref[0, :, e:e + 1]
        scat.append(jnp.where(r == slot, g, 0.0).astype(BF16))
    acc = _dot(jnp.concatenate(scat, axis=1), y_ref[0].reshape(ne * cap, -1))
    x2 = x1_ref[0] + mod_ref[0, 5:6, :] * acc
    ms = jnp.mean(x2 * x2, axis=-1, keepdims=True)
    o_ref[0] = x2 * lax.rsqrt(ms + EPS) * gf_ref[...]


def _combine(rank_t, gate_t, y, x1, mod3, final_g, cap, tm):
    bsz, t, d = x1.shape
    ne = y.shape[1]
    return pl.pallas_call(
        functools.partial(_combine_kernel, cap=cap),
        grid=(bsz, t // tm),
        in_specs=[
            pl.BlockSpec((1, tm, LANES), lambda i, j: (i, j, 0)),
            pl.BlockSpec((1, tm, LANES), lambda i, j: (i, j, 0)),
            pl.BlockSpec((1, ne, cap, d), lambda i, j: (i, 0, 0, 0)),
            pl.BlockSpec((1, tm, d), lambda i, j: (i, j, 0)),
            pl.BlockSpec((1, 6, d), lambda i, j: (i, 0, 0)),
            _const_spec((1, d)),
        ],
        out_specs=pl.BlockSpec((1, tm, d), lambda i, j: (i, j, 0)),
        out_shape=jax.ShapeDtypeStruct((bsz, t, d), F32),
        compiler_params=_cparams("parallel", "arbitrary"),
        name="combine",
    )(rank_t, gate_t, y, x1, mod3, final_g.reshape(1, d))


def kernel(x, c, ctx, c_ctx, w_mod, b_mod, norm1_g, norm2_g, w_in, b_in, hy_short_w, hy_short_b, hy_skip, filt_w1, filt_b1, filt_w2, filt_b2, filt_w3, filt_freq, na_rpb, w_branch_hy, w_branch_na, w_out, w_router, w_gate, w_up, w_down, final_g):
    depth = w_mod.shape[0]
    bsz, seq, d = x.shape
    hy_width = w_branch_hy.shape[1]
    na_width = w_branch_na.shape[1]
    col_q = 3 * hy_width
    col_k = col_q + na_width
    col_g = col_q + 3 * na_width
    cap = EC_CAPACITY * seq // N_EXPERTS
    rows = seq // GRID_W
    mod_rows = -(-(bsz + 1) // 8) * 8

    hy_blk = HY_BLOCK
    cs = _dft_matrix(hy_blk)
    for i in range(depth):
        assert i == depth - 1, "only the final layer's data flow (context feeds keys/values only) is implemented"
        cc = jnp.zeros((mod_rows, d), F32).at[:bsz].set(c).at[bsz].set(c_ctx)
        mod3 = _modulation(cc, w_mod[i], b_mod[i]).reshape(mod_rows, 6, d)

        qscale = jnp.ones((w_in.shape[2],), F32).at[col_q:col_k].set(NA_HEAD_DIM ** -0.5 * LOG2_E)
        w_in_s = (w_in[i] * qscale).astype(BF16)
        b_in_s = b_in[i] * qscale
        perm = np.concatenate([np.arange(0, col_q), np.arange(col_g, w_in.shape[2]), np.arange(col_q, col_g)])
        z_hy, zx = _in_proj(x, mod3, lambda b: b, norm1_g[i], w_in_s[:, perm], b_in_s[perm], tm=TOKEN_TILE,
                            split=col_q)
        rest_g, rest_q = 0, w_in.shape[2] - col_g
        kvc = _in_proj(ctx.reshape(1, -1, d), mod3, lambda b: bsz, norm1_g[i], w_in_s[:, col_k:col_g],
                       b_in_s[col_k:col_g], tm=math.gcd(bsz * ctx.shape[1], TOKEN_TILE)).reshape(bsz, ctx.shape[1], -1)

        spec, nyq = _filter_spectra(seq, hy_width, filt_w1[i], filt_b1[i], filt_w2[i], filt_b2[i], filt_w3[i],
                                    filt_freq[i], hy_skip[i], cs, cb=HY_CHANNEL_BLOCK, blk=hy_blk)
        hy = _hyena(z_hy, hy_short_w[i], hy_short_b[i], spec, nyq, cs, hy_width, cb=HY_CHANNEL_BLOCK, blk=hy_blk)

        experts_w = (w_gate[i], w_up[i], w_down[i])
        na_grp = math.gcd(bsz, 2)
        if _cast_in_na(experts_w, bsz // na_grp * (rows // NA_ROWS_PER_STEP)):
            na, experts_w = _na(zx, kvc, _na_bias_tables(na_rpb[i], rows), rest_q,
                                [w.reshape(-1, w.shape[2]) for w in experts_w], na_grp)
            experts_w = [w2.reshape(w.shape) for w2, w in zip(experts_w, (w_gate[i], w_up[i], w_down[i]))]
        else:
            na, _ = _na(zx, kvc, _na_bias_tables(na_rpb[i], rows), rest_q, [], na_grp)
            experts_w = [w.astype(BF16) for w in experts_w]

        x1, h2, logits_t = _merge(hy, na, zx, x, mod3, norm2_g[i], w_branch_hy[i].astype(BF16),
                                  w_branch_na[i].astype(BF16), w_out[i].astype(BF16), w_router[i].T, rest_g, tm=TOKEN_TILE)
        rank, rank_t, gate_t = _route(logits_t, cap, grp=math.gcd(bsz, 8))
        y = _experts(rank, h2, *experts_w, cap, grp=math.gcd(bsz, 2))
        x = _combine(rank_t, gate_t, y, x1, mod3, final_g, cap, tm=TOKEN_TILE)
    return x
```
